```python
import math
import jax, jax.numpy as jnp
from jax import lax
import numpy as np

D_MODEL = 1024
BATCH = 8
SEQ = 2048
DEPTH = 2
DEC_BATCH = 128
DEC_SEQ = 4
PAST_LEN = 16384
PAGE_SIZE = 128

N_MIXERS = 2
N_RET = (DEPTH + 1) // 2
N_GDN = DEPTH // 2
RET_HEADS = 4
RET_DK = D_MODEL // RET_HEADS
RET_DV = 2 * RET_DK
RET_QK_DIM = RET_HEADS * RET_DK
RET_V_DIM = RET_HEADS * RET_DV
RET_IN = 2 * RET_QK_DIM + 2 * RET_V_DIM
RET_CHUNK = 64
ROPE_BASE = 10000.0
GDN_K_HEADS = D_MODEL // 128
GDN_V_HEADS = 2 * GDN_K_HEADS
GDN_DK = 128
GDN_DV = 128
GDN_KEY_DIM = GDN_K_HEADS * GDN_DK
GDN_VAL_DIM = GDN_V_HEADS * GDN_DV
GDN_CONV_DIM = 2 * GDN_KEY_DIM + GDN_VAL_DIM
GDN_CONV_W = 4
GDN_IN = GDN_CONV_DIM + GDN_VAL_DIM + 2 * GDN_V_HEADS
GDN_CHUNK = 64
D_FF = 2816
FFN_CONV_W = 3
EPS = 1e-6

kernel_name = 'hybrid_retention_gdn_convffn_adaln_step'


def ada_rmsnorm(x, g, shift, scale):
    xf = x.astype(jnp.float32)
    xn = xf * lax.rsqrt(jnp.mean(xf * xf, axis=-1, keepdims=True) + EPS) * g.astype(jnp.float32)
    return (xn * (1.0 + scale[:, None, :]) + shift[:, None, :]).astype(x.dtype)


def causal_dwconv(x, buf, w):
    W = w.shape[0]
    T = x.shape[1]
    xp = jnp.concatenate([buf.astype(x.dtype), x], axis=1)
    out = sum(xp[:, i:i + T, :] * w[i].astype(x.dtype) for i in range(W))
    return out, xp[:, T:, :]


def rope(x, pos0):
    T, d = x.shape[2], x.shape[3]
    half = d // 2
    inv_freq = ROPE_BASE ** (-jnp.arange(half, dtype=jnp.float32) / half)
    pos = pos0 + jnp.arange(T, dtype=jnp.float32)
    ang = pos[:, None] * inv_freq[None, :]
    cos, sin = jnp.cos(ang), jnp.sin(ang)
    x1, x2 = x[..., :half], x[..., half:]
    return jnp.concatenate([x1 * cos - x2 * sin, x1 * sin + x2 * cos], axis=-1)


def l2norm(x):
    return x * lax.rsqrt(jnp.sum(x * x, axis=-1, keepdims=True) + EPS)


def to_chunks(a, N, C):
    B, H = a.shape[0], a.shape[1]
    return jnp.moveaxis(a.reshape((B, H, N, C) + a.shape[3:]), 2, 0)


def from_chunks(o):
    N, B, H, C, d = o.shape
    return jnp.moveaxis(o, 0, 2).reshape(B, H, N * C, d)


def retention_chunked(q, k, v, s0):
    B, H, T, dk = q.shape
    C = math.gcd(T, RET_CHUNK)
    N = T // C
    log_gamma = jnp.log(1.0 - 2.0 ** (-5.0 - jnp.arange(H, dtype=jnp.float32)))
    idx = jnp.arange(C, dtype=jnp.float32)
    diff = idx[:, None] - idx[None, :]
    causal = diff >= 0
    decay = jnp.where(causal[None], jnp.exp(jnp.where(causal, diff, 0.0)[None] * log_gamma[:, None, None]), 0.0)
    q_decay = jnp.exp((idx + 1.0)[None, :] * log_gamma[:, None])[None, :, :, None]
    k_decay = jnp.exp((C - 1.0 - idx)[None, :] * log_gamma[:, None])[None, :, :, None]
    chunk_decay = jnp.exp(C * log_gamma)[None, :, None, None]

    def step(S, inp):
        qc, kc, vc = inp
        intra = jnp.einsum('bhid,bhjd->bhij', qc, kc) * decay[None]
        o = jnp.einsum('bhij,bhjv->bhiv', intra, vc) + jnp.einsum('bhid,bhdv->bhiv', qc, S) * q_decay
        S = S * chunk_decay + jnp.einsum('bhjd,bhjv->bhdv', kc * k_decay, vc)
        return S, o

    S, o = lax.scan(step, s0, (to_chunks(q, N, C), to_chunks(k, N, C), to_chunks(v, N, C)))
    return from_chunks(o), S


def retention_mixer(h, pos0, s0, w_in, w_out):
    B, T, _ = h.shape
    proj = jnp.einsum('btd,de->bte', h, w_in).astype(jnp.float32)
    q = proj[..., :RET_QK_DIM]
    k = proj[..., RET_QK_DIM:2 * RET_QK_DIM]
    v = proj[..., 2 * RET_QK_DIM:2 * RET_QK_DIM + RET_V_DIM]
    g = proj[..., 2 * RET_QK_DIM + RET_V_DIM:]
    heads = lambda a, d: a.reshape(B, T, RET_HEADS, d).transpose(0, 2, 1, 3)
    q = rope(heads(q, RET_DK), pos0)
    k = rope(heads(k, RET_DK), pos0) * RET_DK ** -0.5
    v = heads(v, RET_DV)
    o, S = retention_chunked(q, k, v, s0.astype(jnp.float32))
    o = o * lax.rsqrt(jnp.mean(o * o, axis=-1, keepdims=True) + EPS)
    o = o.transpose(0, 2, 1, 3).reshape(B, T, RET_V_DIM) * jax.nn.silu(g)
    return jnp.einsum('bte,ed->btd', o.astype(h.dtype), w_out), S


def gated_delta_chunked(q, k, v, g, beta, s0):
    B, H, T, dk = q.shape
    dv = v.shape[-1]
    C = math.gcd(T, GDN_CHUNK)
    N = T // C
    idx = jnp.arange(C)
    tri_incl = idx[:, None] >= idx[None, :]
    tri_strict = idx[:, None] > idx[None, :]
    eye = jnp.eye(C, dtype=jnp.float32)

    def step(S, inp):
        qc, kc, vc, gc, bc = inp
        G = jnp.cumsum(gc, axis=-1)
        dG = G[..., :, None] - G[..., None, :]
        decay = jnp.exp(jnp.where(tri_incl, dG, -jnp.inf))
        kb = kc * bc[..., None]
        L = jnp.where(tri_strict, jnp.einsum('bhid,bhjd->bhij', kb, kc) * decay, 0.0)
        rhs = jnp.concatenate([vc * bc[..., None], kb * jnp.exp(G)[..., None]], axis=-1)
        sol = lax.linalg.triangular_solve(eye + L, rhs, left_side=True, lower=True, unit_diagonal=True)
        u, w = sol[..., :dv], sol[..., dv:]
        v_new = u - jnp.einsum('bhid,bhdv->bhiv', w, S)
        attn = jnp.einsum('bhid,bhjd->bhij', qc, kc) * decay
        o = jnp.einsum('bhid,bhdv->bhiv', qc * jnp.exp(G)[..., None], S) + jnp.einsum('bhij,bhjv->bhiv', attn, v_new)
        G_last = G[..., -1]
        S = S * jnp.exp(G_last)[..., None, None] + jnp.einsum('bhjd,bhjv->bhdv', kc * jnp.exp(G_last[..., None] - G)[..., None], v_new)
        return S, o

    S, o = lax.scan(step, s0, (to_chunks(q, N, C), to_chunks(k, N, C), to_chunks(v, N, C),
                               to_chunks(g, N, C), to_chunks(beta, N, C)))
    return from_chunks(o), S


def gdn_mixer(h, conv_buf, s0, w_in, w_conv, a_log, dt_bias, norm_w, w_out):
    B, T, _ = h.shape
    proj = jnp.einsum('btd,de->bte', h, w_in)
    mixed = proj[..., :GDN_CONV_DIM]
    z = proj[..., GDN_CONV_DIM:GDN_CONV_DIM + GDN_VAL_DIM]
    b = proj[..., GDN_CONV_DIM + GDN_VAL_DIM:GDN_CONV_DIM + GDN_VAL_DIM + GDN_V_HEADS]
    a = proj[..., GDN_CONV_DIM + GDN_VAL_DIM + GDN_V_HEADS:]
    conv, new_buf = causal_dwconv(mixed, conv_buf, w_conv)
    conv = jax.nn.silu(conv.astype(jnp.float32))
    q = conv[..., :GDN_KEY_DIM].reshape(B, T, GDN_K_HEADS, GDN_DK)
    k = conv[..., GDN_KEY_DIM:2 * GDN_KEY_DIM].reshape(B, T, GDN_K_HEADS, GDN_DK)
    v = conv[..., 2 * GDN_KEY_DIM:].reshape(B, T, GDN_V_HEADS, GDN_DV)
    rep = GDN_V_HEADS // GDN_K_HEADS
    q = jnp.repeat(l2norm(q), rep, axis=2) * GDN_DK ** -0.5
    k = jnp.repeat(l2norm(k), rep, axis=2)
    beta = jax.nn.sigmoid(b.astype(jnp.float32))
    g = -jnp.exp(a_log.astype(jnp.float32)) * jax.nn.softplus(a.astype(jnp.float32) + dt_bias.astype(jnp.float32))
    o, S = gated_delta_chunked(q.transpose(0, 2, 1, 3), k.transpose(0, 2, 1, 3), v.transpose(0, 2, 1, 3),
                               g.transpose(0, 2, 1), beta.transpose(0, 2, 1), s0.astype(jnp.float32))
    o = o.transpose(0, 2, 1, 3)
    zf = z.astype(jnp.float32).reshape(B, T, GDN_V_HEADS, GDN_DV)
    o = o * lax.rsqrt(jnp.mean(o * o, axis=-1, keepdims=True) + EPS) * norm_w.astype(jnp.float32) * jax.nn.silu(zf)
    out = jnp.einsum('bte,ed->btd', o.reshape(B, T, GDN_VAL_DIM).astype(h.dtype), w_out)
    return out, S, new_buf


def conv_ffn(h, buf, w_up, w_dw, b_dw, w_down):
    up = jnp.einsum('btd,df->btf', h, w_up)
    gate_br, val = up[..., :D_FF], up[..., D_FF:]
    conv, new_buf = causal_dwconv(gate_br, buf, w_dw)
    act = jax.nn.silu(conv + b_dw.astype(conv.dtype)) * val
    return jnp.einsum('btf,fd->btd', act, w_down), new_buf


def trunk(x, c, pos0, s_ret, s_gdn, s_gconv, s_fconv, p):
    cs = jax.nn.silu(c.astype(jnp.float32))
    new_ret, new_gdn, new_gconv, new_fconv = [], [], [], []
    for l in range(DEPTH):
        mod = cs @ p['w_ada'][l].astype(jnp.float32) + p['b_ada'][l].astype(jnp.float32)
        sh1, sc1, g1, sh2, sc2, g2 = jnp.split(mod, 6, axis=-1)
        h = ada_rmsnorm(x, p['norm_mix'][l], sh1, sc1)
        i = l // N_MIXERS
        if l % N_MIXERS == 0:
            out, S = retention_mixer(h, pos0, s_ret[i], p['w_ret_in'][i], p['w_ret_out'][i])
            new_ret.append(S.astype(s_ret.dtype))
        else:
            out, S, buf = gdn_mixer(h, s_gconv[i], s_gdn[i], p['w_gdn_in'][i], p['w_gdn_conv'][i],
                                    p['gdn_a_log'][i], p['gdn_dt_bias'][i], p['gdn_norm'][i], p['w_gdn_out'][i])
            new_gdn.append(S.astype(s_gdn.dtype))
            new_gconv.append(buf.astype(s_gconv.dtype))
        x = (x.astype(jnp.float32) + g1[:, None, :] * out.astype(jnp.float32)).astype(x.dtype)
        h = ada_rmsnorm(x, p['norm_ffn'][l], sh2, sc2)
        out, buf = conv_ffn(h, s_fconv[l], p['w_ffn_up'][l], p['w_ffn_dw'][l], p['b_ffn_dw'][l], p['w_ffn_down'][l])
        new_fconv.append(buf.astype(s_fconv.dtype))
        x = (x.astype(jnp.float32) + g2[:, None, :] * out.astype(jnp.float32)).astype(x.dtype)
    mod = cs @ p['w_ada_final'].astype(jnp.float32) + p['b_ada_final'].astype(jnp.float32)
    shf, scf = jnp.split(mod, 2, axis=-1)
    y = ada_rmsnorm(x, p['norm_final'], shf, scf)
    return y, jnp.stack(new_ret), jnp.stack(new_gdn), jnp.stack(new_gconv), jnp.stack(new_fconv)


def setup_inputs(seed: int = 0) -> dict:
    key = jax.random.key(seed)
    ks = jax.random.split(key, 32)
    f32 = jnp.float32
    nrm = lambda k, shape, s: s * jax.random.normal(k, shape, f32)
    inp = {}
    inp['x_prompt'] = nrm(ks[0], (BATCH, SEQ, D_MODEL), 1.0)
    inp['x_sample'] = nrm(ks[1], (DEC_BATCH, DEC_SEQ, D_MODEL), 1.0)
    inp['state_ret'] = nrm(ks[2], (N_RET, DEC_BATCH, RET_HEADS, RET_DK, RET_DV), 0.1)
    inp['state_gdn'] = nrm(ks[3], (N_GDN, DEC_BATCH, GDN_V_HEADS, GDN_DK, GDN_DV), 0.1)
    inp['state_gdn_conv'] = nrm(ks[4], (N_GDN, DEC_BATCH, GDN_CONV_W - 1, GDN_CONV_DIM), 1.0)
    inp['state_ffn_conv'] = nrm(ks[5], (DEPTH, DEC_BATCH, FFN_CONV_W - 1, D_FF), 1.0)
    inp['c_prompt'] = nrm(ks[6], (BATCH, D_MODEL), 1.0)
    inp['c_sample'] = nrm(ks[7], (DEC_BATCH, D_MODEL), 1.0)
    inp['w_ada'] = nrm(ks[8], (DEPTH, D_MODEL, 6 * D_MODEL), 0.5 * D_MODEL ** -0.5)
    inp['b_ada'] = nrm(ks[9], (DEPTH, 6 * D_MODEL), 0.02)
    inp['w_ada_final'] = nrm(ks[10], (D_MODEL, 2 * D_MODEL), 0.5 * D_MODEL ** -0.5)
    inp['b_ada_final'] = nrm(ks[11], (2 * D_MODEL,), 0.02)
    inp['norm_mix'] = 1.0 + nrm(ks[12], (DEPTH, D_MODEL), 0.02)
    inp['norm_ffn'] = 1.0 + nrm(ks[13], (DEPTH, D_MODEL), 0.02)
    inp['norm_final'] = 1.0 + nrm(ks[14], (D_MODEL,), 0.02)
    inp['w_ret_in'] = nrm(ks[15], (N_RET, D_MODEL, RET_IN), D_MODEL ** -0.5)
    inp['w_ret_out'] = nrm(ks[16], (N_RET, RET_V_DIM, D_MODEL), RET_V_DIM ** -0.5)
    inp['w_gdn_in'] = nrm(ks[17], (N_GDN, D_MODEL, GDN_IN), D_MODEL ** -0.5)
    inp['w_gdn_conv'] = nrm(ks[18], (N_GDN, GDN_CONV_W, GDN_CONV_DIM), GDN_CONV_W ** -0.5)
    inp['gdn_a_log'] = jnp.log(jax.random.uniform(ks[19], (N_GDN, GDN_V_HEADS), f32, 1.0, 16.0))
    dt = jnp.exp(jax.random.uniform(ks[20], (N_GDN, GDN_V_HEADS), f32, math.log(1e-3), math.log(1e-1)))
    inp['gdn_dt_bias'] = dt + jnp.log(-jnp.expm1(-dt))
    inp['gdn_norm'] = 1.0 + nrm(ks[21], (N_GDN, GDN_DV), 0.02)
    inp['w_gdn_out'] = nrm(ks[22], (N_GDN, GDN_VAL_DIM, D_MODEL), GDN_VAL_DIM ** -0.5)
    inp['w_ffn_up'] = nrm(ks[23], (DEPTH, D_MODEL, 2 * D_FF), D_MODEL ** -0.5)
    inp['w_ffn_dw'] = nrm(ks[24], (DEPTH, FFN_CONV_W, D_FF), FFN_CONV_W ** -0.5)
    inp['b_ffn_dw'] = nrm(ks[25], (DEPTH, D_FF), 0.02)
    inp['w_ffn_down'] = nrm(ks[26], (DEPTH, D_FF, D_MODEL), D_FF ** -0.5)
    return inp


def reference(x_prompt, x_sample, state_ret, state_gdn, state_gdn_conv, state_ffn_conv,
              c_prompt, c_sample, w_ada, b_ada, w_ada_final, b_ada_final,
              norm_mix, norm_ffn, norm_final, w_ret_in, w_ret_out,
              w_gdn_in, w_gdn_conv, gdn_a_log, gdn_dt_bias, gdn_norm, w_gdn_out,
              w_ffn_up, w_ffn_dw, b_ffn_dw, w_ffn_down):
    p = {'w_ada': w_ada, 'b_ada': b_ada, 'w_ada_final': w_ada_final, 'b_ada_final': b_ada_final,
         'norm_mix': norm_mix, 'norm_ffn': norm_ffn, 'norm_final': norm_final,
         'w_ret_in': w_ret_in, 'w_ret_out': w_ret_out,
         'w_gdn_in': w_gdn_in, 'w_gdn_conv': w_gdn_conv, 'gdn_a_log': gdn_a_log,
         'gdn_dt_bias': gdn_dt_bias, 'gdn_norm': gdn_norm, 'w_gdn_out': w_gdn_out,
         'w_ffn_up': w_ffn_up, 'w_ffn_dw': w_ffn_dw, 'b_ffn_dw': b_ffn_dw, 'w_ffn_down': w_ffn_down}
    z_ret = jnp.zeros((N_RET, BATCH, RET_HEADS, RET_DK, RET_DV), state_ret.dtype)
    z_gdn = jnp.zeros((N_GDN, BATCH, GDN_V_HEADS, GDN_DK, GDN_DV), state_gdn.dtype)
    z_gconv = jnp.zeros((N_GDN, BATCH, GDN_CONV_W - 1, GDN_CONV_DIM), state_gdn_conv.dtype)
    z_fconv = jnp.zeros((DEPTH, BATCH, FFN_CONV_W - 1, D_FF), state_ffn_conv.dtype)
    y_prompt, ret_p, gdn_p, gconv_p, fconv_p = trunk(x_prompt, c_prompt, 0, z_ret, z_gdn, z_gconv, z_fconv, p)
    y_sample, ret_s, gdn_s, gconv_s, fconv_s = trunk(x_sample, c_sample, PAST_LEN, state_ret, state_gdn,
                                                     state_gdn_conv, state_ffn_conv, p)
    return (y_prompt, y_sample, ret_p, gdn_p, gconv_p, fconv_p, ret_s, gdn_s, gconv_s, fconv_s)
```

```python
import functools
import math

import numpy as np
import jax
import jax.numpy as jnp
from jax import lax
from jax.experimental import pallas as pl
from jax.experimental.pallas import tpu as pltpu

F32 = jnp.float32
BF16 = jnp.bfloat16
EPS = 1e-6
ROPE_BASE = 10000.0
PAST_LEN = 16384

RET_HEADS = 4
RET_DK = 256
RET_DV = 512
RET_CHUNK = 128
GDN_K_HEADS = 8
GDN_V_HEADS = 16
GDN_D = 128
GDN_CHUNK = 64
GDN_CONV_W = 4
FFN_CONV_W = 3
SUBLANES = 8
LANES = 128
INV_BASE = 8
VMEM_LIMIT = 48 * 1024 * 1024

_HIGHEST = lax.Precision.HIGHEST


def _cparams(*sem):
    return pltpu.CompilerParams(dimension_semantics=sem, vmem_limit_bytes=VMEM_LIMIT)


def _silu(x):
    return x * jax.nn.sigmoid(x)


def _softplus(x):
    return jnp.maximum(x, 0.0) + jnp.log1p(jnp.exp(-jnp.abs(x)))


def _ada_norm(x, gamma, shift, scale):
    ms = jnp.mean(x * x, axis=-1, keepdims=True)
    xn = x * lax.rsqrt(ms + EPS) * gamma
    return xn * (1.0 + scale) + shift


def _dot(a, b):
    return jnp.dot(a.astype(BF16), b.astype(BF16), preferred_element_type=F32)


def _dot_nt(a, b):
    return lax.dot_general(a.astype(BF16), b.astype(BF16), (((1,), (1,)), ((), ())),
                           preferred_element_type=F32)


def _dot_tn(a, b):
    return lax.dot_general(a.astype(BF16), b.astype(BF16), (((0,), (0,)), ((), ())),
                           preferred_element_type=F32)


def _ada_kernel(c_ref, w_ref, b_ref, o_ref):
    cs = _silu(c_ref[...])
    o_ref[...] = _dot(cs, w_ref[...]) + b_ref[...]


def _ada_mod(c, w, b, tn=1024):
    L, D, N = w.shape
    Mc = c.shape[0]
    return pl.pallas_call(
        _ada_kernel,
        grid=(L, N // tn),
        in_specs=[pl.BlockSpec((Mc, D), lambda l, j: (0, 0)),
                  pl.BlockSpec((None, D, tn), lambda l, j: (l, 0, j)),
                  pl.BlockSpec((None, 1, tn), lambda l, j: (l, 0, j))],
        out_specs=pl.BlockSpec((None, Mc, tn), lambda l, j: (l, 0, j)),
        out_shape=jax.ShapeDtypeStruct((L, Mc, N), F32),
        compiler_params=_cparams("parallel", "parallel"),
        name="ada_mod",
    )(c, w, b.reshape(L, 1, N))


class _Rows:
    def __init__(self, batch, seq, tm):
        self.batch, self.seq, self.tm = batch, seq, tm
        self.M = batch * seq
        self.n_tiles = self.M // tm
        self.per_token_mod = seq < tm
        if self.per_token_mod:
            assert self.n_tiles == 1
        else:
            assert seq % tm == 0
        self.tiles_per_seq = max(seq // tm, 1)

    def mod(self, m):
        if self.per_token_mod:
            return jnp.repeat(m, self.seq, axis=0)[None]
        return m[:, None, :]

    def mod_spec(self, D, n_grid):
        tps = self.tiles_per_seq
        if self.per_token_mod:
            shape, fn = (None, self.tm, D), (lambda i: (0, 0, 0))
        else:
            shape, fn = (None, 1, D), (lambda i: (i // tps, 0, 0))
        if n_grid == 1:
            return pl.BlockSpec(shape, fn)
        return pl.BlockSpec(shape, lambda i, j: fn(i))


def _norm_proj_kernel(x_ref, gam_ref, sh_ref, sc_ref, w_ref, *rest, has_tail):
    if has_tail:
        wt_ref, wtt_ref, o_ref, ot_ref, ott_ref, h_ref = rest
    else:
        o_ref, h_ref = rest

    @pl.when(pl.program_id(1) == 0)
    def _():
        h = _ada_norm(x_ref[...], gam_ref[...], sh_ref[...], sc_ref[...])
        h_ref[...] = h.astype(BF16)
        if has_tail:
            ot_ref[...] = jnp.dot(h, wt_ref[...], precision=_HIGHEST, preferred_element_type=F32)
            ott_ref[...] = lax.dot_general(wtt_ref[...], h, (((1,), (1,)), ((), ())),
                                           precision=_HIGHEST, preferred_element_type=F32)

    o_ref[...] = jnp.dot(h_ref[...], w_ref[...], preferred_element_type=F32)


def _norm_proj(rows, x, gamma, shift, scale, w, w_tail=None, tn=1024):
    M, D = x.shape
    N = w.shape[1]
    tm = rows.tm
    has_tail = w_tail is not None
    in_specs = [pl.BlockSpec((tm, D), lambda i, j: (i, 0)),
                pl.BlockSpec((1, D), lambda i, j: (0, 0)),
                rows.mod_spec(D, 2), rows.mod_spec(D, 2),
                pl.BlockSpec((D, tn), lambda i, j: (0, j))]
    args = [x, gamma.reshape(1, D), shift, scale, w]
    out_specs = [pl.BlockSpec((tm, tn), lambda i, j: (i, j))]
    out_shape = [jax.ShapeDtypeStruct((M, N), F32)]
    if has_tail:
        nt = w_tail.shape[1]
        in_specs += [pl.BlockSpec((D, nt), lambda i, j: (0, 0)),
                     pl.BlockSpec((nt, D), lambda i, j: (0, 0))]
        args += [w_tail, w_tail.T]
        out_specs += [pl.BlockSpec((tm, nt), lambda i, j: (i, 0)),
                      pl.BlockSpec((nt, tm), lambda i, j: (0, i))]
        out_shape += [jax.ShapeDtypeStruct((M, nt), F32), jax.ShapeDtypeStruct((nt, M), F32)]
    res = pl.pallas_call(
        functools.partial(_norm_proj_kernel, has_tail=has_tail),
        grid=(M // tm, N // tn),
        in_specs=in_specs,
        out_specs=out_specs,
        out_shape=out_shape,
        scratch_shapes=[pltpu.VMEM((tm, D), BF16)],
        compiler_params=_cparams("parallel", "arbitrary"),
        name="norm_proj_tail" if has_tail else "norm_proj",
    )(*args)
    return res if has_tail else res[0]


def _out_proj_kernel(o_ref, w_ref, x_ref, g_ref, y_ref):
    y_ref[...] = x_ref[...] + g_ref[...] * _dot(o_ref[...], w_ref[...])


def _out_proj(rows, o, w, x, gate):
    M, K = o.shape
    D = w.shape[1]
    tm = rows.tm
    return pl.pallas_call(
        _out_proj_kernel,
        grid=(M // tm,),
        in_specs=[pl.BlockSpec((tm, K), lambda i: (i, 0)),
                  pl.BlockSpec((K, D), lambda i: (0, 0)),
                  pl.BlockSpec((tm, D), lambda i: (i, 0)),
                  rows.mod_spec(D, 1)],
        out_specs=pl.BlockSpec((tm, D), lambda i: (i, 0)),
        out_shape=jax.ShapeDtypeStruct((M, D), F32),
        compiler_params=_cparams("parallel"),
        name="out_proj",
    )(o, w, x, gate)


def _ffn_kernel(*refs, seq, tiles_per_seq, has_prev, final):
    it = iter(refs)
    x_ref, gam_ref, sh_ref, sc_ref, gate_ref = (next(it) for _ in range(5))
    wg_ref, wv_ref, wdw_ref, bdw_ref, wd_ref = (next(it) for _ in range(5))
    if has_prev:
        p1_ref, p2_ref = next(it), next(it)
    if final:
        gamf_ref, shf_ref, scf_ref = next(it), next(it), next(it)
    y_ref, cst_ref = next(it), next(it)
    h_ref, acc_ref, cbuf_ref, carry_ref = (next(it) for _ in range(4))

    i, f = pl.program_id(0), pl.program_id(1)
    tm, tf = cbuf_ref.shape[0] - SUBLANES, cbuf_ref.shape[1]

    @pl.when(f == 0)
    def _():
        h = _ada_norm(x_ref[...], gam_ref[...], sh_ref[...], sc_ref[...])
        h_ref[...] = h.astype(BF16)
        acc_ref[...] = jnp.zeros_like(acc_ref)

    h = h_ref[...]
    gbr = jnp.dot(h, wg_ref[...], preferred_element_type=F32)
    val = jnp.dot(h, wv_ref[...], preferred_element_type=F32)
    cbuf_ref[SUBLANES:, :] = gbr
    if has_prev:
        cbuf_ref[0:SUBLANES, :] = jnp.zeros((SUBLANES, tf), F32)
        t = lax.broadcasted_iota(jnp.int32, (tm, tf), 0) % seq
        s1 = jnp.where(t == 0, p1_ref[...], cbuf_ref[SUBLANES - 1:SUBLANES - 1 + tm, :])
        s2 = jnp.where(t <= 1, p2_ref[...], cbuf_ref[SUBLANES - 2:SUBLANES - 2 + tm, :])
        cst_ref[...] = gbr
    else:
        first = (i % tiles_per_seq) == 0
        prev = carry_ref[f]
        cbuf_ref[0:SUBLANES, :] = jnp.where(first, jnp.zeros_like(prev), prev)
        s1 = cbuf_ref[SUBLANES - 1:SUBLANES - 1 + tm, :]
        s2 = cbuf_ref[SUBLANES - 2:SUBLANES - 2 + tm, :]
        carry_ref[f] = cbuf_ref[tm:tm + SUBLANES, :]
        cst_ref[...] = cbuf_ref[tm + SUBLANES - (FFN_CONV_W - 1):tm + SUBLANES, :]
    wdw = wdw_ref[...]
    conv = wdw[0:1, :] * s2 + wdw[1:2, :] * s1 + wdw[2:3, :] * gbr + bdw_ref[...]
    act = _silu(conv) * val
    acc_ref[...] += _dot(act, wd_ref[...])

    @pl.when(f == pl.num_programs(1) - 1)
    def _():
        xn = x_ref[...] + gate_ref[...] * acc_ref[...]
        if final:
            xn = _ada_norm(xn, gamf_ref[...], shf_ref[...], scf_ref[...])
        y_ref[...] = xn


def _ffn(rows, x, gamma, shift, scale, gate, w_up, w_dw, b_dw, w_down, prev=None, final_mod=None,
         tf=256):
    M, D = x.shape
    Fd = w_down.shape[0]
    tm = rows.tm
    nf = Fd // tf
    has_prev = prev is not None
    final = final_mod is not None
    in_specs = [pl.BlockSpec((tm, D), lambda i, f: (i, 0)),
                pl.BlockSpec((1, D), lambda i, f: (0, 0)),
                rows.mod_spec(D, 2), rows.mod_spec(D, 2), rows.mod_spec(D, 2),
                pl.BlockSpec((D, tf), lambda i, f: (0, f)),
                pl.BlockSpec((D, tf), lambda i, f: (0, nf + f)),
                pl.BlockSpec((FFN_CONV_W, tf), lambda i, f: (0, f)),
                pl.BlockSpec((1, tf), lambda i, f: (0, f)),
                pl.BlockSpec((tf, D), lambda i, f: (f, 0))]
    args = [x, gamma.reshape(1, D), shift, scale, gate, w_up, w_up, w_dw, b_dw.reshape(1, Fd), w_down]
    if has_prev:
        in_specs += [pl.BlockSpec((tm, tf), lambda i, f: (i, f))] * 2
        args += list(prev)
        cst_spec = pl.BlockSpec((tm, tf), lambda i, f: (i, f))
        cst_shape = jax.ShapeDtypeStruct((M, Fd), F32)
    else:
        tps = rows.tiles_per_seq
        cst_spec = pl.BlockSpec((None, FFN_CONV_W - 1, tf), lambda i, f: (i // tps, 0, f))
        cst_shape = jax.ShapeDtypeStruct((rows.batch, FFN_CONV_W - 1, Fd), F32)
    if final:
        in_specs += [pl.BlockSpec((1, D), lambda i, f: (0, 0)), rows.mod_spec(D, 2), rows.mod_spec(D, 2)]
        args += [final_mod[0].reshape(1, D), final_mod[1], final_mod[2]]
    y, cst = pl.pallas_call(
        functools.partial(_ffn_kernel, seq=rows.seq, tiles_per_seq=rows.tiles_per_seq,
                          has_prev=has_prev, final=final),
        grid=(M // tm, nf),
        in_specs=in_specs,
        out_specs=[pl.BlockSpec((tm, D), lambda i, f: (i, 0)), cst_spec],
        out_shape=[jax.ShapeDtypeStruct((M, D), F32), cst_shape],
        scratch_shapes=[pltpu.VMEM((tm, D), BF16), pltpu.VMEM((tm, D), F32),
                        pltpu.VMEM((tm + SUBLANES, tf), F32), pltpu.VMEM((nf, SUBLANES, tf), F32)],
        compiler_params=_cparams("arbitrary", "arbitrary"),
        name="conv_ffn",
    )(*args)
    return y, cst


def _ret_core_kernel(*refs, c_real, has_state):
    it = iter(refs)
    q_ref, k_ref, v_ref, g_ref, cos_ref, sin_ref = (next(it) for _ in range(6))
    if has_state:
        s0_ref = next(it)
    o_ref, s_ref = next(it), next(it)
    cp = q_ref.shape[0]

    @pl.when(pl.program_id(1) == 0)
    def _():
        if has_state:
            s_ref[...] = s0_ref[...]
        else:
            s_ref[...] = jnp.zeros_like(s_ref)

    cos, sin = cos_ref[...], sin_ref[...]
    ri = lax.broadcasted_iota(jnp.int32, (cp, cp), 0)
    ci = lax.broadcasted_iota(jnp.int32, (cp, cp), 1)
    causal = ri >= ci
    diff = jnp.where(causal, ri - ci, 0).astype(F32)
    row = lax.broadcasted_iota(jnp.int32, (cp, 1), 0).astype(F32)
    half = RET_DK // 2

    def rope(ref, h):
        x1 = ref[:, h * RET_DK:h * RET_DK + half]
        x2 = ref[:, h * RET_DK + half:(h + 1) * RET_DK]
        return jnp.concatenate([x1 * cos - x2 * sin, x1 * sin + x2 * cos], axis=-1)

    for h in range(RET_HEADS):
        lg = math.log(1.0 - 2.0 ** (-5.0 - h))
        decay = jnp.where(causal, jnp.exp(diff * lg), 0.0)
        q_decay = jnp.exp((row + 1.0) * lg)
        k_decay = jnp.exp((c_real - 1.0 - row) * lg)
        chunk_decay = math.exp(c_real * lg)
        q = rope(q_ref, h)
        k = rope(k_ref, h) * (RET_DK ** -0.5)
        v = v_ref[:, h * RET_DV:(h + 1) * RET_DV]
        s = s_ref[h]
        intra = _dot_nt(q, k) * decay
        o = _dot(intra, v) + _dot(q, s) * q_decay
        s_ref[h] = s * chunk_decay + _dot_tn(k * k_decay, v)
        o = o * lax.rsqrt(jnp.mean(o * o, axis=-1, keepdims=True) + EPS)
        o_ref[:, h * RET_DV:(h + 1) * RET_DV] = o * _silu(g_ref[:, h * RET_DV:(h + 1) * RET_DV])


def _ret_core(proj3, cos, sin, s0, c_real, cp):
    B, Tp, _ = proj3.shape
    qk = RET_HEADS * RET_DK
    vd = RET_HEADS * RET_DV
    has_state = s0 is not None
    in_specs = [pl.BlockSpec((None, cp, qk), lambda b, n: (b, n, 0)),
                pl.BlockSpec((None, cp, qk), lambda b, n: (b, n, 1)),
                pl.BlockSpec((None, cp, vd), lambda b, n: (b, n, 1)),
                pl.BlockSpec((None, cp, vd), lambda b, n: (b, n, 2)),
                pl.BlockSpec((cp, RET_DK // 2), lambda b, n: (n, 0)),
                pl.BlockSpec((cp, RET_DK // 2), lambda b, n: (n, 0))]
    args = [proj3, proj3, proj3, proj3, cos, sin]
    s_spec = pl.BlockSpec((None, RET_HEADS, RET_DK, RET_DV), lambda b, n: (b, 0, 0, 0))
    if has_state:
        in_specs.append(s_spec)
        args.append(s0)
    return pl.pallas_call(
        functools.partial(_ret_core_kernel, c_real=c_real, has_state=has_state),
        grid=(B, Tp // cp),
        in_specs=in_specs,
        out_specs=[pl.BlockSpec((None, cp, vd), lambda b, n: (b, n, 0)), s_spec],
        out_shape=[jax.ShapeDtypeStruct((B, Tp, vd), F32),
                   jax.ShapeDtypeStruct((B, RET_HEADS, RET_DK, RET_DV), F32)],
        compiler_params=_cparams("parallel", "arbitrary"),
        name="ret_core",
    )(*args)


def _unit_lower_inverse(l_mat, ri, ci):
    cp = l_mat.shape[0]
    eye = (ri == ci).astype(F32)
    m = -jnp.where((ri // INV_BASE) == (ci // INV_BASE), l_mat, 0.0)
    p = eye + m
    pw = m
    span = 2
    while span < INV_BASE:
        pw = _dot(pw, pw)
        p = p + _dot(p, pw)
        span *= 2
    size = INV_BASE
    while size < cp:
        off = ((ri // (2 * size)) == (ci // (2 * size))) & ((ri // size) != (ci // size))
        lo = jnp.where(off, l_mat, 0.0)
        p = p - _dot(p, _dot(lo, p))
        size *= 2
    return p


def _gdn_core_kernel(*refs, c_real, has_state):
    it = iter(refs)
    mx_ref, z_ref, ba_ref, bat_ref = (next(it) for _ in range(4))
    wc_ref, alr_ref, dtr_ref, alc_ref, dtc_ref, nw_ref = (next(it) for _ in range(6))
    if has_state:
        cs_ref, s0_ref = next(it), next(it)
    o_ref, s_ref = next(it), next(it)
    xp_ref, qkv_ref = next(it), next(it)
    cp = mx_ref.shape[0]
    key = GDN_K_HEADS * GDN_D
    n = pl.program_id(1)

    @pl.when(n == 0)
    def _():
        if has_state:
            s_ref[...] = s0_ref[...]
            xp_ref[0:SUBLANES, :] = cs_ref[...]
        else:
            s_ref[...] = jnp.zeros_like(s_ref)
            xp_ref[0:SUBLANES, :] = jnp.zeros((SUBLANES, xp_ref.shape[1]), F32)

    xp_ref[SUBLANES:, :] = mx_ref[...]
    base = SUBLANES - (GDN_CONV_W - 1)
    wc = wc_ref[...]
    for c in range(xp_ref.shape[1] // LANES):
        sl = slice(c * LANES, (c + 1) * LANES)
        acc = wc[0:1, sl] * xp_ref[base:base + cp, sl]
        for i in range(1, GDN_CONV_W):
            acc = acc + wc[i:i + 1, sl] * xp_ref[base + i:base + i + cp, sl]
        acc = _silu(acc)
        if c < 2 * GDN_K_HEADS:
            acc = acc * lax.rsqrt(jnp.sum(acc * acc, axis=-1, keepdims=True) + EPS)
            if c < GDN_K_HEADS:
                acc = acc * (GDN_D ** -0.5)
        qkv_ref[:, sl] = acc
    xp_ref[0:SUBLANES, :] = xp_ref[cp:cp + SUBLANES, :]

    ri = lax.broadcasted_iota(jnp.int32, (cp, cp), 0)
    ci = lax.broadcasted_iota(jnp.int32, (cp, cp), 1)
    tri_incl = ri >= ci
    tri_strict = ri > ci
    col_live = lax.broadcasted_iota(jnp.int32, (cp, 1), 0) < c_real
    row_live = lax.broadcasted_iota(jnp.int32, (1, cp), 1) < c_real

    ba = ba_ref[...]
    beta_c = jnp.where(col_live, jax.nn.sigmoid(ba), 0.0)
    g_c = jnp.where(col_live, -jnp.exp(alr_ref[...]) * _softplus(ba + dtr_ref[...]), 0.0)
    gsum_c = jnp.dot(tri_incl.astype(F32), g_c, precision=_HIGHEST, preferred_element_type=F32)
    at = bat_ref[GDN_V_HEADS:2 * GDN_V_HEADS, :]
    g_r = jnp.where(row_live, -jnp.exp(alc_ref[...]) * _softplus(at + dtc_ref[...]), 0.0)
    gsum_r = jnp.dot(g_r, (ri <= ci).astype(F32), precision=_HIGHEST, preferred_element_type=F32)

    rep = GDN_V_HEADS // GDN_K_HEADS
    for kh in range(GDN_K_HEADS):
        q = qkv_ref[:, kh * GDN_D:(kh + 1) * GDN_D]
        k = qkv_ref[:, key + kh * GDN_D:key + (kh + 1) * GDN_D]
        kk = _dot_nt(k, k)
        qk = _dot_nt(q, k)
        for r in range(rep):
            h = kh * rep + r
            v = qkv_ref[:, 2 * key + h * GDN_D:2 * key + (h + 1) * GDN_D]
            gc = gsum_c[:, GDN_V_HEADS + h:GDN_V_HEADS + h + 1]
            gr = gsum_r[h:h + 1, :]
            bc = beta_c[:, h:h + 1]
            decay = jnp.exp(jnp.where(tri_incl, gc - gr, -jnp.inf))
            l_mat = jnp.where(tri_strict, kk * bc * decay, 0.0)
            attn = qk * decay
            eg = jnp.exp(gc)
            p = _unit_lower_inverse(l_mat, ri, ci)
            rhs = jnp.concatenate([v * bc, k * (bc * eg)], axis=-1)
            nrm = p - (ri == ci).astype(F32)
            sol = rhs + _dot(nrm, rhs)
            u, w = sol[:, :GDN_D], sol[:, GDN_D:]
            s = s_ref[h]
            v_new = u - _dot(w, s)
            o = _dot(q * eg, s) + _dot(attn, v_new)
            g_last = gc[c_real - 1:c_real, :]
            s_ref[h] = s * jnp.exp(g_last) + _dot_tn(k * jnp.exp(g_last - gc), v_new)
            zh = z_ref[:, h * GDN_D:(h + 1) * GDN_D]
            o = o * lax.rsqrt(jnp.mean(o * o, axis=-1, keepdims=True) + EPS) * nw_ref[...] * _silu(zh)
            o_ref[:, h * GDN_D:(h + 1) * GDN_D] = o


def _gdn_core(proj3, ba3, bat3, w_conv, a_log, dt_bias, norm_w, cs8, s0, c_real, cp):
    B, Tp, _ = proj3.shape
    key = GDN_K_HEADS * GDN_D
    vd = GDN_V_HEADS * GDN_D
    cdim = 2 * key + vd
    nc = Tp // cp
    has_state = s0 is not None
    pad = jnp.zeros((GDN_V_HEADS,), F32)
    lane = lambda a: jnp.concatenate([pad, a, jnp.zeros((LANES - 2 * GDN_V_HEADS,), F32)]).reshape(1, LANES)
    full = lambda shape: pl.BlockSpec(shape, lambda b, n: (0,) * len(shape))
    in_specs = [pl.BlockSpec((None, cp, cdim), lambda b, n: (b, n, 0)),
                pl.BlockSpec((None, cp, vd), lambda b, n: (b, n, cdim // vd)),
                pl.BlockSpec((None, cp, LANES), lambda b, n: (b, n, 0)),
                pl.BlockSpec((None, LANES, cp), lambda b, n: (b * nc + n, 0, 0)),
                full((GDN_CONV_W, cdim)), full((1, LANES)), full((1, LANES)),
                full((GDN_V_HEADS, 1)), full((GDN_V_HEADS, 1)), full((1, GDN_D))]
    args = [proj3, proj3, ba3, bat3, w_conv, lane(a_log), lane(dt_bias),
            a_log.reshape(GDN_V_HEADS, 1), dt_bias.reshape(GDN_V_HEADS, 1), norm_w.reshape(1, GDN_D)]
    s_spec = pl.BlockSpec((None, GDN_V_HEADS, GDN_D, GDN_D), lambda b, n: (b, 0, 0, 0))
    if has_state:
        in_specs += [pl.BlockSpec((None, SUBLANES, cdim), lambda b, n: (b, 0, 0)), s_spec]
        args += [cs8, s0]
    return pl.pallas_call(
        functools.partial(_gdn_core_kernel, c_real=c_real, has_state=has_state),
        grid=(B, nc),
        in_specs=in_specs,
        out_specs=[pl.BlockSpec((None, cp, vd), lambda b, n: (b, n, 0)), s_spec],
        out_shape=[jax.ShapeDtypeStruct((B, Tp, vd), F32),
                   jax.ShapeDtypeStruct((B, GDN_V_HEADS, GDN_D, GDN_D), F32)],
        scratch_shapes=[pltpu.VMEM((cp + SUBLANES, cdim), F32), pltpu.VMEM((cp, cdim), F32)],
        compiler_params=_cparams("parallel", "arbitrary"),
        name="gdn_core",
    )(*args)


def _rope_tables(pos0, t_real, t_pad):
    half = RET_DK // 2
    inv_freq = ROPE_BASE ** (-np.arange(half, dtype=np.float64) / half)
    pos = pos0 + np.arange(t_pad, dtype=np.float64)
    ang = pos[:, None] * inv_freq[None, :]
    live = (np.arange(t_pad) < t_real)[:, None]
    return (jnp.asarray(np.where(live, np.cos(ang), 0.0), F32),
            jnp.asarray(np.where(live, np.sin(ang), 0.0), F32))


def _pad_seq(a, t_pad):
    t = a.shape[1]
    if t == t_pad:
        return a
    return jnp.pad(a, ((0, 0), (0, t_pad - t), (0, 0)))


def _trunk(x, mods, mod_final, pos0, s_ret, s_gdn, s_gconv, s_fconv, p, wb, tm):
    B, T, D = x.shape
    rows = _Rows(B, T, tm)
    M = B * T
    xf = x.reshape(M, D)
    t_pad = max(T, SUBLANES)
    new_ret = new_gdn = new_gconv = None
    new_fconv = []
    for l in range(2):
        sh1, sc1, g1, sh2, sc2, g2 = (rows.mod(m) for m in jnp.split(mods[l], 6, axis=-1))
        if l == 0:
            proj = _norm_proj(rows, xf, p['norm_mix'][l], sh1, sc1, wb['ret_in'])
            cp = min(RET_CHUNK, t_pad)
            cos, sin = _rope_tables(pos0, T, t_pad)
            o3, new_ret = _ret_core(_pad_seq(proj.reshape(B, T, -1), t_pad), cos, sin,
                                    None if s_ret is None else s_ret[0], min(T, cp), cp)
            w_out = wb['ret_out']
        else:
            proj, ba, bat = _norm_proj(rows, xf, p['norm_mix'][l], sh1, sc1, wb['gdn_in'],
                                       w_tail=wb['gdn_tail'])
            cp = min(GDN_CHUNK, t_pad)
            nc = t_pad // cp
            proj3 = proj.reshape(B, T, -1)
            cdim = (2 * GDN_K_HEADS + GDN_V_HEADS) * GDN_D
            new_gconv = proj3[:, T - (GDN_CONV_W - 1):, :cdim]
            ba3 = _pad_seq(ba.reshape(B, T, LANES), t_pad)
            bat3 = jnp.pad(bat.reshape(LANES, B, T), ((0, 0), (0, 0), (0, t_pad - T)))
            bat3 = bat3.reshape(LANES, B, nc, cp).transpose(1, 2, 0, 3).reshape(B * nc, LANES, cp)
            cs8 = None
            if s_gconv is not None:
                cs8 = jnp.pad(s_gconv[0], ((0, 0), (SUBLANES - (GDN_CONV_W - 1), 0), (0, 0)))
            o3, new_gdn = _gdn_core(_pad_seq(proj3, t_pad), ba3, bat3, p['w_gdn_conv'][0],
                                    p['gdn_a_log'][0], p['gdn_dt_bias'][0], p['gdn_norm'][0],
                                    cs8, None if s_gdn is None else s_gdn[0], min(T, cp), cp)
            w_out = wb['gdn_out']
        o = o3[:, :T].reshape(M, -1)
        xf = _out_proj(rows, o, w_out, xf, g1)
        prev = None
        if s_fconv is not None:
            buf = s_fconv[l]
            zero = jnp.zeros_like(buf[:, :1])
            fill = jnp.zeros((B, T - 2, buf.shape[-1]), F32)
            p1 = jnp.concatenate([buf[:, 1:2], zero, fill], axis=1).reshape(M, -1)
            p2 = jnp.concatenate([buf[:, 0:1], buf[:, 1:2], fill], axis=1).reshape(M, -1)
            prev = (p1, p2)
        final_mod = None
        if l == 1:
            shf, scf = (rows.mod(m) for m in jnp.split(mod_final, 2, axis=-1))
            final_mod = (p['norm_final'], shf, scf)
        xf, cst = _ffn(rows, xf, p['norm_ffn'][l], sh2, sc2, g2, wb['ffn_up'][l], p['w_ffn_dw'][l],
                       p['b_ffn_dw'][l], wb['ffn_down'][l], prev=prev, final_mod=final_mod)
        if s_fconv is not None:
            cst = cst.reshape(B, T, -1)[:, T - (FFN_CONV_W - 1):]
        new_fconv.append(cst)
    return (xf.reshape(B, T, D), new_ret[None], new_gdn[None], new_gconv[None], jnp.stack(new_fconv))


def kernel(x_prompt, x_sample, state_ret, state_gdn, state_gdn_conv, state_ffn_conv, c_prompt, c_sample, w_ada, b_ada, w_ada_final, b_ada_final, norm_mix, norm_ffn, norm_final, w_ret_in, w_ret_out, w_gdn_in, w_gdn_conv, gdn_a_log, gdn_dt_bias, gdn_norm, w_gdn_out, w_ffn_up, w_ffn_dw, b_ffn_dw, w_ffn_down):
    p = {'norm_mix': norm_mix, 'norm_ffn': norm_ffn, 'norm_final': norm_final,
         'w_gdn_conv': w_gdn_conv, 'gdn_a_log': gdn_a_log, 'gdn_dt_bias': gdn_dt_bias,
         'gdn_norm': gdn_norm, 'w_ffn_dw': w_ffn_dw, 'b_ffn_dw': b_ffn_dw}
    cdim = (2 * GDN_K_HEADS + GDN_V_HEADS) * GDN_D
    vd = GDN_V_HEADS * GDN_D
    tail = w_gdn_in[0][:, cdim + vd:]
    wb = {'ret_in': w_ret_in[0].astype(BF16), 'ret_out': w_ret_out[0].astype(BF16),
          'gdn_in': w_gdn_in[0][:, :cdim + vd].astype(BF16), 'gdn_out': w_gdn_out[0].astype(BF16),
          'gdn_tail': jnp.pad(tail, ((0, 0), (0, LANES - tail.shape[1]))),
          'ffn_up': w_ffn_up.astype(BF16), 'ffn_down': w_ffn_down.astype(BF16)}

    bp = c_prompt.shape[0]
    c_all = jnp.concatenate([c_prompt, c_sample], axis=0)
    mods = _ada_mod(c_all, w_ada, b_ada)
    mod_final = _ada_mod(c_all, w_ada_final[None], b_ada_final[None])[0]

    out_p = _trunk(x_prompt, mods[:, :bp], mod_final[:bp], 0, None, None, None, None, p, wb, tm=512)
    out_s = _trunk(x_sample, mods[:, bp:], mod_final[bp:], PAST_LEN, state_ret, state_gdn,
                   state_gdn_conv, state_ffn_conv, p, wb,
                   tm=x_sample.shape[0] * x_sample.shape[1])
    y_p, ret_p, gdn_p, gconv_p, fconv_p = out_p
    y_s, ret_s, gdn_s, gconv_s, fconv_s = out_s
    return (y_p, y_s, ret_p, gdn_p, gconv_p, fconv_p, ret_s, gdn_s, gconv_s, fconv_s)
```

```python
import functools
import math

import numpy as np
import jax
import jax.numpy as jnp
from jax import lax
from jax.experimental import pallas as pl
from jax.experimental.pallas import tpu as pltpu

F32 = jnp.float32
BF16 = jnp.bfloat16
EPS = 1e-6
ROPE_BASE = 10000.0
PAST_LEN = 16384

RET_HEADS = 4
RET_DK = 256
RET_DV = 512
RET_CHUNK = 128
GDN_K_HEADS = 8
GDN_V_HEADS = 16
GDN_D = 128
GDN_CHUNK = 64
GDN_CONV_W = 4
FFN_CONV_W = 3
SUBLANES = 8
LANES = 128
INV_BASE = 8
GROUP_ROWS = 128
VMEM_LIMIT = 48 * 1024 * 1024

_HIGHEST = lax.Precision.HIGHEST


def _cparams(*sem):
    return pltpu.CompilerParams(dimension_semantics=sem, vmem_limit_bytes=VMEM_LIMIT)


def _silu(x):
    return x * jax.nn.sigmoid(x)


def _softplus(x):
    return jnp.maximum(x, 0.0) + jnp.log1p(jnp.exp(-jnp.abs(x)))


def _ada_norm(x, gamma, shift, scale):
    ms = jnp.mean(x * x, axis=-1, keepdims=True)
    xn = x * lax.rsqrt(ms + EPS) * gamma
    return xn * (1.0 + scale) + shift


def _dot(a, b):
    return jnp.dot(a.astype(BF16), b.astype(BF16), preferred_element_type=F32)


def _dot_nt(a, b):
    return lax.dot_general(a.astype(BF16), b.astype(BF16), (((1,), (1,)), ((), ())),
                           preferred_element_type=F32)


def _dot_tn(a, b):
    return lax.dot_general(a.astype(BF16), b.astype(BF16), (((0,), (0,)), ((), ())),
                           preferred_element_type=F32)


def _ada_kernel(c_ref, w_ref, b_ref, o_ref):
    cs = _silu(c_ref[...])
    o_ref[...] = _dot(cs, w_ref[...]) + b_ref[...]


def _ada_mod(c, w, b, tn=1024):
    L, D, N = w.shape
    Mc = c.shape[0]
    return pl.pallas_call(
        _ada_kernel,
        grid=(L, N // tn),
        in_specs=[pl.BlockSpec((Mc, D), lambda l, j: (0, 0)),
                  pl.BlockSpec((None, D, tn), lambda l, j: (l, 0, j)),
                  pl.BlockSpec((None, 1, tn), lambda l, j: (l, 0, j))],
        out_specs=pl.BlockSpec((None, Mc, tn), lambda l, j: (l, 0, j)),
        out_shape=jax.ShapeDtypeStruct((L, Mc, N), F32),
        compiler_params=_cparams("parallel", "parallel"),
        name="ada_mod",
    )(c, w, b.reshape(L, 1, N))


class _Rows:
    def __init__(self, batch, seq, tm):
        self.batch, self.seq, self.tm = batch, seq, tm
        self.M = batch * seq
        self.n_tiles = self.M // tm
        self.per_token_mod = seq < tm
        if self.per_token_mod:
            assert self.n_tiles == 1
        else:
            assert seq % tm == 0
        self.tiles_per_seq = max(seq // tm, 1)

    def mod(self, m):
        if self.per_token_mod:
            return jnp.repeat(m, self.seq, axis=0)[None]
        return m[:, None, :]

    def mod_spec(self, D, n_grid):
        tps = self.tiles_per_seq
        if self.per_token_mod:
            shape, fn = (None, self.tm, D), (lambda i: (0, 0, 0))
        else:
            shape, fn = (None, 1, D), (lambda i: (i // tps, 0, 0))
        if n_grid == 1:
            return pl.BlockSpec(shape, fn)
        return pl.BlockSpec(shape, lambda i, j: fn(i))


def _norm_proj_kernel(x_ref, gam_ref, sh_ref, sc_ref, w_ref, *rest, has_tail):
    if has_tail:
        wtt_ref, o_ref, ott_ref, h_ref = rest
    else:
        o_ref, h_ref = rest

    @pl.when(pl.program_id(1) == 0)
    def _():
        h = _ada_norm(x_ref[...], gam_ref[...], sh_ref[...], sc_ref[...])
        h_ref[...] = h.astype(BF16)
        if has_tail:
            ott_ref[...] = lax.dot_general(wtt_ref[...], h, (((1,), (1,)), ((), ())),
                                           precision=_HIGHEST, preferred_element_type=F32)

    o_ref[...] = jnp.dot(h_ref[...], w_ref[...], preferred_element_type=F32)


def _norm_proj(rows, x, gamma, shift, scale, w, w_tail=None, tn=1024):
    M, D = x.shape
    N = w.shape[1]
    tm = rows.tm
    has_tail = w_tail is not None
    in_specs = [pl.BlockSpec((tm, D), lambda i, j: (i, 0)),
                pl.BlockSpec((1, D), lambda i, j: (0, 0)),
                rows.mod_spec(D, 2), rows.mod_spec(D, 2),
                pl.BlockSpec((D, tn), lambda i, j: (0, j))]
    args = [x, gamma.reshape(1, D), shift, scale, w]
    out_specs = [pl.BlockSpec((tm, tn), lambda i, j: (i, j))]
    out_shape = [jax.ShapeDtypeStruct((M, N), F32)]
    if has_tail:
        nt = w_tail.shape[1]
        in_specs += [pl.BlockSpec((nt, D), lambda i, j: (0, 0))]
        args += [w_tail.T]
        out_specs += [pl.BlockSpec((nt, tm), lambda i, j: (0, i))]
        out_shape += [jax.ShapeDtypeStruct((nt, M), F32)]
    res = pl.pallas_call(
        functools.partial(_norm_proj_kernel, has_tail=has_tail),
        grid=(M // tm, N // tn),
        in_specs=in_specs,
        out_specs=out_specs,
        out_shape=out_shape,
        scratch_shapes=[pltpu.VMEM((tm, D), BF16)],
        compiler_params=_cparams("parallel", "arbitrary"),
        name="norm_proj_tail" if has_tail else "norm_proj",
    )(*args)
    return res if has_tail else res[0]


def _out_proj_kernel(o_ref, w_ref, x_ref, g_ref, y_ref):
    y_ref[...] = x_ref[...] + g_ref[...] * _dot(o_ref[...], w_ref[...])


def _out_proj(rows, o, w, x, gate):
    M, K = o.shape
    D = w.shape[1]
    tm = rows.tm
    return pl.pallas_call(
        _out_proj_kernel,
        grid=(M // tm,),
        in_specs=[pl.BlockSpec((tm, K), lambda i: (i, 0)),
                  pl.BlockSpec((K, D), lambda i: (0, 0)),
                  pl.BlockSpec((tm, D), lambda i: (i, 0)),
                  rows.mod_spec(D, 1)],
        out_specs=pl.BlockSpec((tm, D), lambda i: (i, 0)),
        out_shape=jax.ShapeDtypeStruct((M, D), F32),
        compiler_params=_cparams("parallel"),
        name="out_proj",
    )(o, w, x, gate)


def _ffn_kernel(*refs, seq, tiles_per_seq, has_prev, final):
    it = iter(refs)
    x_ref, gam_ref, sh_ref, sc_ref, gate_ref = (next(it) for _ in range(5))
    wg_ref, wv_ref, wdw_ref, bdw_ref, wd_ref = (next(it) for _ in range(5))
    if has_prev:
        p1_ref, p2_ref = next(it), next(it)
    if final:
        gamf_ref, shf_ref, scf_ref = next(it), next(it), next(it)
    y_ref, cst_ref = next(it), next(it)
    h_ref, acc_ref, cbuf_ref, carry_ref = (next(it) for _ in range(4))

    i, f = pl.program_id(0), pl.program_id(1)
    tm, tf = cbuf_ref.shape[0] - SUBLANES, cbuf_ref.shape[1]

    @pl.when(f == 0)
    def _():
        h = _ada_norm(x_ref[...], gam_ref[...], sh_ref[...], sc_ref[...])
        h_ref[...] = h.astype(BF16)
        acc_ref[...] = jnp.zeros_like(acc_ref)

    h = h_ref[...]
    gbr = jnp.dot(h, wg_ref[...], preferred_element_type=F32)
    val = jnp.dot(h, wv_ref[...], preferred_element_type=F32)
    cbuf_ref[SUBLANES:, :] = gbr
    if has_prev:
        cbuf_ref[0:SUBLANES, :] = jnp.zeros((SUBLANES, tf), F32)
        t = lax.broadcasted_iota(jnp.int32, (tm, tf), 0) % seq
        s1 = jnp.where(t == 0, p1_ref[...], cbuf_ref[SUBLANES - 1:SUBLANES - 1 + tm, :])
        s2 = jnp.where(t <= 1, p2_ref[...], cbuf_ref[SUBLANES - 2:SUBLANES - 2 + tm, :])
        cst_ref[...] = gbr
    else:
        first = (i % tiles_per_seq) == 0
        prev = carry_ref[f]
        cbuf_ref[0:SUBLANES, :] = jnp.where(first, jnp.zeros_like(prev), prev)
        s1 = cbuf_ref[SUBLANES - 1:SUBLANES - 1 + tm, :]
        s2 = cbuf_ref[SUBLANES - 2:SUBLANES - 2 + tm, :]
        carry_ref[f] = cbuf_ref[tm:tm + SUBLANES, :]
        cst_ref[...] = cbuf_ref[tm + SUBLANES - (FFN_CONV_W - 1):tm + SUBLANES, :]
    wdw = wdw_ref[...]
    conv = wdw[0:1, :] * s2 + wdw[1:2, :] * s1 + wdw[2:3, :] * gbr + bdw_ref[...]
    act = _silu(conv) * val
    acc_ref[...] += _dot(act, wd_ref[...])

    @pl.when(f == pl.num_programs(1) - 1)
    def _():
        xn = x_ref[...] + gate_ref[...] * acc_ref[...]
        if final:
            xn = _ada_norm(xn, gamf_ref[...], shf_ref[...], scf_ref[...])
        y_ref[...] = xn


def _ffn(rows, x, gamma, shift, scale, gate, w_up, w_dw, b_dw, w_down, prev=None, final_mod=None,
         tf=256):
    M, D = x.shape
    Fd = w_down.shape[0]
    tm = rows.tm
    nf = Fd // tf
    has_prev = prev is not None
    final = final_mod is not None
    in_specs = [pl.BlockSpec((tm, D), lambda i, f: (i, 0)),
                pl.BlockSpec((1, D), lambda i, f: (0, 0)),
                rows.mod_spec(D, 2), rows.mod_spec(D, 2), rows.mod_spec(D, 2),
                pl.BlockSpec((D, tf), lambda i, f: (0, f)),
                pl.BlockSpec((D, tf), lambda i, f: (0, nf + f)),
                pl.BlockSpec((FFN_CONV_W, tf), lambda i, f: (0, f)),
                pl.BlockSpec((1, tf), lambda i, f: (0, f)),
                pl.BlockSpec((tf, D), lambda i, f: (f, 0))]
    args = [x, gamma.reshape(1, D), shift, scale, gate, w_up, w_up, w_dw, b_dw.reshape(1, Fd), w_down]
    if has_prev:
        in_specs += [pl.BlockSpec((tm, tf), lambda i, f: (i, f))] * 2
        args += list(prev)
        cst_spec = pl.BlockSpec((tm, tf), lambda i, f: (i, f))
        cst_shape = jax.ShapeDtypeStruct((M, Fd), F32)
    else:
        cst_spec = pl.BlockSpec((None, FFN_CONV_W - 1, tf), lambda i, f: (i, 0, f))
        cst_shape = jax.ShapeDtypeStruct((rows.n_tiles, FFN_CONV_W - 1, Fd), F32)
    if final:
        in_specs += [pl.BlockSpec((1, D), lambda i, f: (0, 0)), rows.mod_spec(D, 2), rows.mod_spec(D, 2)]
        args += [final_mod[0].reshape(1, D), final_mod[1], final_mod[2]]
    y, cst = pl.pallas_call(
        functools.partial(_ffn_kernel, seq=rows.seq, tiles_per_seq=rows.tiles_per_seq,
                          has_prev=has_prev, final=final),
        grid=(M // tm, nf),
        in_specs=in_specs,
        out_specs=[pl.BlockSpec((tm, D), lambda i, f: (i, 0)), cst_spec],
        out_shape=[jax.ShapeDtypeStruct((M, D), F32), cst_shape],
        scratch_shapes=[pltpu.VMEM((tm, D), BF16), pltpu.VMEM((tm, D), F32),
                        pltpu.VMEM((tm + SUBLANES, tf), F32), pltpu.VMEM((nf, SUBLANES, tf), F32)],
        compiler_params=_cparams("arbitrary", "arbitrary"),
        name="conv_ffn",
    )(*args)
    if not has_prev:
        cst = cst[rows.tiles_per_seq - 1::rows.tiles_per_seq]
    return y, cst


def _ret_core_kernel(*refs, c_real, has_state):
    it = iter(refs)
    q_ref, k_ref, v_ref, g_ref, cos_ref, sin_ref = (next(it) for _ in range(6))
    if has_state:
        s0_ref = next(it)
    o_ref, s_ref = next(it), next(it)
    cp = q_ref.shape[0]

    @pl.when(pl.program_id(1) == 0)
    def _():
        if has_state:
            s_ref[...] = s0_ref[...]
        else:
            s_ref[...] = jnp.zeros_like(s_ref)

    cos, sin = cos_ref[...], sin_ref[...]
    ri = lax.broadcasted_iota(jnp.int32, (cp, cp), 0)
    ci = lax.broadcasted_iota(jnp.int32, (cp, cp), 1)
    causal = ri >= ci
    diff = jnp.where(causal, ri - ci, 0).astype(F32)
    row = lax.broadcasted_iota(jnp.int32, (cp, 1), 0).astype(F32)
    half = RET_DK // 2

    def rope(ref, h):
        x1 = ref[:, h * RET_DK:h * RET_DK + half]
        x2 = ref[:, h * RET_DK + half:(h + 1) * RET_DK]
        return jnp.concatenate([x1 * cos - x2 * sin, x1 * sin + x2 * cos], axis=-1)

    for h in range(RET_HEADS):
        lg = math.log(1.0 - 2.0 ** (-5.0 - h))
        decay = jnp.where(causal, jnp.exp(diff * lg), 0.0)
        q_decay = jnp.exp((row + 1.0) * lg)
        k_decay = jnp.exp((c_real - 1.0 - row) * lg)
        chunk_decay = math.exp(c_real * lg)
        q = rope(q_ref, h)
        k = rope(k_ref, h) * (RET_DK ** -0.5)
        v = v_ref[:, h * RET_DV:(h + 1) * RET_DV]
        s = s_ref[h]
        intra = _dot_nt(q, k) * decay
        o = _dot(intra, v) + _dot(q, s) * q_decay
        s_ref[h] = s * chunk_decay + _dot_tn(k * k_decay, v)
        o = o * lax.rsqrt(jnp.mean(o * o, axis=-1, keepdims=True) + EPS)
        o_ref[:, h * RET_DV:(h + 1) * RET_DV] = o * _silu(g_ref[:, h * RET_DV:(h + 1) * RET_DV])


def _ret_core(proj3, cos, sin, s0, c_real, cp):
    B, Tp, _ = proj3.shape
    qk = RET_HEADS * RET_DK
    vd = RET_HEADS * RET_DV
    has_state = s0 is not None
    in_specs = [pl.BlockSpec((None, cp, qk), lambda b, n: (b, n, 0)),
                pl.BlockSpec((None, cp, qk), lambda b, n: (b, n, 1)),
                pl.BlockSpec((None, cp, vd), lambda b, n: (b, n, 1)),
                pl.BlockSpec((None, cp, vd), lambda b, n: (b, n, 2)),
                pl.BlockSpec((cp, RET_DK // 2), lambda b, n: (n, 0)),
                pl.BlockSpec((cp, RET_DK // 2), lambda b, n: (n, 0))]
    args = [proj3, proj3, proj3, proj3, cos, sin]
    s_spec = pl.BlockSpec((None, RET_HEADS, RET_DK, RET_DV), lambda b, n: (b, 0, 0, 0))
    if has_state:
        in_specs.append(s_spec)
        args.append(s0)
    return pl.pallas_call(
        functools.partial(_ret_core_kernel, c_real=c_real, has_state=has_state),
        grid=(B, Tp // cp),
        in_specs=in_specs,
        out_specs=[pl.BlockSpec((None, cp, vd), lambda b, n: (b, n, 0)), s_spec],
        out_shape=[jax.ShapeDtypeStruct((B, Tp, vd), F32),
                   jax.ShapeDtypeStruct((B, RET_HEADS, RET_DK, RET_DV), F32)],
        compiler_params=_cparams("parallel", "arbitrary"),
        name="ret_core",
    )(*args)


def _block_inverse_many(ls, ri, ci, bs):
    eye = (ri == ci).astype(F32)
    base = (ri // INV_BASE) == (ci // INV_BASE)
    pws = [-jnp.where(base, l, 0.0) for l in ls]
    ps = [eye + m for m in pws]
    span = 2
    while span < INV_BASE:
        pws = [_dot(pw, pw) for pw in pws]
        ps = [p + _dot(p, pw) for p, pw in zip(ps, pws)]
        span *= 2
    size = INV_BASE
    while size < bs:
        off = ((ri // (2 * size)) == (ci // (2 * size))) & ((ri // size) != (ci // size))
        xs = [_dot(jnp.where(off, l, 0.0), p) for l, p in zip(ls, ps)]
        ps = [p - _dot(p, x) for p, x in zip(ps, xs)]
        size *= 2
    return ps


def _gdn_core_kernel(*refs, c_real, has_state):
    it = iter(refs)
    mx_ref, z_ref, gt_ref, wc_ref, al_ref, dt_ref, nw_ref = (next(it) for _ in range(7))
    if has_state:
        cs_ref, s0_ref = next(it), next(it)
    o_ref, s_ref = next(it), next(it)
    xp_ref, qkv_ref = next(it), next(it)
    bb, cp, cdim = mx_ref.shape
    bs = cp
    hg = GROUP_ROWS // bs
    ng = GDN_V_HEADS // hg
    rep = GDN_V_HEADS // GDN_K_HEADS
    key = GDN_K_HEADS * GDN_D
    n_items = bb * ng
    assert n_items == SUBLANES

    @pl.when(pl.program_id(1) == 0)
    def _():
        if has_state:
            s_ref[...] = s0_ref[...]
            xp_ref[:, 0:SUBLANES, :] = cs_ref[...]
        else:
            s_ref[...] = jnp.zeros_like(s_ref)
            xp_ref[:, 0:SUBLANES, :] = jnp.zeros((bb, SUBLANES, cdim), F32)

    base = SUBLANES - (GDN_CONV_W - 1)
    wc = wc_ref[...]
    for i in range(bb):
        xp_ref[i, SUBLANES:, :] = mx_ref[i]
        for c in range(cdim // LANES):
            sl = slice(c * LANES, (c + 1) * LANES)
            acc = wc[0:1, sl] * xp_ref[i, base:base + cp, sl]
            for t in range(1, GDN_CONV_W):
                acc = acc + wc[t:t + 1, sl] * xp_ref[i, base + t:base + t + cp, sl]
            acc = _silu(acc)
            if c < 2 * GDN_K_HEADS:
                acc = acc * lax.rsqrt(jnp.sum(acc * acc, axis=-1, keepdims=True) + EPS)
                if c < GDN_K_HEADS:
                    acc = acc * (GDN_D ** -0.5)
            qkv_ref[i, :, sl] = acc
        xp_ref[i, 0:SUBLANES, :] = xp_ref[i, cp:cp + SUBLANES, :]

    ri = lax.broadcasted_iota(jnp.int32, (GROUP_ROWS, GROUP_ROWS), 0)
    ci = lax.broadcasted_iota(jnp.int32, (GROUP_ROWS, GROUP_ROWS), 1)
    same = (ri // bs) == (ci // bs)
    incl = same & (ri >= ci)
    strict = same & (ri > ci)
    eye = (ri == ci).astype(F32)

    live = (lax.broadcasted_iota(jnp.int32, (1, GROUP_ROWS), 1) % bs) < c_real
    beta_rows = jnp.where(live, jax.nn.sigmoid(gt_ref[0]), 0.0)
    g_rows = jnp.where(live, -jnp.exp(al_ref[...]) * _softplus(gt_ref[1] + dt_ref[...]), 0.0)
    gsum_rows = jnp.dot(g_rows, (same & (ri <= ci)).astype(F32), precision=_HIGHEST,
                        preferred_element_type=F32)
    cols = jnp.concatenate([beta_rows, gsum_rows,
                            jnp.zeros((GROUP_ROWS - 2 * SUBLANES, GROUP_ROWS), F32)], axis=0).T

    items = [(i, j) for i in range(bb) for j in range(ng)]

    def stack(i, j, ref, col_of_head):
        return jnp.concatenate(
            [ref[i, :, col_of_head(j * hg + hh):col_of_head(j * hg + hh) + GDN_D] for hh in range(hg)],
            axis=0)

    qxs = [stack(i, j, qkv_ref, lambda h: (h // rep) * GDN_D) for i, j in items]
    kxs = [stack(i, j, qkv_ref, lambda h: key + (h // rep) * GDN_D) for i, j in items]
    vxs = [stack(i, j, qkv_ref, lambda h: 2 * key + h * GDN_D) for i, j in items]
    bcs = [jnp.broadcast_to(cols[:, m:m + 1], (GROUP_ROWS, GROUP_ROWS)) for m in range(n_items)]
    gcs = [jnp.broadcast_to(cols[:, SUBLANES + m:SUBLANES + m + 1], (GROUP_ROWS, GROUP_ROWS))
           for m in range(n_items)]
    kbs = [kx * bc for kx, bc in zip(kxs, bcs)]
    kks = [_dot_nt(kb, kx) for kb, kx in zip(kbs, kxs)]
    qks = [_dot_nt(qx, kx) for qx, kx in zip(qxs, kxs)]
    decays = [jnp.exp(jnp.where(incl, gc - gsum_rows[m:m + 1, :], -jnp.inf))
              for m, gc in enumerate(gcs)]
    ls = [jnp.where(strict, kk * d, 0.0) for kk, d in zip(kks, decays)]
    attns = [qk * d for qk, d in zip(qks, decays)]
    ps = _block_inverse_many(ls, ri, ci, bs)
    egs = [jnp.exp(gc) for gc in gcs]
    rhss = [jnp.concatenate([vx * bc, kb * eg], axis=-1) for vx, bc, kb, eg in zip(vxs, bcs, kbs, egs)]
    sols = [rhs + _dot(p - eye, rhs) for rhs, p in zip(rhss, ps)]
    qes = [qx * eg for qx, eg in zip(qxs, egs)]

    heads = [(m, i, j * hg + hh, slice(hh * bs, (hh + 1) * bs))
             for m, (i, j) in enumerate(items) for hh in range(hg)]
    wqs = [_dot(jnp.concatenate([sols[m][rs, GDN_D:], qes[m][rs, :]], axis=0), s_ref[i, h])
           for m, i, h, rs in heads]
    v_news, qss = [], []
    for m in range(n_items):
        part = wqs[m * hg:(m + 1) * hg]
        v_news.append(sols[m][:, :GDN_D] - jnp.concatenate([r[:bs] for r in part], axis=0))
        qss.append(jnp.concatenate([r[bs:] for r in part], axis=0))
    outs = [qs + _dot(attn, vn) for qs, attn, vn in zip(qss, attns, v_news)]

    def last_rows(gc):
        return [gc[hh * bs + c_real - 1:hh * bs + c_real, :] for hh in range(hg)]
    kds = [kx * jnp.exp(jnp.concatenate([jnp.broadcast_to(r, (bs, GROUP_ROWS)) for r in last_rows(gc)],
                                        axis=0) - gc)
           for kx, gc in zip(kxs, gcs)]
    upds = [_dot_tn(kds[m][rs, :], v_news[m][rs, :]) for m, i, h, rs in heads]
    for (m, i, h, rs), upd in zip(heads, upds):
        g_last = gcs[m][rs.start + c_real - 1:rs.start + c_real, :]
        s_ref[i, h] = s_ref[i, h] * jnp.exp(g_last) + upd

    nw = nw_ref[...]
    for m, i, h, rs in heads:
        o = outs[m][rs, :]
        o = o * lax.rsqrt(jnp.mean(o * o, axis=-1, keepdims=True) + EPS) * nw
        o_ref[i, :, h * GDN_D:(h + 1) * GDN_D] = o * _silu(z_ref[i, :, h * GDN_D:(h + 1) * GDN_D])


def _gdn_core(proj3, gates, w_conv, a_log, dt_bias, norm_w, cs8, s0, c_real, cp, bb):
    B, Tp, _ = proj3.shape
    key = GDN_K_HEADS * GDN_D
    vd = GDN_V_HEADS * GDN_D
    cdim = 2 * key + vd
    nc = Tp // cp
    hg = GROUP_ROWS // cp
    ng = GDN_V_HEADS // hg
    has_state = s0 is not None
    lanes = lambda a: jnp.tile(jnp.repeat(a.reshape(ng, hg), cp, axis=1), (bb, 1))
    full = lambda shape: pl.BlockSpec(shape, lambda b, n: (0,) * len(shape))
    in_specs = [pl.BlockSpec((bb, cp, cdim), lambda b, n: (b, n, 0)),
                pl.BlockSpec((bb, cp, vd), lambda b, n: (b, n, cdim // vd)),
                pl.BlockSpec((None, 2, bb * ng, GROUP_ROWS), lambda b, n: (b * nc + n, 0, 0, 0)),
                full((GDN_CONV_W, cdim)), full((bb * ng, GROUP_ROWS)), full((bb * ng, GROUP_ROWS)),
                full((1, GDN_D))]
    args = [proj3, proj3, gates, w_conv, lanes(a_log), lanes(dt_bias), norm_w.reshape(1, GDN_D)]
    s_spec = pl.BlockSpec((bb, GDN_V_HEADS, GDN_D, GDN_D), lambda b, n: (b, 0, 0, 0))
    if has_state:
        in_specs += [pl.BlockSpec((bb, SUBLANES, cdim), lambda b, n: (b, 0, 0)), s_spec]
        args += [cs8, s0]
    return pl.pallas_call(
        functools.partial(_gdn_core_kernel, c_real=c_real, has_state=has_state),
        grid=(B // bb, nc),
        in_specs=in_specs,
        out_specs=[pl.BlockSpec((bb, cp, vd), lambda b, n: (b, n, 0)), s_spec],
        out_shape=[jax.ShapeDtypeStruct((B, Tp, vd), F32),
                   jax.ShapeDtypeStruct((B, GDN_V_HEADS, GDN_D, GDN_D), F32)],
        scratch_shapes=[pltpu.VMEM((bb, cp + SUBLANES, cdim), F32), pltpu.VMEM((bb, cp, cdim), F32)],
        compiler_params=_cparams("parallel", "arbitrary"),
        name="gdn_core",
    )(*args)


def _rope_tables(pos0, t_real, t_pad):
    half = RET_DK // 2
    inv_freq = ROPE_BASE ** (-np.arange(half, dtype=np.float64) / half)
    pos = pos0 + np.arange(t_pad, dtype=np.float64)
    ang = pos[:, None] * inv_freq[None, :]
    live = (np.arange(t_pad) < t_real)[:, None]
    return (jnp.asarray(np.where(live, np.cos(ang), 0.0), F32),
            jnp.asarray(np.where(live, np.sin(ang), 0.0), F32))


def _pad_seq(a, t_pad):
    t = a.shape[1]
    if t == t_pad:
        return a
    return jnp.pad(a, ((0, 0), (0, t_pad - t), (0, 0)))


def _trunk(x, mods, mod_final, pos0, s_ret, s_gdn, s_gconv, s_fconv, p, wb, tm):
    B, T, D = x.shape
    rows = _Rows(B, T, tm)
    M = B * T
    xf = x.reshape(M, D)
    t_pad = max(T, SUBLANES)
    new_ret = new_gdn = new_gconv = None
    new_fconv = []
    for l in range(2):
        sh1, sc1, g1, sh2, sc2, g2 = (rows.mod(m) for m in jnp.split(mods[l], 6, axis=-1))
        if l == 0:
            proj = _norm_proj(rows, xf, p['norm_mix'][l], sh1, sc1, wb['ret_in'])
            cp = min(RET_CHUNK, t_pad)
            cos, sin = _rope_tables(pos0, T, t_pad)
            o3, new_ret = _ret_core(_pad_seq(proj.reshape(B, T, -1), t_pad), cos, sin,
                                    None if s_ret is None else s_ret[0], min(T, cp), cp)
            w_out = wb['ret_out']
        else:
            proj, bat = _norm_proj(rows, xf, p['norm_mix'][l], sh1, sc1, wb['gdn_in'],
                                   w_tail=wb['gdn_tail'])
            cp = min(GDN_CHUNK, t_pad)
            nc = t_pad // cp
            ng = GDN_V_HEADS * cp // GROUP_ROWS
            bb = SUBLANES // ng
            proj3 = proj.reshape(B, T, -1)
            cdim = (2 * GDN_K_HEADS + GDN_V_HEADS) * GDN_D
            new_gconv = proj3[:, T - (GDN_CONV_W - 1):, :cdim]
            gates = jnp.pad(bat.reshape(2, GDN_V_HEADS, B, T), ((0, 0), (0, 0), (0, 0), (0, t_pad - T)))
            gates = gates.reshape(2, ng, GROUP_ROWS // cp, B // bb, bb, nc, cp)
            gates = gates.transpose(3, 5, 0, 4, 1, 2, 6).reshape(B // bb * nc, 2, bb * ng, GROUP_ROWS)
            cs8 = None
            if s_gconv is not None:
                cs8 = jnp.pad(s_gconv[0], ((0, 0), (SUBLANES - (GDN_CONV_W - 1), 0), (0, 0)))
            o3, new_gdn = _gdn_core(_pad_seq(proj3, t_pad), gates, p['w_gdn_conv'][0],
                                    p['gdn_a_log'][0], p['gdn_dt_bias'][0], p['gdn_norm'][0],
                                    cs8, None if s_gdn is None else s_gdn[0], min(T, cp), cp, bb)
            w_out = wb['gdn_out']
        o = o3[:, :T].reshape(M, -1)
        xf = _out_proj(rows, o, w_out, xf, g1)
        prev = None
        if s_fconv is not None:
            buf = s_fconv[l]
            zero = jnp.zeros_like(buf[:, :1])
            fill = jnp.zeros((B, T - 2, buf.shape[-1]), F32)
            p1 = jnp.concatenate([buf[:, 1:2], zero, fill], axis=1).reshape(M, -1)
            p2 = jnp.concatenate([buf[:, 0:1], buf[:, 1:2], fill], axis=1).reshape(M, -1)
            prev = (p1, p2)
        final_mod = None
        if l == 1:
            shf, scf = (rows.mod(m) for m in jnp.split(mod_final, 2, axis=-1))
            final_mod = (p['norm_final'], shf, scf)
        xf, cst = _ffn(rows, xf, p['norm_ffn'][l], sh2, sc2, g2, wb['ffn_up'][l], p['w_ffn_dw'][l],
                       p['b_ffn_dw'][l], wb['ffn_down'][l], prev=prev, final_mod=final_mod)
        if s_fconv is not None:
            cst = cst.reshape(B, T, -1)[:, T - (FFN_CONV_W - 1):]
        new_fconv.append(cst)
    return (xf.reshape(B, T, D), new_ret[None], new_gdn[None], new_gconv[None], jnp.stack(new_fconv))


def kernel(x_prompt, x_sample, state_ret, state_gdn, state_gdn_conv, state_ffn_conv, c_prompt, c_sample, w_ada, b_ada, w_ada_final, b_ada_final, norm_mix, norm_ffn, norm_final, w_ret_in, w_ret_out, w_gdn_in, w_gdn_conv, gdn_a_log, gdn_dt_bias, gdn_norm, w_gdn_out, w_ffn_up, w_ffn_dw, b_ffn_dw, w_ffn_down):
    p = {'norm_mix': norm_mix, 'norm_ffn': norm_ffn, 'norm_final': norm_final,
         'w_gdn_conv': w_gdn_conv, 'gdn_a_log': gdn_a_log, 'gdn_dt_bias': gdn_dt_bias,
         'gdn_norm': gdn_norm, 'w_ffn_dw': w_ffn_dw, 'b_ffn_dw': b_ffn_dw}
    cdim = (2 * GDN_K_HEADS + GDN_V_HEADS) * GDN_D
    vd = GDN_V_HEADS * GDN_D
    tail = w_gdn_in[0][:, cdim + vd:]
    wb = {'ret_in': w_ret_in[0].astype(BF16), 'ret_out': w_ret_out[0].astype(BF16),
          'gdn_in': w_gdn_in[0][:, :cdim + vd].astype(BF16), 'gdn_out': w_gdn_out[0].astype(BF16),
          'gdn_tail': tail,
          'ffn_up': w_ffn_up.astype(BF16), 'ffn_down': w_ffn_down.astype(BF16)}

    bp = c_prompt.shape[0]
    c_all = jnp.concatenate([c_prompt, c_sample], axis=0)
    mods = _ada_mod(c_all, w_ada, b_ada)
    mod_final = _ada_mod(c_all, w_ada_final[None], b_ada_final[None])[0]

    out_p = _trunk(x_prompt, mods[:, :bp], mod_final[:bp], 0, None, None, None, None, p, wb, tm=512)
    out_s = _trunk(x_sample, mods[:, bp:], mod_final[bp:], PAST_LEN, state_ret, state_gdn,
                   state_gdn_conv, state_ffn_conv, p, wb,
                   tm=x_sample.shape[0] * x_sample.shape[1])
    y_p, ret_p, gdn_p, gconv_p, fconv_p = out_p
    y_s, ret_s, gdn_s, gconv_s, fconv_s = out_s
    return (y_p, y_s, ret_p, gdn_p, gconv_p, fconv_p, ret_s, gdn_s, gconv_s, fconv_s)
```

```python
import functools
import math

import numpy as np
import jax
import jax.numpy as jnp
from jax import lax
from jax.experimental import pallas as pl
from jax.experimental.pallas import tpu as pltpu

F32 = jnp.float32
BF16 = jnp.bfloat16
EPS = 1e-6
ROPE_BASE = 10000.0
PAST_LEN = 16384

RET_HEADS = 4
RET_DK = 256
RET_DV = 512
RET_CHUNK = 128
GDN_K_HEADS = 8
GDN_V_HEADS = 16
GDN_D = 128
GDN_CHUNK = 64
GDN_CONV_W = 4
FFN_CONV_W = 3
SUBLANES = 8
LANES = 128
INV_BASE = 8
GROUP_ROWS = 128
VMEM_LIMIT = 48 * 1024 * 1024

_HIGHEST = lax.Precision.HIGHEST


def _cparams(*sem):
    return pltpu.CompilerParams(dimension_semantics=sem, vmem_limit_bytes=VMEM_LIMIT)


def _silu(x):
    return x * jax.nn.sigmoid(x)


def _softplus(x):
    return jnp.maximum(x, 0.0) + jnp.log1p(jnp.exp(-jnp.abs(x)))


def _ada_norm(x, gamma, shift, scale):
    ms = jnp.mean(x * x, axis=-1, keepdims=True)
    xn = x * lax.rsqrt(ms + EPS) * gamma
    return xn * (1.0 + scale) + shift


def _dot(a, b):
    return jnp.dot(a.astype(BF16), b.astype(BF16), preferred_element_type=F32)


def _dot_nt(a, b):
    return lax.dot_general(a.astype(BF16), b.astype(BF16), (((1,), (1,)), ((), ())),
                           preferred_element_type=F32)


def _dot_tn(a, b):
    return lax.dot_general(a.astype(BF16), b.astype(BF16), (((0,), (0,)), ((), ())),
                           preferred_element_type=F32)


def _ada_kernel(c_ref, w_ref, b_ref, o_ref):
    cs = _silu(c_ref[...])
    o_ref[...] = _dot(cs, w_ref[...]) + b_ref[...]


def _ada_mod(c, w, b, tn=1024):
    L, D, N = w.shape
    Mc = c.shape[0]
    return pl.pallas_call(
        _ada_kernel,
        grid=(L, N // tn),
        in_specs=[pl.BlockSpec((Mc, D), lambda l, j: (0, 0)),
                  pl.BlockSpec((None, D, tn), lambda l, j: (l, 0, j)),
                  pl.BlockSpec((None, 1, tn), lambda l, j: (l, 0, j))],
        out_specs=pl.BlockSpec((None, Mc, tn), lambda l, j: (l, 0, j)),
        out_shape=jax.ShapeDtypeStruct((L, Mc, N), F32),
        compiler_params=_cparams("parallel", "parallel"),
        name="ada_mod",
    )(c, w, b.reshape(L, 1, N))


class _Rows:
    def __init__(self, batch, seq, tm):
        self.batch, self.seq, self.tm = batch, seq, tm
        self.M = batch * seq
        self.n_tiles = self.M // tm
        self.per_token_mod = seq < tm
        assert (tm % seq == 0) if self.per_token_mod else (seq % tm == 0)
        self.tiles_per_seq = max(seq // tm, 1)

    def mod(self, m):
        if self.per_token_mod:
            return jnp.repeat(m, self.seq, axis=0).reshape(self.n_tiles, self.tm, -1)
        return m[:, None, :]

    def mod_spec(self, D):
        tps = self.tiles_per_seq
        if self.per_token_mod:
            return pl.BlockSpec((None, self.tm, D), lambda i: (i, 0, 0))
        return pl.BlockSpec((None, 1, D), lambda i: (i // tps, 0, 0))


def _resident(shape):
    return pl.BlockSpec(shape, lambda i: (0,) * len(shape), pipeline_mode=pl.Buffered(1))


def _norm_proj_kernel(x_ref, gam_ref, sh_ref, sc_ref, w_ref, *rest, has_tail, tn):
    if has_tail:
        wtt_ref, o_ref, ott_ref, h_ref = rest
    else:
        o_ref, h_ref = rest
    h = _ada_norm(x_ref[...], gam_ref[...], sh_ref[...], sc_ref[...])
    h_ref[...] = h.astype(BF16)
    if has_tail:
        ott_ref[...] = lax.dot_general(wtt_ref[...], h, (((1,), (1,)), ((), ())),
                                       precision=_HIGHEST, preferred_element_type=F32)
    for j in range(w_ref.shape[1] // tn):
        sl = slice(j * tn, (j + 1) * tn)
        o_ref[:, sl] = jnp.dot(h_ref[...], w_ref[:, sl], preferred_element_type=F32).astype(o_ref.dtype)


def _norm_proj(rows, x, gamma, shift, scale, w, out_dtype, w_tail=None, tn=512):
    M, D = x.shape
    N = w.shape[1]
    tm = rows.tm
    has_tail = w_tail is not None
    in_specs = [pl.BlockSpec((tm, D), lambda i: (i, 0)), _resident((1, D)),
                rows.mod_spec(D), rows.mod_spec(D), _resident((D, N))]
    args = [x, gamma.reshape(1, D), shift, scale, w]
    out_specs = [pl.BlockSpec((tm, N), lambda i: (i, 0))]
    out_shape = [jax.ShapeDtypeStruct((M, N), out_dtype)]
    if has_tail:
        nt = w_tail.shape[1]
        in_specs += [_resident((nt, D))]
        args += [w_tail.T]
        out_specs += [pl.BlockSpec((nt, tm), lambda i: (0, i))]
        out_shape += [jax.ShapeDtypeStruct((nt, M), F32)]
    res = pl.pallas_call(
        functools.partial(_norm_proj_kernel, has_tail=has_tail, tn=tn),
        grid=(M // tm,),
        in_specs=in_specs,
        out_specs=out_specs,
        out_shape=out_shape,
        scratch_shapes=[pltpu.VMEM((tm, D), BF16)],
        compiler_params=_cparams("parallel"),
        name="norm_proj_tail" if has_tail else "norm_proj",
    )(*args)
    return res if has_tail else res[0]


def _out_proj_kernel(o_ref, w_ref, x_ref, g_ref, y_ref):
    y_ref[...] = x_ref[...] + g_ref[...] * _dot(o_ref[...], w_ref[...])


def _out_proj(rows, o, w, x, gate):
    M, K = o.shape
    D = w.shape[1]
    tm = rows.tm
    return pl.pallas_call(
        _out_proj_kernel,
        grid=(M // tm,),
        in_specs=[pl.BlockSpec((tm, K), lambda i: (i, 0)),
                  _resident((K, D)),
                  pl.BlockSpec((tm, D), lambda i: (i, 0)),
                  rows.mod_spec(D)],
        out_specs=pl.BlockSpec((tm, D), lambda i: (i, 0)),
        out_shape=jax.ShapeDtypeStruct((M, D), F32),
        compiler_params=_cparams("parallel"),
        name="out_proj",
    )(o, w, x, gate)


def _ffn_kernel(*refs, seq, tiles_per_seq, has_prev, final, tf):
    it = iter(refs)
    x_ref, gam_ref, sh_ref, sc_ref, gate_ref = (next(it) for _ in range(5))
    wup_ref, wdw_ref, bdw_ref, wd_ref = (next(it) for _ in range(4))
    if has_prev:
        p1_ref, p2_ref = next(it), next(it)
    if final:
        gamf_ref, shf_ref, scf_ref = next(it), next(it), next(it)
    y_ref, cst_ref = next(it), next(it)
    h_ref, act_ref, cbuf_ref = (next(it) for _ in range(3))
    tm = x_ref.shape[0]
    fd = wd_ref.shape[0]

    h_ref[...] = _ada_norm(x_ref[...], gam_ref[...], sh_ref[...], sc_ref[...]).astype(BF16)
    if not has_prev:
        @pl.when((pl.program_id(0) % tiles_per_seq) == 0)
        def _():
            cbuf_ref[0:SUBLANES, :] = jnp.zeros((SUBLANES, fd), F32)

    for f in range(fd // tf):
        sl = slice(f * tf, (f + 1) * tf)
        h = h_ref[...]
        gbr = jnp.dot(h, wup_ref[:, sl], preferred_element_type=F32)
        val = jnp.dot(h, wup_ref[:, fd + f * tf:fd + (f + 1) * tf], preferred_element_type=F32)
        cbuf_ref[SUBLANES:, sl] = gbr
        s1 = cbuf_ref[SUBLANES - 1:SUBLANES - 1 + tm, sl]
        s2 = cbuf_ref[SUBLANES - 2:SUBLANES - 2 + tm, sl]
        if has_prev:
            t = lax.broadcasted_iota(jnp.int32, (tm, tf), 0) % seq
            s1 = jnp.where(t == 0, p1_ref[:, sl], s1)
            s2 = jnp.where(t <= 1, p2_ref[:, sl], s2)
            cst_ref[:, sl] = gbr
        conv = (wdw_ref[0:1, sl] * s2 + wdw_ref[1:2, sl] * s1 + wdw_ref[2:3, sl] * gbr
                + bdw_ref[:, sl])
        act_ref[:, sl] = (_silu(conv) * val).astype(BF16)

    if not has_prev:
        cst_ref[...] = cbuf_ref[tm + SUBLANES - (FFN_CONV_W - 1):tm + SUBLANES, :]
        cbuf_ref[0:SUBLANES, :] = cbuf_ref[tm:tm + SUBLANES, :]
    xn = x_ref[...] + gate_ref[...] * jnp.dot(act_ref[...], wd_ref[...], preferred_element_type=F32)
    if final:
        xn = _ada_norm(xn, gamf_ref[...], shf_ref[...], scf_ref[...])
    y_ref[...] = xn


def _ffn(rows, x, gamma, shift, scale, gate, w_up, w_dw, b_dw, w_down, prev=None, final_mod=None,
         tf=256):
    M, D = x.shape
    Fd = w_down.shape[0]
    tm = rows.tm
    has_prev = prev is not None
    final = final_mod is not None
    in_specs = [pl.BlockSpec((tm, D), lambda i: (i, 0)), _resident((1, D)),
                rows.mod_spec(D), rows.mod_spec(D), rows.mod_spec(D),
                _resident((D, 2 * Fd)), _resident((FFN_CONV_W, Fd)), _resident((1, Fd)),
                _resident((Fd, D))]
    args = [x, gamma.reshape(1, D), shift, scale, gate, w_up, w_dw, b_dw.reshape(1, Fd), w_down]
    if has_prev:
        in_specs += [pl.BlockSpec((tm, Fd), lambda i: (i, 0))] * 2
        args += list(prev)
        cst_spec = pl.BlockSpec((tm, Fd), lambda i: (i, 0))
        cst_shape = jax.ShapeDtypeStruct((M, Fd), F32)
    else:
        cst_spec = pl.BlockSpec((None, FFN_CONV_W - 1, Fd), lambda i: (i, 0, 0))
        cst_shape = jax.ShapeDtypeStruct((rows.n_tiles, FFN_CONV_W - 1, Fd), F32)
    if final:
        in_specs += [_resident((1, D)), rows.mod_spec(D), rows.mod_spec(D)]
        args += [final_mod[0].reshape(1, D), final_mod[1], final_mod[2]]
    y, cst = pl.pallas_call(
        functools.partial(_ffn_kernel, seq=rows.seq, tiles_per_seq=rows.tiles_per_seq,
                          has_prev=has_prev, final=final, tf=tf),
        grid=(M // tm,),
        in_specs=in_specs,
        out_specs=[pl.BlockSpec((tm, D), lambda i: (i, 0)), cst_spec],
        out_shape=[jax.ShapeDtypeStruct((M, D), F32), cst_shape],
        scratch_shapes=[pltpu.VMEM((tm, D), BF16), pltpu.VMEM((tm, Fd), BF16),
                        pltpu.VMEM((tm + SUBLANES, Fd), F32)],
        compiler_params=_cparams("arbitrary"),
        name="conv_ffn",
    )(*args)
    if not has_prev:
        cst = cst[rows.tiles_per_seq - 1::rows.tiles_per_seq]
    return y, cst


def _ret_core_kernel(*refs, c_real, has_state):
    it = iter(refs)
    q_ref, k_ref, v_ref, g_ref, cos_ref, sin_ref = (next(it) for _ in range(6))
    if has_state:
        s0_ref = next(it)
    o_ref, s_ref = next(it), next(it)
    cp = q_ref.shape[0]

    @pl.when(pl.program_id(1) == 0)
    def _():
        if has_state:
            s_ref[...] = s0_ref[...]
        else:
            s_ref[...] = jnp.zeros_like(s_ref)

    cos, sin = cos_ref[...], sin_ref[...]
    ri = lax.broadcasted_iota(jnp.int32, (cp, cp), 0)
    ci = lax.broadcasted_iota(jnp.int32, (cp, cp), 1)
    causal = ri >= ci
    diff = jnp.where(causal, ri - ci, 0).astype(F32)
    row = lax.broadcasted_iota(jnp.int32, (cp, 1), 0).astype(F32)
    half = RET_DK // 2

    def rope(ref, h):
        x1 = ref[:, h * RET_DK:h * RET_DK + half].astype(F32)
        x2 = ref[:, h * RET_DK + half:(h + 1) * RET_DK].astype(F32)
        return jnp.concatenate([x1 * cos - x2 * sin, x1 * sin + x2 * cos], axis=-1)

    for h in range(RET_HEADS):
        lg = math.log(1.0 - 2.0 ** (-5.0 - h))
        decay = jnp.where(causal, jnp.exp(diff * lg), 0.0)
        q_decay = jnp.exp((row + 1.0) * lg)
        k_decay = jnp.exp((c_real - 1.0 - row) * lg)
        chunk_decay = math.exp(c_real * lg)
        q = rope(q_ref, h)
        k = rope(k_ref, h) * (RET_DK ** -0.5)
        v = v_ref[:, h * RET_DV:(h + 1) * RET_DV]
        s = s_ref[h]
        intra = _dot_nt(q, k) * decay
        o = _dot(intra, v) + _dot(q, s) * q_decay
        s_ref[h] = s * chunk_decay + _dot_tn(k * k_decay, v)
        o = o * lax.rsqrt(jnp.mean(o * o, axis=-1, keepdims=True) + EPS)
        gate = _silu(g_ref[:, h * RET_DV:(h + 1) * RET_DV].astype(F32))
        o_ref[:, h * RET_DV:(h + 1) * RET_DV] = (o * gate).astype(o_ref.dtype)


def _ret_core(proj3, cos, sin, s0, c_real, cp):
    B, Tp, _ = proj3.shape
    qk = RET_HEADS * RET_DK
    vd = RET_HEADS * RET_DV
    has_state = s0 is not None
    in_specs = [pl.BlockSpec((None, cp, qk), lambda b, n: (b, n, 0)),
                pl.BlockSpec((None, cp, qk), lambda b, n: (b, n, 1)),
                pl.BlockSpec((None, cp, vd), lambda b, n: (b, n, 1)),
                pl.BlockSpec((None, cp, vd), lambda b, n: (b, n, 2)),
                pl.BlockSpec((cp, RET_DK // 2), lambda b, n: (n, 0)),
                pl.BlockSpec((cp, RET_DK // 2), lambda b, n: (n, 0))]
    args = [proj3, proj3, proj3, proj3, cos, sin]
    s_spec = pl.BlockSpec((None, RET_HEADS, RET_DK, RET_DV), lambda b, n: (b, 0, 0, 0))
    if has_state:
        in_specs.append(s_spec)
        args.append(s0)
    return pl.pallas_call(
        functools.partial(_ret_core_kernel, c_real=c_real, has_state=has_state),
        grid=(B, Tp // cp),
        in_specs=in_specs,
        out_specs=[pl.BlockSpec((None, cp, vd), lambda b, n: (b, n, 0)), s_spec],
        out_shape=[jax.ShapeDtypeStruct((B, Tp, vd), proj3.dtype),
                   jax.ShapeDtypeStruct((B, RET_HEADS, RET_DK, RET_DV), F32)],
        compiler_params=_cparams("parallel", "arbitrary"),
        name="ret_core",
    )(*args)


def _block_inverse_many(ls, ri, ci, bs):
    eye = (ri == ci).astype(F32)
    base = (ri // INV_BASE) == (ci // INV_BASE)
    pws = [-jnp.where(base, l, 0.0) for l in ls]
    ps = [eye + m for m in pws]
    span = 2
    while span < INV_BASE:
        pws = [_dot(pw, pw) for pw in pws]
        ps = [p + _dot(p, pw) for p, pw in zip(ps, pws)]
        span *= 2
    size = INV_BASE
    while size < bs:
        off = ((ri // (2 * size)) == (ci // (2 * size))) & ((ri // size) != (ci // size))
        xs = [_dot(jnp.where(off, l, 0.0), p) for l, p in zip(ls, ps)]
        ps = [p - _dot(p, x) for p, x in zip(ps, xs)]
        size *= 2
    return ps


def _gdn_core_kernel(*refs, c_real, has_state):
    it = iter(refs)
    mx_ref, z_ref, gt_ref, wc_ref, al_ref, dt_ref, nw_ref = (next(it) for _ in range(7))
    if has_state:
        cs_ref, s0_ref = next(it), next(it)
    o_ref, s_ref = next(it), next(it)
    xp_ref, qkv_ref = next(it), next(it)
    bb, cp, cdim = mx_ref.shape
    bs = cp
    hg = GROUP_ROWS // bs
    ng = GDN_V_HEADS // hg
    rep = GDN_V_HEADS // GDN_K_HEADS
    key = GDN_K_HEADS * GDN_D
    n_items = bb * ng
    assert n_items == SUBLANES

    @pl.when(pl.program_id(1) == 0)
    def _():
        if has_state:
            s_ref[...] = s0_ref[...]
            xp_ref[:, 0:SUBLANES, :] = cs_ref[...]
        else:
            s_ref[...] = jnp.zeros_like(s_ref)
            xp_ref[:, 0:SUBLANES, :] = jnp.zeros((bb, SUBLANES, cdim), F32)

    base = SUBLANES - (GDN_CONV_W - 1)
    wc = wc_ref[...]
    for i in range(bb):
        xp_ref[i, SUBLANES:, :] = mx_ref[i].astype(F32)
        for c in range(cdim // LANES):
            sl = slice(c * LANES, (c + 1) * LANES)
            acc = wc[0:1, sl] * xp_ref[i, base:base + cp, sl]
            for t in range(1, GDN_CONV_W):
                acc = acc + wc[t:t + 1, sl] * xp_ref[i, base + t:base + t + cp, sl]
            acc = _silu(acc)
            if c < 2 * GDN_K_HEADS:
                acc = acc * lax.rsqrt(jnp.sum(acc * acc, axis=-1, keepdims=True) + EPS)
                if c < GDN_K_HEADS:
                    acc = acc * (GDN_D ** -0.5)
            qkv_ref[i, :, sl] = acc
        xp_ref[i, 0:SUBLANES, :] = xp_ref[i, cp:cp + SUBLANES, :]

    ri = lax.broadcasted_iota(jnp.int32, (GROUP_ROWS, GROUP_ROWS), 0)
    ci = lax.broadcasted_iota(jnp.int32, (GROUP_ROWS, GROUP_ROWS), 1)
    same = (ri // bs) == (ci // bs)
    incl = same & (ri >= ci)
    strict = same & (ri > ci)
    eye = (ri == ci).astype(F32)

    live = (lax.broadcasted_iota(jnp.int32, (1, GROUP_ROWS), 1) % bs) < c_real
    beta_rows = jnp.where(live, jax.nn.sigmoid(gt_ref[0]), 0.0)
    g_rows = jnp.where(live, -jnp.exp(al_ref[...]) * _softplus(gt_ref[1] + dt_ref[...]), 0.0)
    gsum_rows = jnp.dot(g_rows, (same & (ri <= ci)).astype(F32), precision=_HIGHEST,
                        preferred_element_type=F32)
    cols = jnp.concatenate([beta_rows, gsum_rows,
                            jnp.zeros((GROUP_ROWS - 2 * SUBLANES, GROUP_ROWS), F32)], axis=0).T

    items = [(i, j) for i in range(bb) for j in range(ng)]

    def stack(i, j, ref, col_of_head):
        return jnp.concatenate(
            [ref[i, :, col_of_head(j * hg + hh):col_of_head(j * hg + hh) + GDN_D] for hh in range(hg)],
            axis=0)

    qxs = [stack(i, j, qkv_ref, lambda h: (h // rep) * GDN_D) for i, j in items]
    kxs = [stack(i, j, qkv_ref, lambda h: key + (h // rep) * GDN_D) for i, j in items]
    vxs = [stack(i, j, qkv_ref, lambda h: 2 * key + h * GDN_D) for i, j in items]
    bcs = [jnp.broadcast_to(cols[:, m:m + 1], (GROUP_ROWS, GROUP_ROWS)) for m in range(n_items)]
    gcs = [jnp.broadcast_to(cols[:, SUBLANES + m:SUBLANES + m + 1], (GROUP_ROWS, GROUP_ROWS))
           for m in range(n_items)]
    kbs = [kx * bc for kx, bc in zip(kxs, bcs)]
    kks = [_dot_nt(kb, kx) for kb, kx in zip(kbs, kxs)]
    qks = [_dot_nt(qx, kx) for qx, kx in zip(qxs, kxs)]
    decays = [jnp.exp(jnp.where(incl, gc - gsum_rows[m:m + 1, :], -jnp.inf))
              for m, gc in enumerate(gcs)]
    ls = [jnp.where(strict, kk * d, 0.0) for kk, d in zip(kks, decays)]
    attns = [qk * d for qk, d in zip(qks, decays)]
    ps = _block_inverse_many(ls, ri, ci, bs)
    egs = [jnp.exp(gc) for gc in gcs]
    rhss = [jnp.concatenate([vx * bc, kb * eg], axis=-1) for vx, bc, kb, eg in zip(vxs, bcs, kbs, egs)]
    sols = [rhs + _dot(p - eye, rhs) for rhs, p in zip(rhss, ps)]
    qes = [qx * eg for qx, eg in zip(qxs, egs)]

    heads = [(m, i, j * hg + hh, slice(hh * bs, (hh + 1) * bs))
             for m, (i, j) in enumerate(items) for hh in range(hg)]
    wqs = [_dot(jnp.concatenate([sols[m][rs, GDN_D:], qes[m][rs, :]], axis=0), s_ref[i, h])
           for m, i, h, rs in heads]
    v_news, qss = [], []
    for m in range(n_items):
        part = wqs[m * hg:(m + 1) * hg]
        v_news.append(sols[m][:, :GDN_D] - jnp.concatenate([r[:bs] for r in part], axis=0))
        qss.append(jnp.concatenate([r[bs:] for r in part], axis=0))
    outs = [qs + _dot(attn, vn) for qs, attn, vn in zip(qss, attns, v_news)]

    def last_rows(gc):
        return [gc[hh * bs + c_real - 1:hh * bs + c_real, :] for hh in range(hg)]
    kds = [kx * jnp.exp(jnp.concatenate([jnp.broadcast_to(r, (bs, GROUP_ROWS)) for r in last_rows(gc)],
                                        axis=0) - gc)
           for kx, gc in zip(kxs, gcs)]
    upds = [_dot_tn(kds[m][rs, :], v_news[m][rs, :]) for m, i, h, rs in heads]
    for (m, i, h, rs), upd in zip(heads, upds):
        g_last = gcs[m][rs.start + c_real - 1:rs.start + c_real, :]
        s_ref[i, h] = s_ref[i, h] * jnp.exp(g_last) + upd

    nw = nw_ref[...]
    for m, i, h, rs in heads:
        o = outs[m][rs, :]
        o = o * lax.rsqrt(jnp.mean(o * o, axis=-1, keepdims=True) + EPS) * nw
        gate = _silu(z_ref[i, :, h * GDN_D:(h + 1) * GDN_D].astype(F32))
        o_ref[i, :, h * GDN_D:(h + 1) * GDN_D] = (o * gate).astype(o_ref.dtype)


def _gdn_core(proj3, gates, w_conv, a_log, dt_bias, norm_w, cs8, s0, c_real, cp, bb):
    B, Tp, _ = proj3.shape
    key = GDN_K_HEADS * GDN_D
    vd = GDN_V_HEADS * GDN_D
    cdim = 2 * key + vd
    nc = Tp // cp
    hg = GROUP_ROWS // cp
    ng = GDN_V_HEADS // hg
    has_state = s0 is not None
    lanes = lambda a: jnp.tile(jnp.repeat(a.reshape(ng, hg), cp, axis=1), (bb, 1))
    full = lambda shape: pl.BlockSpec(shape, lambda b, n: (0,) * len(shape))
    in_specs = [pl.BlockSpec((bb, cp, cdim), lambda b, n: (b, n, 0)),
                pl.BlockSpec((bb, cp, vd), lambda b, n: (b, n, cdim // vd)),
                pl.BlockSpec((None, 2, bb * ng, GROUP_ROWS), lambda b, n: (b * nc + n, 0, 0, 0)),
                full((GDN_CONV_W, cdim)), full((bb * ng, GROUP_ROWS)), full((bb * ng, GROUP_ROWS)),
                full((1, GDN_D))]
    args = [proj3, proj3, gates, w_conv, lanes(a_log), lanes(dt_bias), norm_w.reshape(1, GDN_D)]
    s_spec = pl.BlockSpec((bb, GDN_V_HEADS, GDN_D, GDN_D), lambda b, n: (b, 0, 0, 0))
    if has_state:
        in_specs += [pl.BlockSpec((bb, SUBLANES, cdim), lambda b, n: (b, 0, 0)), s_spec]
        args += [cs8, s0]
    return pl.pallas_call(
        functools.partial(_gdn_core_kernel, c_real=c_real, has_state=has_state),
        grid=(B // bb, nc),
        in_specs=in_specs,
        out_specs=[pl.BlockSpec((bb, cp, vd), lambda b, n: (b, n, 0)), s_spec],
        out_shape=[jax.ShapeDtypeStruct((B, Tp, vd), proj3.dtype),
                   jax.ShapeDtypeStruct((B, GDN_V_HEADS, GDN_D, GDN_D), F32)],
        scratch_shapes=[pltpu.VMEM((bb, cp + SUBLANES, cdim), F32), pltpu.VMEM((bb, cp, cdim), F32)],
        compiler_params=_cparams("parallel", "arbitrary"),
        name="gdn_core",
    )(*args)


def _rope_tables(pos0, t_real, t_pad):
    half = RET_DK // 2
    inv_freq = ROPE_BASE ** (-np.arange(half, dtype=np.float64) / half)
    pos = pos0 + np.arange(t_pad, dtype=np.float64)
    ang = pos[:, None] * inv_freq[None, :]
    live = (np.arange(t_pad) < t_real)[:, None]
    return (jnp.asarray(np.where(live, np.cos(ang), 0.0), F32),
            jnp.asarray(np.where(live, np.sin(ang), 0.0), F32))


def _pad_seq(a, t_pad):
    t = a.shape[1]
    if t == t_pad:
        return a
    return jnp.pad(a, ((0, 0), (0, t_pad - t), (0, 0)))


def _trunk(x, mods, mod_final, pos0, s_ret, s_gdn, s_gconv, s_fconv, p, wb, tm, act_dtype):
    B, T, D = x.shape
    rows = _Rows(B, T, tm)
    M = B * T
    xf = x.reshape(M, D)
    t_pad = max(T, SUBLANES)
    new_ret = new_gdn = new_gconv = None
    new_fconv = []
    for l in range(2):
        sh1, sc1, g1, sh2, sc2, g2 = (rows.mod(m) for m in jnp.split(mods[l], 6, axis=-1))
        if l == 0:
            proj = _norm_proj(rows, xf, p['norm_mix'][l], sh1, sc1, wb['ret_in'], act_dtype)
            cp = min(RET_CHUNK, t_pad)
            cos, sin = _rope_tables(pos0, T, t_pad)
            o3, new_ret = _ret_core(_pad_seq(proj.reshape(B, T, -1), t_pad), cos, sin,
                                    None if s_ret is None else s_ret[0], min(T, cp), cp)
            w_out = wb['ret_out']
        else:
            proj, bat = _norm_proj(rows, xf, p['norm_mix'][l], sh1, sc1, wb['gdn_in'], act_dtype,
                                   w_tail=wb['gdn_tail'])
            cp = min(GDN_CHUNK, t_pad)
            nc = t_pad // cp
            ng = GDN_V_HEADS * cp // GROUP_ROWS
            bb = SUBLANES // ng
            proj3 = proj.reshape(B, T, -1)
            cdim = (2 * GDN_K_HEADS + GDN_V_HEADS) * GDN_D
            new_gconv = proj3[:, T - (GDN_CONV_W - 1):, :cdim].astype(F32)
            gates = jnp.pad(bat.reshape(2, GDN_V_HEADS, B, T), ((0, 0), (0, 0), (0, 0), (0, t_pad - T)))
            gates = gates.reshape(2, ng, GROUP_ROWS // cp, B // bb, bb, nc, cp)
            gates = gates.transpose(3, 5, 0, 4, 1, 2, 6).reshape(B // bb * nc, 2, bb * ng, GROUP_ROWS)
            cs8 = None
            if s_gconv is not None:
                cs8 = jnp.pad(s_gconv[0], ((0, 0), (SUBLANES - (GDN_CONV_W - 1), 0), (0, 0)))
            o3, new_gdn = _gdn_core(_pad_seq(proj3, t_pad), gates, p['w_gdn_conv'][0],
                                    p['gdn_a_log'][0], p['gdn_dt_bias'][0], p['gdn_norm'][0],
                                    cs8, None if s_gdn is None else s_gdn[0], min(T, cp), cp, bb)
            w_out = wb['gdn_out']
        o = o3[:, :T].reshape(M, -1)
        xf = _out_proj(rows, o, w_out, xf, g1)
        prev = None
        if s_fconv is not None:
            buf = s_fconv[l]
            zero = jnp.zeros_like(buf[:, :1])
            fill = jnp.zeros((B, T - 2, buf.shape[-1]), F32)
            p1 = jnp.concatenate([buf[:, 1:2], zero, fill], axis=1).reshape(M, -1)
            p2 = jnp.concatenate([buf[:, 0:1], buf[:, 1:2], fill], axis=1).reshape(M, -1)
            prev = (p1, p2)
        final_mod = None
        if l == 1:
            shf, scf = (rows.mod(m) for m in jnp.split(mod_final, 2, axis=-1))
            final_mod = (p['norm_final'], shf, scf)
        xf, cst = _ffn(rows, xf, p['norm_ffn'][l], sh2, sc2, g2, wb['ffn_up'][l], p['w_ffn_dw'][l],
                       p['b_ffn_dw'][l], wb['ffn_down'][l], prev=prev, final_mod=final_mod)
        if s_fconv is not None:
            cst = cst.reshape(B, T, -1)[:, T - (FFN_CONV_W - 1):]
        new_fconv.append(cst)
    return (xf.reshape(B, T, D), new_ret[None], new_gdn[None], new_gconv[None], jnp.stack(new_fconv))


def kernel(x_prompt, x_sample, state_ret, state_gdn, state_gdn_conv, state_ffn_conv, c_prompt, c_sample, w_ada, b_ada, w_ada_final, b_ada_final, norm_mix, norm_ffn, norm_final, w_ret_in, w_ret_out, w_gdn_in, w_gdn_conv, gdn_a_log, gdn_dt_bias, gdn_norm, w_gdn_out, w_ffn_up, w_ffn_dw, b_ffn_dw, w_ffn_down):
    p = {'norm_mix': norm_mix, 'norm_ffn': norm_ffn, 'norm_final': norm_final,
         'w_gdn_conv': w_gdn_conv, 'gdn_a_log': gdn_a_log, 'gdn_dt_bias': gdn_dt_bias,
         'gdn_norm': gdn_norm, 'w_ffn_dw': w_ffn_dw, 'b_ffn_dw': b_ffn_dw}
    cdim = (2 * GDN_K_HEADS + GDN_V_HEADS) * GDN_D
    vd = GDN_V_HEADS * GDN_D
    tail = w_gdn_in[0][:, cdim + vd:]
    wb = {'ret_in': w_ret_in[0].astype(BF16), 'ret_out': w_ret_out[0].astype(BF16),
          'gdn_in': w_gdn_in[0][:, :cdim + vd].astype(BF16), 'gdn_out': w_gdn_out[0].astype(BF16),
          'gdn_tail': tail,
          'ffn_up': w_ffn_up.astype(BF16), 'ffn_down': w_ffn_down.astype(BF16)}

    bp = c_prompt.shape[0]
    c_all = jnp.concatenate([c_prompt, c_sample], axis=0)
    mods = _ada_mod(c_all, w_ada, b_ada)
    mod_final = _ada_mod(c_all, w_ada_final[None], b_ada_final[None])[0]

    out_p = _trunk(x_prompt, mods[:, :bp], mod_final[:bp], 0, None, None, None, None, p, wb,
                   tm=512, act_dtype=BF16)
    out_s = _trunk(x_sample, mods[:, bp:], mod_final[bp:], PAST_LEN, state_ret, state_gdn,
                   state_gdn_conv, state_ffn_conv, p, wb,
                   tm=min(256, x_sample.shape[0] * x_sample.shape[1]), act_dtype=F32)
    y_p, ret_p, gdn_p, gconv_p, fconv_p = out_p
    y_s, ret_s, gdn_s, gconv_s, fconv_s = out_s
    return (y_p, y_s, ret_p, gdn_p, gconv_p, fconv_p, ret_s, gdn_s, gconv_s, fconv_s)
```

```python
import functools
import math

import numpy as np
import jax
import jax.numpy as jnp
from jax import lax
from jax.experimental import pallas as pl
from jax.experimental.pallas import tpu as pltpu

F32 = jnp.float32
BF16 = jnp.bfloat16
EPS = 1e-6
ROPE_BASE = 10000.0
PAST_LEN = 16384

RET_HEADS = 4
RET_DK = 256
RET_DV = 512
RET_CHUNK = 128
GDN_K_HEADS = 8
GDN_V_HEADS = 16
GDN_D = 128
GDN_CHUNK = 64
GDN_CONV_W = 4
FFN_CONV_W = 3
SUBLANES = 8
LANES = 128
INV_BASE = 8
GROUP_ROWS = 128
PROJ_COLS = 512
FFN_COLS = 256
VMEM_LIMIT = 48 * 1024 * 1024

_HIGHEST = lax.Precision.HIGHEST


def _cparams(*sem):
    return pltpu.CompilerParams(dimension_semantics=sem, vmem_limit_bytes=VMEM_LIMIT)


def _silu(x):
    hx = 0.5 * x
    return hx + hx * jnp.tanh(hx)


def _softplus(x):
    return jnp.maximum(x, 0.0) + jnp.log1p(jnp.exp(-jnp.abs(x)))


def _ada_norm(x, gamma, shift, scale):
    ms = jnp.mean(x * x, axis=-1, keepdims=True)
    xn = x * lax.rsqrt(ms + EPS) * gamma
    return xn * (1.0 + scale) + shift


def _dot(a, b):
    return jnp.dot(a.astype(BF16), b.astype(BF16), preferred_element_type=F32)


def _dot_nt(a, b):
    return lax.dot_general(a.astype(BF16), b.astype(BF16), (((1,), (1,)), ((), ())),
                           preferred_element_type=F32)


def _dot_tn(a, b):
    return lax.dot_general(a.astype(BF16), b.astype(BF16), (((0,), (0,)), ((), ())),
                           preferred_element_type=F32)


def _ada_kernel(c_ref, w_ref, b_ref, o_ref):
    cs = _silu(c_ref[...])
    o_ref[...] = _dot(cs, w_ref[...]) + b_ref[...]


def _ada_mod(c, w, b, tn=1024):
    L, D, N = w.shape
    Mc = c.shape[0]
    return pl.pallas_call(
        _ada_kernel,
        grid=(L, N // tn),
        in_specs=[pl.BlockSpec((Mc, D), lambda l, j: (0, 0)),
                  pl.BlockSpec((None, D, tn), lambda l, j: (l, 0, j)),
                  pl.BlockSpec((None, 1, tn), lambda l, j: (l, 0, j))],
        out_specs=pl.BlockSpec((None, Mc, tn), lambda l, j: (l, 0, j)),
        out_shape=jax.ShapeDtypeStruct((L, Mc, N), F32),
        compiler_params=_cparams("parallel", "parallel"),
        name="ada_mod",
    )(c, w, b.reshape(L, 1, N))


class _Tiles:
    def __init__(self, B, Tp, bt, tt):
        assert bt == 1 or tt == Tp
        self.B, self.Tp, self.bt, self.tt = B, Tp, bt, tt
        self.tps = Tp // tt
        self.n = (B // bt) * self.tps
        self.tm = bt * tt

    def x_spec(self, D):
        tps = self.tps
        return pl.BlockSpec((self.bt, self.tt, D), lambda i: (i // tps, i % tps, 0))

    def seq_spec(self, r, D):
        tps = self.tps
        return pl.BlockSpec((self.bt, r, D), lambda i: (i // tps, 0, 0))

    def rows_spec(self, N):
        return pl.BlockSpec((self.tm, N), lambda i: (i, 0))

    def pos_spec(self, N):
        tps = self.tps
        return pl.BlockSpec((self.tt, N), lambda i: (i % tps, 0))

    def tail_rows_spec(self, r, N):
        return pl.BlockSpec((self.bt, r, N), lambda i: (i, 0, 0))

    def last_tile(self, a):
        return a[self.tps - 1::self.tps]


def _resident(shape):
    return pl.BlockSpec(shape, lambda i: (0,) * len(shape), pipeline_mode=pl.Buffered(1))


def _mod(m3, k, D):
    return m3[:, :, k * D:(k + 1) * D]


def _normed_rows(x_ref, gam_ref, mod_ref, k_shift, h_ref):
    D = x_ref.shape[-1]
    m3 = mod_ref[...]
    h3 = _ada_norm(x_ref[...], gam_ref[...], _mod(m3, k_shift, D), _mod(m3, k_shift + 1, D))
    h_ref[...] = h3.reshape(h_ref.shape).astype(BF16)
    return h3


def _causal_taps(g3, prev_ref, sl, width):
    ext = jnp.concatenate([prev_ref[:, :, sl], g3], axis=1)
    return [pltpu.roll(ext, k, axis=1)[:, SUBLANES:, :] for k in range(width - 1, 0, -1)]


def _start_of_sequence(tiles):
    return (pl.program_id(0) % tiles.tps) == 0


def _ret_proj_kernel(x_ref, gam_ref, mod_ref, w_ref, cos_ref, sin_ref, o_ref, h_ref):
    bt, tt, _ = x_ref.shape
    _normed_rows(x_ref, gam_ref, mod_ref, 0, h_ref)
    cos, sin = cos_ref[...], sin_ref[...]
    qk = RET_HEADS * RET_DK
    half = RET_DK // 2
    for j in range(w_ref.shape[1] // PROJ_COLS):
        sl = slice(j * PROJ_COLS, (j + 1) * PROJ_COLS)
        y = jnp.dot(h_ref[...], w_ref[:, sl], preferred_element_type=F32)
        if sl.start < 2 * qk:
            y3 = y.reshape(bt, tt, PROJ_COLS)
            parts = []
            for c in range(PROJ_COLS // RET_DK):
                x1 = y3[:, :, c * RET_DK:c * RET_DK + half]
                x2 = y3[:, :, c * RET_DK + half:(c + 1) * RET_DK]
                parts += [x1 * cos - x2 * sin, x1 * sin + x2 * cos]
            y = jnp.concatenate(parts, axis=-1).reshape(bt * tt, PROJ_COLS)
            if sl.start >= qk:
                y = y * (RET_DK ** -0.5)
        o_ref[:, sl] = y.astype(o_ref.dtype)


def _ret_proj(tiles, x3, gamma, mod3, w, cos, sin, out_dtype):
    B, Tp, D = x3.shape
    N = w.shape[1]
    return pl.pallas_call(
        _ret_proj_kernel,
        grid=(tiles.n,),
        in_specs=[tiles.x_spec(D), _resident((1, D)), tiles.seq_spec(1, mod3.shape[-1]),
                  _resident((D, N)), tiles.pos_spec(cos.shape[1]), tiles.pos_spec(sin.shape[1])],
        out_specs=tiles.rows_spec(N),
        out_shape=jax.ShapeDtypeStruct((B * Tp, N), out_dtype),
        scratch_shapes=[pltpu.VMEM((tiles.tm, D), BF16)],
        compiler_params=_cparams("parallel"),
        name="ret_proj",
    )(x3, gamma.reshape(1, D), mod3, w, cos, sin)


def _gdn_proj_kernel(*refs, tiles, c_real, has_state):
    it = iter(refs)
    x_ref, gam_ref, mod_ref, w_ref, wtt_ref, wc_ref = (next(it) for _ in range(6))
    if has_state:
        cs_ref = next(it)
    o_ref, ott_ref, cst_ref = next(it), next(it), next(it)
    h_ref, prev_ref = next(it), next(it)
    bt, tt, _ = x_ref.shape
    key = GDN_K_HEADS * GDN_D
    cdim = wc_ref.shape[1]
    keep = GDN_CONV_W - 1

    h3 = _normed_rows(x_ref, gam_ref, mod_ref, 0, h_ref)
    ott_ref[...] = lax.dot_general(wtt_ref[...], h3.reshape(bt * tt, -1), (((1,), (1,)), ((), ())),
                                   precision=_HIGHEST, preferred_element_type=F32)
    if has_state:
        prev_ref[...] = jnp.zeros_like(prev_ref)
        prev_ref[:, SUBLANES - keep:, :] = cs_ref[...]
    else:
        @pl.when(_start_of_sequence(tiles))
        def _():
            prev_ref[...] = jnp.zeros_like(prev_ref)

    for j in range(w_ref.shape[1] // PROJ_COLS):
        sl = slice(j * PROJ_COLS, (j + 1) * PROJ_COLS)
        y = jnp.dot(h_ref[...], w_ref[:, sl], preferred_element_type=F32)
        if sl.start < cdim:
            y3 = y.reshape(bt, tt, PROJ_COLS)
            taps = _causal_taps(y3, prev_ref, sl, GDN_CONV_W)
            acc = wc_ref[keep:keep + 1, sl] * y3
            for i, tap in enumerate(taps):
                acc = acc + wc_ref[i:i + 1, sl] * tap
            acc = _silu(acc)
            if sl.start < 2 * key:
                parts = []
                for c in range(PROJ_COLS // GDN_D):
                    a = acc[:, :, c * GDN_D:(c + 1) * GDN_D]
                    inv = lax.rsqrt(jnp.sum(a * a, axis=-1, keepdims=True) + EPS)
                    parts.append(a * (inv * (GDN_D ** -0.5) if sl.start < key else inv))
                acc = jnp.concatenate(parts, axis=-1)
            cst_ref[:, :, sl] = y3[:, c_real - keep:c_real, :]
            if not has_state:
                prev_ref[:, :, sl] = y3[:, tt - SUBLANES:, :]
            y = acc.reshape(bt * tt, PROJ_COLS)
        o_ref[:, sl] = y.astype(o_ref.dtype)


def _gdn_proj(tiles, x3, gamma, mod3, w, w_tail, w_conv, conv_state, c_real, out_dtype):
    B, Tp, D = x3.shape
    N = w.shape[1]
    nt = w_tail.shape[1]
    cdim = w_conv.shape[1]
    keep = GDN_CONV_W - 1
    has_state = conv_state is not None
    in_specs = [tiles.x_spec(D), _resident((1, D)), tiles.seq_spec(1, mod3.shape[-1]),
                _resident((D, N)), _resident((nt, D)), _resident((GDN_CONV_W, cdim))]
    args = [x3, gamma.reshape(1, D), mod3, w, w_tail.T, w_conv]
    if has_state:
        in_specs.append(tiles.seq_spec(keep, cdim))
        args.append(conv_state)
    proj, ott, cst = pl.pallas_call(
        functools.partial(_gdn_proj_kernel, tiles=tiles, c_real=c_real, has_state=has_state),
        grid=(tiles.n,),
        in_specs=in_specs,
        out_specs=[tiles.rows_spec(N), pl.BlockSpec((nt, tiles.tm), lambda i: (0, i)),
                   tiles.tail_rows_spec(keep, cdim)],
        out_shape=[jax.ShapeDtypeStruct((B * Tp, N), out_dtype),
                   jax.ShapeDtypeStruct((nt, B * Tp), F32),
                   jax.ShapeDtypeStruct((tiles.n * tiles.bt, keep, cdim), F32)],
        scratch_shapes=[pltpu.VMEM((tiles.tm, D), BF16),
                        pltpu.VMEM((tiles.bt, SUBLANES, cdim), F32)],
        compiler_params=_cparams("arbitrary"),
        name="gdn_proj",
    )(*args)
    return proj, ott, tiles.last_tile(cst)


def _out_proj_kernel(o_ref, w_ref, x_ref, mod_ref, y_ref):
    D = x_ref.shape[-1]
    acc = _dot(o_ref[...], w_ref[...]).reshape(x_ref.shape)
    y_ref[...] = x_ref[...] + _mod(mod_ref[...], 2, D) * acc


def _out_proj(tiles, o, w, x3, mod3):
    B, Tp, D = x3.shape
    K = o.shape[1]
    return pl.pallas_call(
        _out_proj_kernel,
        grid=(tiles.n,),
        in_specs=[tiles.rows_spec(K), _resident((K, D)), tiles.x_spec(D),
                  tiles.seq_spec(1, mod3.shape[-1])],
        out_specs=tiles.x_spec(D),
        out_shape=jax.ShapeDtypeStruct((B, Tp, D), F32),
        compiler_params=_cparams("parallel"),
        name="out_proj",
    )(o, w, x3, mod3)


def _ffn_kernel(*refs, tiles, c_real, has_state, final):
    it = iter(refs)
    x_ref, gam_ref, mod_ref, wup_ref, wdw_ref, bdw_ref, wd_ref = (next(it) for _ in range(7))
    if has_state:
        cs_ref = next(it)
    if final:
        gamf_ref, modf_ref = next(it), next(it)
    y_ref, cst_ref = next(it), next(it)
    h_ref, act_ref, prev_ref = (next(it) for _ in range(3))
    bt, tt, D = x_ref.shape
    fd = wd_ref.shape[0]
    keep = FFN_CONV_W - 1

    _normed_rows(x_ref, gam_ref, mod_ref, 3, h_ref)
    if has_state:
        prev_ref[...] = jnp.zeros_like(prev_ref)
        prev_ref[:, SUBLANES - keep:, :] = cs_ref[...]
    else:
        @pl.when(_start_of_sequence(tiles))
        def _():
            prev_ref[...] = jnp.zeros_like(prev_ref)

    for f in range(fd // FFN_COLS):
        sl = slice(f * FFN_COLS, (f + 1) * FFN_COLS)
        h = h_ref[...]
        g3 = jnp.dot(h, wup_ref[:, sl], preferred_element_type=F32).reshape(bt, tt, FFN_COLS)
        val = jnp.dot(h, wup_ref[:, fd + sl.start:fd + sl.stop], preferred_element_type=F32)
        s2, s1 = _causal_taps(g3, prev_ref, sl, FFN_CONV_W)
        conv = (wdw_ref[0:1, sl] * s2 + wdw_ref[1:2, sl] * s1 + wdw_ref[2:3, sl] * g3
                + bdw_ref[:, sl])
        cst_ref[:, :, sl] = g3[:, c_real - keep:c_real, :]
        if not has_state:
            prev_ref[:, :, sl] = g3[:, tt - SUBLANES:, :]
        act_ref[:, sl] = (_silu(conv).reshape(bt * tt, FFN_COLS) * val).astype(BF16)

    acc = jnp.dot(act_ref[...], wd_ref[...], preferred_element_type=F32).reshape(bt, tt, D)
    xn = x_ref[...] + _mod(mod_ref[...], 5, D) * acc
    if final:
        mf = modf_ref[...]
        xn = _ada_norm(xn, gamf_ref[...], _mod(mf, 0, D), _mod(mf, 1, D))
    y_ref[...] = xn


def _ffn(tiles, x3, gamma, mod3, w_up, w_dw, b_dw, w_down, conv_state, c_real, final=None):
    B, Tp, D = x3.shape
    Fd = w_down.shape[0]
    keep = FFN_CONV_W - 1
    has_state = conv_state is not None
    in_specs = [tiles.x_spec(D), _resident((1, D)), tiles.seq_spec(1, mod3.shape[-1]),
                _resident((D, 2 * Fd)), _resident((FFN_CONV_W, Fd)), _resident((1, Fd)),
                _resident((Fd, D))]
    args = [x3, gamma.reshape(1, D), mod3, w_up, w_dw, b_dw.reshape(1, Fd), w_down]
    if has_state:
        in_specs.append(tiles.seq_spec(keep, Fd))
        args.append(conv_state)
    if final is not None:
        in_specs += [_resident((1, D)), tiles.seq_spec(1, final[1].shape[-1])]
        args += [final[0].reshape(1, D), final[1]]
    y3, cst = pl.pallas_call(
        functools.partial(_ffn_kernel, tiles=tiles, c_real=c_real, has_state=has_state,
                          final=final is not None),
        grid=(tiles.n,),
        in_specs=in_specs,
        out_specs=[tiles.x_spec(D), tiles.tail_rows_spec(keep, Fd)],
        out_shape=[jax.ShapeDtypeStruct((B, Tp, D), F32),
                   jax.ShapeDtypeStruct((tiles.n * tiles.bt, keep, Fd), F32)],
        scratch_shapes=[pltpu.VMEM((tiles.tm, D), BF16), pltpu.VMEM((tiles.tm, Fd), BF16),
                        pltpu.VMEM((tiles.bt, SUBLANES, Fd), F32)],
        compiler_params=_cparams("arbitrary"),
        name="conv_ffn",
    )(*args)
    return y3, tiles.last_tile(cst)


def _ret_core_kernel(*refs, c_real, has_state):
    it = iter(refs)
    q_ref, k_ref, v_ref, g_ref = (next(it) for _ in range(4))
    if has_state:
        s0_ref = next(it)
    o_ref, s_ref = next(it), next(it)
    cp = q_ref.shape[0]

    @pl.when(pl.program_id(1) == 0)
    def _():
        if has_state:
            s_ref[...] = s0_ref[...]
        else:
            s_ref[...] = jnp.zeros_like(s_ref)

    ri = lax.broadcasted_iota(jnp.int32, (cp, cp), 0)
    ci = lax.broadcasted_iota(jnp.int32, (cp, cp), 1)
    causal = ri >= ci
    diff = jnp.where(causal, ri - ci, 0).astype(F32)
    rowi = lax.broadcasted_iota(jnp.int32, (cp, 1), 0)
    row = rowi.astype(F32)
    heads = range(RET_HEADS)
    lgs = [math.log(1.0 - 2.0 ** (-5.0 - h)) for h in heads]

    qs = [q_ref[:, h * RET_DK:(h + 1) * RET_DK] for h in heads]
    ks = [k_ref[:, h * RET_DK:(h + 1) * RET_DK].astype(F32) for h in heads]
    vs = [v_ref[:, h * RET_DV:(h + 1) * RET_DV] for h in heads]
    ss = [s_ref[h] for h in heads]
    scores = [_dot_nt(q, k) for q, k in zip(qs, ks)]
    cross = [_dot(q, s) for q, s in zip(qs, ss)]
    kds = [k * jnp.where(rowi < c_real, jnp.exp((c_real - 1.0 - row) * lg), 0.0)
           for k, lg in zip(ks, lgs)]
    upds = [_dot_tn(kd, v) for kd, v in zip(kds, vs)]
    intras = [sc * jnp.where(causal, jnp.exp(diff * lg), 0.0) for sc, lg in zip(scores, lgs)]
    inner = [_dot(a, v) for a, v in zip(intras, vs)]
    for h in heads:
        s_ref[h] = ss[h] * math.exp(c_real * lgs[h]) + upds[h]
        o = inner[h] + cross[h] * jnp.exp((row + 1.0) * lgs[h])
        o = o * lax.rsqrt(jnp.mean(o * o, axis=-1, keepdims=True) + EPS)
        gate = _silu(g_ref[:, h * RET_DV:(h + 1) * RET_DV].astype(F32))
        o_ref[:, h * RET_DV:(h + 1) * RET_DV] = (o * gate).astype(o_ref.dtype)


def _ret_core(proj3, s0, c_real, cp):
    B, Tp, _ = proj3.shape
    qk = RET_HEADS * RET_DK
    vd = RET_HEADS * RET_DV
    has_state = s0 is not None
    in_specs = [pl.BlockSpec((None, cp, qk), lambda b, n: (b, n, 0)),
                pl.BlockSpec((None, cp, qk), lambda b, n: (b, n, 1)),
                pl.BlockSpec((None, cp, vd), lambda b, n: (b, n, 1)),
                pl.BlockSpec((None, cp, vd), lambda b, n: (b, n, 2))]
    args = [proj3, proj3, proj3, proj3]
    s_spec = pl.BlockSpec((None, RET_HEADS, RET_DK, RET_DV), lambda b, n: (b, 0, 0, 0))
    if has_state:
        in_specs.append(s_spec)
        args.append(s0)
    return pl.pallas_call(
        functools.partial(_ret_core_kernel, c_real=c_real, has_state=has_state),
        grid=(B, Tp // cp),
        in_specs=in_specs,
        out_specs=[pl.BlockSpec((None, cp, vd), lambda b, n: (b, n, 0)), s_spec],
        out_shape=[jax.ShapeDtypeStruct((B, Tp, vd), proj3.dtype),
                   jax.ShapeDtypeStruct((B, RET_HEADS, RET_DK, RET_DV), F32)],
        compiler_params=_cparams("parallel", "arbitrary"),
        name="ret_core",
    )(*args)


def _block_inverse_many(ls, ri, ci, bs):
    eye = (ri == ci).astype(F32)
    base = (ri // INV_BASE) == (ci // INV_BASE)
    pws = [-jnp.where(base, l, 0.0) for l in ls]
    ps = [eye + m for m in pws]
    span = 2
    while span < INV_BASE:
        pws = [_dot(pw, pw) for pw in pws]
        ps = [p + _dot(p, pw) for p, pw in zip(ps, pws)]
        span *= 2
    size = INV_BASE
    while size < bs:
        off = ((ri // (2 * size)) == (ci // (2 * size))) & ((ri // size) != (ci // size))
        xs = [_dot(jnp.where(off, l, 0.0), p) for l, p in zip(ls, ps)]
        ps = [p - _dot(p, x) for p, x in zip(ps, xs)]
        size *= 2
    return ps


def _gdn_core_kernel(*refs, c_real, has_state):
    it = iter(refs)
    qkv_ref, z_ref, gt_ref, al_ref, dt_ref, nw_ref = (next(it) for _ in range(6))
    if has_state:
        s0_ref = next(it)
    o_ref, s_ref = next(it), next(it)
    bb, cp, _ = qkv_ref.shape
    bs = cp
    hg = GROUP_ROWS // bs
    ng = GDN_V_HEADS // hg
    rep = GDN_V_HEADS // GDN_K_HEADS
    key = GDN_K_HEADS * GDN_D
    n_items = bb * ng
    assert n_items % SUBLANES == 0

    @pl.when(pl.program_id(1) == 0)
    def _():
        if has_state:
            s_ref[...] = s0_ref[...]
        else:
            s_ref[...] = jnp.zeros_like(s_ref)

    ri = lax.broadcasted_iota(jnp.int32, (GROUP_ROWS, GROUP_ROWS), 0)
    ci = lax.broadcasted_iota(jnp.int32, (GROUP_ROWS, GROUP_ROWS), 1)
    same = (ri // bs) == (ci // bs)
    incl = same & (ri >= ci)
    strict = same & (ri > ci)
    eye = (ri == ci).astype(F32)

    live = (lax.broadcasted_iota(jnp.int32, (1, GROUP_ROWS), 1) % bs) < c_real
    beta_rows = jnp.where(live, jax.nn.sigmoid(gt_ref[0]), 0.0)
    g_rows = jnp.where(live, -jnp.exp(al_ref[...]) * _softplus(gt_ref[1] + dt_ref[...]), 0.0)
    gsum_rows = jnp.dot(g_rows, (same & (ri <= ci)).astype(F32), precision=_HIGHEST,
                        preferred_element_type=F32)
    cols = jnp.concatenate([beta_rows, gsum_rows,
                            jnp.zeros((GROUP_ROWS - 2 * n_items, GROUP_ROWS), F32)], axis=0).T

    items = [(i, j) for i in range(bb) for j in range(ng)]

    def stack(i, j, col_of_head):
        return jnp.concatenate(
            [qkv_ref[i, :, col_of_head(j * hg + hh):col_of_head(j * hg + hh) + GDN_D].astype(F32)
             for hh in range(hg)], axis=0)

    qxs = [stack(i, j, lambda h: (h // rep) * GDN_D) for i, j in items]
    kxs = [stack(i, j, lambda h: key + (h // rep) * GDN_D) for i, j in items]
    vxs = [stack(i, j, lambda h: 2 * key + h * GDN_D) for i, j in items]
    bcs = [jnp.broadcast_to(cols[:, m:m + 1], (GROUP_ROWS, GROUP_ROWS)) for m in range(n_items)]
    gcs = [jnp.broadcast_to(cols[:, n_items + m:n_items + m + 1], (GROUP_ROWS, GROUP_ROWS))
           for m in range(n_items)]
    kbs = [kx * bc for kx, bc in zip(kxs, bcs)]
    kks = [_dot_nt(kb, kx) for kb, kx in zip(kbs, kxs)]
    qks = [_dot_nt(qx, kx) for qx, kx in zip(qxs, kxs)]
    decays = [jnp.exp(jnp.where(incl, gc - gsum_rows[m:m + 1, :], -jnp.inf))
              for m, gc in enumerate(gcs)]
    ls = [jnp.where(strict, kk * d, 0.0) for kk, d in zip(kks, decays)]
    attns = [qk * d for qk, d in zip(qks, decays)]
    ps = _block_inverse_many(ls, ri, ci, bs)
    egs = [jnp.exp(gc) for gc in gcs]
    rhss = [jnp.concatenate([vx * bc, kb * eg], axis=-1) for vx, bc, kb, eg in zip(vxs, bcs, kbs, egs)]
    sols = [rhs + _dot(p - eye, rhs) for rhs, p in zip(rhss, ps)]
    qes = [qx * eg for qx, eg in zip(qxs, egs)]

    heads = [(m, i, j * hg + hh, slice(hh * bs, (hh + 1) * bs))
             for m, (i, j) in enumerate(items) for hh in range(hg)]
    wqs = [_dot(jnp.concatenate([sols[m][rs, GDN_D:], qes[m][rs, :]], axis=0), s_ref[i, h])
           for m, i, h, rs in heads]
    v_news, qss = [], []
    for m in range(n_items):
        part = wqs[m * hg:(m + 1) * hg]
        v_news.append(sols[m][:, :GDN_D] - jnp.concatenate([r[:bs] for r in part], axis=0))
        qss.append(jnp.concatenate([r[bs:] for r in part], axis=0))
    outs = [qs + _dot(attn, vn) for qs, attn, vn in zip(qss, attns, v_news)]

    def last_rows(gc):
        return [gc[hh * bs + c_real - 1:hh * bs + c_real, :] for hh in range(hg)]
    kds = [kx * jnp.exp(jnp.concatenate([jnp.broadcast_to(r, (bs, GROUP_ROWS)) for r in last_rows(gc)],
                                        axis=0) - gc)
           for kx, gc in zip(kxs, gcs)]
    upds = [_dot_tn(kds[m][rs, :], v_news[m][rs, :]) for m, i, h, rs in heads]
    for (m, i, h, rs), upd in zip(heads, upds):
        g_last = gcs[m][rs.start + c_real - 1:rs.start + c_real, :]
        s_ref[i, h] = s_ref[i, h] * jnp.exp(g_last) + upd

    nw = nw_ref[...]
    for m, i, h, rs in heads:
        o = outs[m][rs, :]
        o = o * lax.rsqrt(jnp.mean(o * o, axis=-1, keepdims=True) + EPS) * nw
        gate = _silu(z_ref[i, :, h * GDN_D:(h + 1) * GDN_D].astype(F32))
        o_ref[i, :, h * GDN_D:(h + 1) * GDN_D] = (o * gate).astype(o_ref.dtype)


def _gdn_core(proj3, gates, a_log, dt_bias, norm_w, s0, c_real, cp, bb):
    B, Tp, _ = proj3.shape
    key = GDN_K_HEADS * GDN_D
    vd = GDN_V_HEADS * GDN_D
    cdim = 2 * key + vd
    nc = Tp // cp
    hg = GROUP_ROWS // cp
    ng = GDN_V_HEADS // hg
    has_state = s0 is not None
    lanes = lambda a: jnp.tile(jnp.repeat(a.reshape(ng, hg), cp, axis=1), (bb, 1))
    full = lambda shape: pl.BlockSpec(shape, lambda b, n: (0,) * len(shape))
    in_specs = [pl.BlockSpec((bb, cp, cdim), lambda b, n: (b, n, 0)),
                pl.BlockSpec((bb, cp, vd), lambda b, n: (b, n, cdim // vd)),
                pl.BlockSpec((None, 2, bb * ng, GROUP_ROWS), lambda b, n: (b * nc + n, 0, 0, 0)),
                full((bb * ng, GROUP_ROWS)), full((bb * ng, GROUP_ROWS)), full((1, GDN_D))]
    args = [proj3, proj3, gates, lanes(a_log), lanes(dt_bias), norm_w.reshape(1, GDN_D)]
    s_spec = pl.BlockSpec((bb, GDN_V_HEADS, GDN_D, GDN_D), lambda b, n: (b, 0, 0, 0))
    if has_state:
        in_specs.append(s_spec)
        args.append(s0)
    return pl.pallas_call(
        functools.partial(_gdn_core_kernel, c_real=c_real, has_state=has_state),
        grid=(B // bb, nc),
        in_specs=in_specs,
        out_specs=[pl.BlockSpec((bb, cp, vd), lambda b, n: (b, n, 0)), s_spec],
        out_shape=[jax.ShapeDtypeStruct((B, Tp, vd), proj3.dtype),
                   jax.ShapeDtypeStruct((B, GDN_V_HEADS, GDN_D, GDN_D), F32)],
        compiler_params=_cparams("parallel", "arbitrary"),
        name="gdn_core",
    )(*args)


def _rope_tables(pos0, t_real, t_pad):
    half = RET_DK // 2
    inv_freq = ROPE_BASE ** (-np.arange(half, dtype=np.float64) / half)
    pos = pos0 + np.arange(t_pad, dtype=np.float64)
    ang = pos[:, None] * inv_freq[None, :]
    live = (np.arange(t_pad) < t_real)[:, None]
    return (jnp.asarray(np.where(live, np.cos(ang), 0.0), F32),
            jnp.asarray(np.where(live, np.sin(ang), 0.0), F32))


def _trunk(x, mods, mod_final, pos0, s_ret, s_gdn, s_gconv, s_fconv, p, wb, tiles, act_dtype):
    B, T, D = x.shape
    Tp = tiles.Tp
    x3 = x if T == Tp else jnp.pad(x, ((0, 0), (0, Tp - T), (0, 0)))
    M = B * Tp
    stateful = s_ret is not None
    new_fconv = []

    mod3 = mods[0][:, None, :]
    cos, sin = _rope_tables(pos0, T, Tp)
    proj = _ret_proj(tiles, x3, p['norm_mix'][0], mod3, wb['ret_in'], cos, sin, act_dtype)
    cp = min(RET_CHUNK, Tp)
    o3, new_ret = _ret_core(proj.reshape(B, Tp, -1), s_ret[0] if stateful else None, min(T, cp), cp)
    x3 = _out_proj(tiles, o3.reshape(M, -1), wb['ret_out'], x3, mod3)
    x3, cst = _ffn(tiles, x3, p['norm_ffn'][0], mod3, wb['ffn_up'][0], p['w_ffn_dw'][0],
                   p['b_ffn_dw'][0], wb['ffn_down'][0], s_fconv[0] if stateful else None, min(T, tiles.tt))
    new_fconv.append(cst)

    mod3 = mods[1][:, None, :]
    cp = min(GDN_CHUNK, Tp)
    nc = Tp // cp
    ng = GDN_V_HEADS * cp // GROUP_ROWS
    bb = max(SUBLANES // ng, 1)
    proj, bat, new_gconv = _gdn_proj(tiles, x3, p['norm_mix'][1], mod3, wb['gdn_in'], wb['gdn_tail'],
                                     p['w_gdn_conv'][0], s_gconv[0] if stateful else None,
                                     min(T, tiles.tt), act_dtype)
    gates = bat.reshape(2, ng, GROUP_ROWS // cp, B // bb, bb, nc, cp)
    gates = gates.transpose(3, 5, 0, 4, 1, 2, 6).reshape(B // bb * nc, 2, bb * ng, GROUP_ROWS)
    o3, new_gdn = _gdn_core(proj.reshape(B, Tp, -1), gates, p['gdn_a_log'][0], p['gdn_dt_bias'][0],
                            p['gdn_norm'][0], s_gdn[0] if stateful else None, min(T, cp), cp, bb)
    x3 = _out_proj(tiles, o3.reshape(M, -1), wb['gdn_out'], x3, mod3)
    y3, cst = _ffn(tiles, x3, p['norm_ffn'][1], mod3, wb['ffn_up'][1], p['w_ffn_dw'][1],
                   p['b_ffn_dw'][1], wb['ffn_down'][1], s_fconv[1] if stateful else None, min(T, tiles.tt),
                   final=(p['norm_final'], mod_final[:, None, :]))
    new_fconv.append(cst)
    return (y3[:, :T], new_ret[None], new_gdn[None], new_gconv[None], jnp.stack(new_fconv))


def kernel(x_prompt, x_sample, state_ret, state_gdn, state_gdn_conv, state_ffn_conv, c_prompt, c_sample, w_ada, b_ada, w_ada_final, b_ada_final, norm_mix, norm_ffn, norm_final, w_ret_in, w_ret_out, w_gdn_in, w_gdn_conv, gdn_a_log, gdn_dt_bias, gdn_norm, w_gdn_out, w_ffn_up, w_ffn_dw, b_ffn_dw, w_ffn_down):
    p = {'norm_mix': norm_mix, 'norm_ffn': norm_ffn, 'norm_final': norm_final,
         'w_gdn_conv': w_gdn_conv, 'gdn_a_log': gdn_a_log, 'gdn_dt_bias': gdn_dt_bias,
         'gdn_norm': gdn_norm, 'w_ffn_dw': w_ffn_dw, 'b_ffn_dw': b_ffn_dw}
    cdim = (2 * GDN_K_HEADS + GDN_V_HEADS) * GDN_D
    vd = GDN_V_HEADS * GDN_D
    wb = {'ret_in': w_ret_in[0].astype(BF16), 'ret_out': w_ret_out[0].astype(BF16),
          'gdn_in': w_gdn_in[0][:, :cdim + vd].astype(BF16), 'gdn_out': w_gdn_out[0].astype(BF16),
          'gdn_tail': w_gdn_in[0][:, cdim + vd:],
          'ffn_up': w_ffn_up.astype(BF16), 'ffn_down': w_ffn_down.astype(BF16)}

    bp, tp = x_prompt.shape[:2]
    bs_ = x_sample.shape[0]
    c_all = jnp.concatenate([c_prompt, c_sample], axis=0)
    mods = _ada_mod(c_all, w_ada, b_ada)
    mod_final = _ada_mod(c_all, w_ada_final[None], b_ada_final[None])[0]

    out_p = _trunk(x_prompt, mods[:, :bp], mod_final[:bp], 0, None, None, None, None, p, wb,
                   _Tiles(bp, tp, 1, min(512, tp)), BF16)
    out_s = _trunk(x_sample, mods[:, bp:], mod_final[bp:], PAST_LEN, state_ret, state_gdn,
                   state_gdn_conv, state_ffn_conv, p, wb,
                   _Tiles(bs_, SUBLANES, min(32, bs_), SUBLANES), F32)
    y_p, ret_p, gdn_p, gconv_p, fconv_p = out_p
    y_s, ret_s, gdn_s, gconv_s, fconv_s = out_s
    return (y_p, y_s, ret_p, gdn_p, gconv_p, fconv_p, ret_s, gdn_s, gconv_s, fconv_s)
```

```python
import functools
import math

import numpy as np
import jax
import jax.numpy as jnp
from jax import lax
from jax.experimental import pallas as pl
from jax.experimental.pallas import tpu as pltpu

F32 = jnp.float32
BF16 = jnp.bfloat16
EPS = 1e-6
ROPE_BASE = 10000.0
PAST_LEN = 16384

RET_HEADS = 4
RET_DK = 256
RET_DV = 512
RET_CHUNK = 128
GDN_K_HEADS = 8
GDN_V_HEADS = 16
GDN_D = 128
GDN_CHUNK = 64
GDN_CONV_W = 4
FFN_CONV_W = 3
SUBLANES = 8
LANES = 128
INV_BASE = 8
GROUP_ROWS = 128
GDN_SEQS_PER_STEP = 2
RET_SEQS_PER_STEP = 2
PROJ_COLS = 512
FFN_COLS = 256
VMEM_LIMIT = 48 * 1024 * 1024

_HIGHEST = lax.Precision.HIGHEST


def _cparams(*sem):
    return pltpu.CompilerParams(dimension_semantics=sem, vmem_limit_bytes=VMEM_LIMIT)


def _silu_of_half(hx):
    return hx + hx * jnp.tanh(hx)


def _silu(x):
    return _silu_of_half(0.5 * x)


def _softplus(x):
    return jnp.maximum(x, 0.0) + jnp.log1p(jnp.exp(-jnp.abs(x)))


def _ada_norm(x, gamma, shift, scale):
    ms = jnp.mean(x * x, axis=-1, keepdims=True)
    xn = x * lax.rsqrt(ms + EPS) * gamma
    return xn * (1.0 + scale) + shift


def _dot(a, b):
    return jnp.dot(a.astype(BF16), b.astype(BF16), preferred_element_type=F32)


def _dot_nt(a, b):
    return lax.dot_general(a.astype(BF16), b.astype(BF16), (((1,), (1,)), ((), ())),
                           preferred_element_type=F32)


def _dot_tn(a, b):
    return lax.dot_general(a.astype(BF16), b.astype(BF16), (((0,), (0,)), ((), ())),
                           preferred_element_type=F32)


def _ada_kernel(c_ref, w_ref, b_ref, o_ref):
    cs = _silu(c_ref[...])
    o_ref[...] = _dot(cs, w_ref[...]) + b_ref[...]


def _ada_mod(c, w, b, tn=1024):
    L, D, N = w.shape
    Mc = c.shape[0]
    return pl.pallas_call(
        _ada_kernel,
        grid=(L, N // tn),
        in_specs=[pl.BlockSpec((Mc, D), lambda l, j: (0, 0)),
                  pl.BlockSpec((None, D, tn), lambda l, j: (l, 0, j)),
                  pl.BlockSpec((None, 1, tn), lambda l, j: (l, 0, j))],
        out_specs=pl.BlockSpec((None, Mc, tn), lambda l, j: (l, 0, j)),
        out_shape=jax.ShapeDtypeStruct((L, Mc, N), F32),
        compiler_params=_cparams("parallel", "parallel"),
        name="ada_mod",
    )(c, w, b.reshape(L, 1, N))


class _Tiles:
    def __init__(self, B, Tp, bt, tt):
        assert bt == 1 or tt == Tp
        self.B, self.Tp, self.bt, self.tt = B, Tp, bt, tt
        self.tps = Tp // tt
        self.n = (B // bt) * self.tps
        self.tm = bt * tt

    def x_spec(self, D):
        tps = self.tps
        return pl.BlockSpec((self.bt, self.tt, D), lambda i: (i // tps, i % tps, 0))

    def seq_spec(self, r, D):
        tps = self.tps
        return pl.BlockSpec((self.bt, r, D), lambda i: (i // tps, 0, 0))

    def rows_spec(self, N):
        return pl.BlockSpec((self.tm, N), lambda i: (i, 0))

    def pos_spec(self, N):
        tps = self.tps
        return pl.BlockSpec((self.tt, N), lambda i: (i % tps, 0))

    def tail_rows_spec(self, r, N):
        return pl.BlockSpec((self.bt, r, N), lambda i: (i, 0, 0))

    def last_tile(self, a):
        return a[self.tps - 1::self.tps]


def _resident(shape):
    return pl.BlockSpec(shape, lambda i: (0,) * len(shape), pipeline_mode=pl.Buffered(1))


def _mod(m3, k, D):
    return m3[:, :, k * D:(k + 1) * D]


def _normed_rows(x_ref, gam_ref, mod_ref, k_shift, h_ref):
    D = x_ref.shape[-1]
    m3 = mod_ref[...]
    h3 = _ada_norm(x_ref[...], gam_ref[...], _mod(m3, k_shift, D), _mod(m3, k_shift + 1, D))
    h_ref[...] = h3.reshape(h_ref.shape).astype(BF16)
    return h3


def _causal_taps(g3, prev_ref, sl, width):
    ext = jnp.concatenate([prev_ref[:, :, sl], g3], axis=1)
    return [pltpu.roll(ext, k, axis=1)[:, SUBLANES:, :] for k in range(width - 1, 0, -1)]


def _start_of_sequence(tiles):
    return (pl.program_id(0) % tiles.tps) == 0


def _ret_proj_kernel(x_ref, gam_ref, mod_ref, w_ref, cos_ref, sin_ref, o_ref, h_ref):
    bt, tt, _ = x_ref.shape
    _normed_rows(x_ref, gam_ref, mod_ref, 0, h_ref)
    cos, sin = cos_ref[...], sin_ref[...]
    qk = RET_HEADS * RET_DK
    half = RET_DK // 2
    for j in range(w_ref.shape[1] // PROJ_COLS):
        sl = slice(j * PROJ_COLS, (j + 1) * PROJ_COLS)
        y = jnp.dot(h_ref[...], w_ref[:, sl], preferred_element_type=F32)
        if sl.start < 2 * qk:
            y3 = y.reshape(bt, tt, PROJ_COLS)
            parts = []
            for c in range(PROJ_COLS // RET_DK):
                x1 = y3[:, :, c * RET_DK:c * RET_DK + half]
                x2 = y3[:, :, c * RET_DK + half:(c + 1) * RET_DK]
                parts += [x1 * cos - x2 * sin, x1 * sin + x2 * cos]
            y = jnp.concatenate(parts, axis=-1).reshape(bt * tt, PROJ_COLS)
            if sl.start >= qk:
                y = y * (RET_DK ** -0.5)
        o_ref[:, sl] = y.astype(o_ref.dtype)


def _ret_proj(tiles, x3, gamma, mod3, w, cos, sin, out_dtype):
    B, Tp, D = x3.shape
    N = w.shape[1]
    return pl.pallas_call(
        _ret_proj_kernel,
        grid=(tiles.n,),
        in_specs=[tiles.x_spec(D), _resident((1, D)), tiles.seq_spec(1, mod3.shape[-1]),
                  _resident((D, N)), tiles.pos_spec(cos.shape[1]), tiles.pos_spec(sin.shape[1])],
        out_specs=tiles.rows_spec(N),
        out_shape=jax.ShapeDtypeStruct((B * Tp, N), out_dtype),
        scratch_shapes=[pltpu.VMEM((tiles.tm, D), BF16)],
        compiler_params=_cparams("parallel"),
        name="ret_proj",
    )(x3, gamma.reshape(1, D), mod3, w, cos, sin)


def _gdn_proj_kernel(*refs, tiles, c_real, has_state):
    it = iter(refs)
    x_ref, gam_ref, mod_ref, w_ref, wtt_ref, wc_ref = (next(it) for _ in range(6))
    if has_state:
        cs_ref = next(it)
    o_ref, ott_ref, cst_ref = next(it), next(it), next(it)
    h_ref, prev_ref = next(it), next(it)
    bt, tt, _ = x_ref.shape
    key = GDN_K_HEADS * GDN_D
    cdim = wc_ref.shape[1]
    keep = GDN_CONV_W - 1

    h3 = _normed_rows(x_ref, gam_ref, mod_ref, 0, h_ref)
    ott_ref[...] = lax.dot_general(wtt_ref[...], h3.reshape(bt * tt, -1), (((1,), (1,)), ((), ())),
                                   precision=_HIGHEST, preferred_element_type=F32)
    if has_state:
        prev_ref[...] = jnp.zeros_like(prev_ref)
        prev_ref[:, SUBLANES - keep:, :] = cs_ref[...]
    else:
        @pl.when(_start_of_sequence(tiles))
        def _():
            prev_ref[...] = jnp.zeros_like(prev_ref)

    for j in range(w_ref.shape[1] // PROJ_COLS):
        sl = slice(j * PROJ_COLS, (j + 1) * PROJ_COLS)
        y = jnp.dot(h_ref[...], w_ref[:, sl], preferred_element_type=F32)
        if sl.start < cdim:
            y3 = y.reshape(bt, tt, PROJ_COLS)
            taps = _causal_taps(y3, prev_ref, sl, GDN_CONV_W)
            wch = 0.5 * wc_ref[:, sl]
            acc = wch[keep:keep + 1, :] * y3
            for i, tap in enumerate(taps):
                acc = acc + wch[i:i + 1, :] * tap
            acc = _silu_of_half(acc)
            if sl.start < 2 * key:
                parts = []
                for c in range(PROJ_COLS // GDN_D):
                    a = acc[:, :, c * GDN_D:(c + 1) * GDN_D]
                    inv = lax.rsqrt(jnp.sum(a * a, axis=-1, keepdims=True) + EPS)
                    parts.append(a * (inv * (GDN_D ** -0.5) if sl.start < key else inv))
                acc = jnp.concatenate(parts, axis=-1)
            cst_ref[:, :, sl] = y3[:, c_real - keep:c_real, :]
            if not has_state:
                prev_ref[:, :, sl] = y3[:, tt - SUBLANES:, :]
            y = acc.reshape(bt * tt, PROJ_COLS)
        o_ref[:, sl] = y.astype(o_ref.dtype)


def _gdn_proj(tiles, x3, gamma, mod3, w, w_tail, w_conv, conv_state, c_real, out_dtype):
    B, Tp, D = x3.shape
    N = w.shape[1]
    nt = w_tail.shape[1]
    cdim = w_conv.shape[1]
    keep = GDN_CONV_W - 1
    has_state = conv_state is not None
    in_specs = [tiles.x_spec(D), _resident((1, D)), tiles.seq_spec(1, mod3.shape[-1]),
                _resident((D, N)), _resident((nt, D)), _resident((GDN_CONV_W, cdim))]
    args = [x3, gamma.reshape(1, D), mod3, w, w_tail.T, w_conv]
    if has_state:
        in_specs.append(tiles.seq_spec(keep, cdim))
        args.append(conv_state)
    proj, ott, cst = pl.pallas_call(
        functools.partial(_gdn_proj_kernel, tiles=tiles, c_real=c_real, has_state=has_state),
        grid=(tiles.n,),
        in_specs=in_specs,
        out_specs=[tiles.rows_spec(N), pl.BlockSpec((nt, tiles.tm), lambda i: (0, i)),
                   tiles.tail_rows_spec(keep, cdim)],
        out_shape=[jax.ShapeDtypeStruct((B * Tp, N), out_dtype),
                   jax.ShapeDtypeStruct((nt, B * Tp), F32),
                   jax.ShapeDtypeStruct((tiles.n * tiles.bt, keep, cdim), F32)],
        scratch_shapes=[pltpu.VMEM((tiles.tm, D), BF16),
                        pltpu.VMEM((tiles.bt, SUBLANES, cdim), F32)],
        compiler_params=_cparams("arbitrary"),
        name="gdn_proj",
    )(*args)
    return proj, ott, tiles.last_tile(cst)


def _ffn_kernel(*refs, tiles, c_real, has_state, final):
    it = iter(refs)
    o_ref, wo_ref, x_ref, gam_ref, mod_ref = (next(it) for _ in range(5))
    wup_ref, wdw_ref, bdw_ref, wd_ref = (next(it) for _ in range(4))
    if has_state:
        cs_ref = next(it)
    if final:
        gamf_ref, modf_ref = next(it), next(it)
    y_ref, cst_ref = next(it), next(it)
    xm_ref, h_ref, act_ref, prev_ref = (next(it) for _ in range(4))
    bt, tt, D = x_ref.shape
    fd = wd_ref.shape[0]
    keep = FFN_CONV_W - 1

    xm_ref[...] = x_ref[...] + _mod(mod_ref[...], 2, D) * _dot(o_ref[...], wo_ref[...]).reshape(bt, tt, D)
    _normed_rows(xm_ref, gam_ref, mod_ref, 3, h_ref)
    if has_state:
        prev_ref[...] = jnp.zeros_like(prev_ref)
        prev_ref[:, SUBLANES - keep:, :] = cs_ref[...]
    else:
        @pl.when(_start_of_sequence(tiles))
        def _():
            prev_ref[...] = jnp.zeros_like(prev_ref)

    for f in range(fd // FFN_COLS):
        sl = slice(f * FFN_COLS, (f + 1) * FFN_COLS)
        h = h_ref[...]
        g3 = jnp.dot(h, wup_ref[:, sl], preferred_element_type=F32).reshape(bt, tt, FFN_COLS)
        val = jnp.dot(h, wup_ref[:, fd + sl.start:fd + sl.stop], preferred_element_type=F32)
        s2, s1 = _causal_taps(g3, prev_ref, sl, FFN_CONV_W)
        wh = 0.5 * wdw_ref[:, sl]
        conv = wh[0:1, :] * s2 + wh[1:2, :] * s1 + wh[2:3, :] * g3 + 0.5 * bdw_ref[:, sl]
        cst_ref[:, :, sl] = g3[:, c_real - keep:c_real, :]
        if not has_state:
            prev_ref[:, :, sl] = g3[:, tt - SUBLANES:, :]
        act_ref[:, sl] = (_silu_of_half(conv).reshape(bt * tt, FFN_COLS) * val).astype(BF16)

    acc = jnp.dot(act_ref[...], wd_ref[...], preferred_element_type=F32).reshape(bt, tt, D)
    xn = xm_ref[...] + _mod(mod_ref[...], 5, D) * acc
    if final:
        mf = modf_ref[...]
        xn = _ada_norm(xn, gamf_ref[...], _mod(mf, 0, D), _mod(mf, 1, D))
    y_ref[...] = xn


def _ffn(tiles, o, w_out, x3, gamma, mod3, w_up, w_dw, b_dw, w_down, conv_state, c_real, final=None):
    B, Tp, D = x3.shape
    Fd = w_down.shape[0]
    K = o.shape[1]
    keep = FFN_CONV_W - 1
    has_state = conv_state is not None
    in_specs = [tiles.rows_spec(K), _resident((K, D)),
                tiles.x_spec(D), _resident((1, D)), tiles.seq_spec(1, mod3.shape[-1]),
                _resident((D, 2 * Fd)), _resident((FFN_CONV_W, Fd)), _resident((1, Fd)),
                _resident((Fd, D))]
    args = [o, w_out, x3, gamma.reshape(1, D), mod3, w_up, w_dw, b_dw.reshape(1, Fd), w_down]
    if has_state:
        in_specs.append(tiles.seq_spec(keep, Fd))
        args.append(conv_state)
    if final is not None:
        in_specs += [_resident((1, D)), tiles.seq_spec(1, final[1].shape[-1])]
        args += [final[0].reshape(1, D), final[1]]
    y3, cst = pl.pallas_call(
        functools.partial(_ffn_kernel, tiles=tiles, c_real=c_real, has_state=has_state,
                          final=final is not None),
        grid=(tiles.n,),
        in_specs=in_specs,
        out_specs=[tiles.x_spec(D), tiles.tail_rows_spec(keep, Fd)],
        out_shape=[jax.ShapeDtypeStruct((B, Tp, D), F32),
                   jax.ShapeDtypeStruct((tiles.n * tiles.bt, keep, Fd), F32)],
        scratch_shapes=[pltpu.VMEM((tiles.bt, tiles.tt, D), F32), pltpu.VMEM((tiles.tm, D), BF16),
                        pltpu.VMEM((tiles.tm, Fd), BF16), pltpu.VMEM((tiles.bt, SUBLANES, Fd), F32)],
        compiler_params=_cparams("arbitrary"),
        name="out_proj_ffn",
    )(*args)
    return y3, tiles.last_tile(cst)


def _ret_core_kernel(*refs, c_real, has_state):
    it = iter(refs)
    q_ref, k_ref, v_ref, g_ref = (next(it) for _ in range(4))
    if has_state:
        s0_ref = next(it)
    o_ref, s_ref = next(it), next(it)
    nb, cp, _ = q_ref.shape

    @pl.when(pl.program_id(1) == 0)
    def _():
        if has_state:
            s_ref[...] = s0_ref[...]
        else:
            s_ref[...] = jnp.zeros_like(s_ref)

    ri = lax.broadcasted_iota(jnp.int32, (cp, cp), 0)
    ci = lax.broadcasted_iota(jnp.int32, (cp, cp), 1)
    causal = ri >= ci
    diff = jnp.where(causal, ri - ci, 0).astype(F32)
    rowi = lax.broadcasted_iota(jnp.int32, (cp, 1), 0)
    row = rowi.astype(F32)
    lgs = [math.log(1.0 - 2.0 ** (-5.0 - h)) for h in range(RET_HEADS)]
    decays = [jnp.where(causal, jnp.exp(diff * lg), 0.0) for lg in lgs]
    k_decays = [jnp.where(rowi < c_real, jnp.exp((c_real - 1.0 - row) * lg), 0.0) for lg in lgs]
    q_decays = [jnp.exp((row + 1.0) * lg) for lg in lgs]

    items = [(i, h) for i in range(nb) for h in range(RET_HEADS)]
    qs = [q_ref[i, :, h * RET_DK:(h + 1) * RET_DK] for i, h in items]
    ks = [k_ref[i, :, h * RET_DK:(h + 1) * RET_DK].astype(F32) for i, h in items]
    vs = [v_ref[i, :, h * RET_DV:(h + 1) * RET_DV] for i, h in items]
    ss = [s_ref[i, h] for i, h in items]
    scores = [_dot_nt(q, k) for q, k in zip(qs, ks)]
    cross = [_dot(q, s) for q, s in zip(qs, ss)]
    upds = [_dot_tn(k * k_decays[h], v) for k, v, (i, h) in zip(ks, vs, items)]
    inner = [_dot(sc * decays[h], v) for sc, v, (i, h) in zip(scores, vs, items)]
    for m, (i, h) in enumerate(items):
        s_ref[i, h] = ss[m] * math.exp(c_real * lgs[h]) + upds[m]
        o = inner[m] + cross[m] * q_decays[h]
        o = o * lax.rsqrt(jnp.mean(o * o, axis=-1, keepdims=True) + EPS)
        gate = _silu(g_ref[i, :, h * RET_DV:(h + 1) * RET_DV].astype(F32))
        o_ref[i, :, h * RET_DV:(h + 1) * RET_DV] = (o * gate).astype(o_ref.dtype)


def _ret_core(proj3, s0, c_real, cp):
    B, Tp, _ = proj3.shape
    qk = RET_HEADS * RET_DK
    vd = RET_HEADS * RET_DV
    has_state = s0 is not None
    nb = min(B, RET_SEQS_PER_STEP)
    in_specs = [pl.BlockSpec((nb, cp, qk), lambda b, n: (b, n, 0)),
                pl.BlockSpec((nb, cp, qk), lambda b, n: (b, n, 1)),
                pl.BlockSpec((nb, cp, vd), lambda b, n: (b, n, 1)),
                pl.BlockSpec((nb, cp, vd), lambda b, n: (b, n, 2))]
    args = [proj3, proj3, proj3, proj3]
    s_spec = pl.BlockSpec((nb, RET_HEADS, RET_DK, RET_DV), lambda b, n: (b, 0, 0, 0))
    if has_state:
        in_specs.append(s_spec)
        args.append(s0)
    return pl.pallas_call(
        functools.partial(_ret_core_kernel, c_real=c_real, has_state=has_state),
        grid=(B // nb, Tp // cp),
        in_specs=in_specs,
        out_specs=[pl.BlockSpec((nb, cp, vd), lambda b, n: (b, n, 0)), s_spec],
        out_shape=[jax.ShapeDtypeStruct((B, Tp, vd), proj3.dtype),
                   jax.ShapeDtypeStruct((B, RET_HEADS, RET_DK, RET_DV), F32)],
        compiler_params=_cparams("parallel", "arbitrary"),
        name="ret_core",
    )(*args)


def _block_inverse_many(ls, ri, ci, bs):
    eye = (ri == ci).astype(F32)
    base = (ri // INV_BASE) == (ci // INV_BASE)
    pws = [-jnp.where(base, l, 0.0) for l in ls]
    ps = [eye + m for m in pws]
    span = 2
    while span < INV_BASE:
        pws = [_dot(pw, pw) for pw in pws]
        ps = [p + _dot(p, pw) for p, pw in zip(ps, pws)]
        span *= 2
    size = INV_BASE
    while size < bs:
        off = ((ri // (2 * size)) == (ci // (2 * size))) & ((ri // size) != (ci // size))
        xs = [_dot(jnp.where(off, l, 0.0), p) for l, p in zip(ls, ps)]
        ps = [p - _dot(p, x) for p, x in zip(ps, xs)]
        size *= 2
    return ps


def _gdn_core_kernel(*refs, c_real, has_state):
    it = iter(refs)
    qkv_ref, z_ref, gt_ref, al_ref, dt_ref, nw_ref = (next(it) for _ in range(6))
    if has_state:
        s0_ref = next(it)
    o_ref, s_ref = next(it), next(it)
    bb, cp, _ = qkv_ref.shape
    bs = cp
    hg = GROUP_ROWS // bs
    ng = GDN_V_HEADS // hg
    rep = GDN_V_HEADS // GDN_K_HEADS
    key = GDN_K_HEADS * GDN_D
    n_items = bb * ng
    assert n_items % SUBLANES == 0

    @pl.when(pl.program_id(1) == 0)
    def _():
        if has_state:
            s_ref[...] = s0_ref[...]
        else:
            s_ref[...] = jnp.zeros_like(s_ref)

    ri = lax.broadcasted_iota(jnp.int32, (GROUP_ROWS, GROUP_ROWS), 0)
    ci = lax.broadcasted_iota(jnp.int32, (GROUP_ROWS, GROUP_ROWS), 1)
    same = (ri // bs) == (ci // bs)
    incl = same & (ri >= ci)
    strict = same & (ri > ci)
    eye = (ri == ci).astype(F32)

    live = (lax.broadcasted_iota(jnp.int32, (1, GROUP_ROWS), 1) % bs) < c_real
    beta_rows = jnp.where(live, jax.nn.sigmoid(gt_ref[0]), 0.0)
    g_rows = jnp.where(live, -jnp.exp(al_ref[...]) * _softplus(gt_ref[1] + dt_ref[...]), 0.0)
    gsum_rows = jnp.dot(g_rows, (same & (ri <= ci)).astype(F32), precision=_HIGHEST,
                        preferred_element_type=F32)
    cols = jnp.concatenate([beta_rows, gsum_rows,
                            jnp.zeros((GROUP_ROWS - 2 * n_items, GROUP_ROWS), F32)], axis=0).T

    items = [(i, j) for i in range(bb) for j in range(ng)]

    def stack(i, j, col_of_head):
        return jnp.concatenate(
            [qkv_ref[i, :, col_of_head(j * hg + hh):col_of_head(j * hg + hh) + GDN_D].astype(F32)
             for hh in range(hg)], axis=0)

    qxs = [stack(i, j, lambda h: (h // rep) * GDN_D) for i, j in items]
    kxs = [stack(i, j, lambda h: key + (h // rep) * GDN_D) for i, j in items]
    vxs = [stack(i, j, lambda h: 2 * key + h * GDN_D) for i, j in items]
    bcs = [jnp.broadcast_to(cols[:, m:m + 1], (GROUP_ROWS, GROUP_ROWS)) for m in range(n_items)]
    gcs = [jnp.broadcast_to(cols[:, n_items + m:n_items + m + 1], (GROUP_ROWS, GROUP_ROWS))
           for m in range(n_items)]
    kbs = [kx * bc for kx, bc in zip(kxs, bcs)]
    kks = [_dot_nt(kb, kx) for kb, kx in zip(kbs, kxs)]
    qks = [_dot_nt(qx, kx) for qx, kx in zip(qxs, kxs)]
    decays = [jnp.exp(jnp.where(incl, gc - gsum_rows[m:m + 1, :], -jnp.inf))
              for m, gc in enumerate(gcs)]
    ls = [jnp.where(strict, kk * d, 0.0) for kk, d in zip(kks, decays)]
    attns = [qk * d for qk, d in zip(qks, decays)]
    ps = _block_inverse_many(ls, ri, ci, bs)
    egs = [jnp.exp(gc) for gc in gcs]
    rhss = [jnp.concatenate([vx * bc, kb * eg], axis=-1) for vx, bc, kb, eg in zip(vxs, bcs, kbs, egs)]
    sols = [rhs + _dot(p - eye, rhs) for rhs, p in zip(rhss, ps)]
    qes = [qx * eg for qx, eg in zip(qxs, egs)]

    heads = [(m, i, j * hg + hh, slice(hh * bs, (hh + 1) * bs))
             for m, (i, j) in enumerate(items) for hh in range(hg)]
    wqs = [_dot(jnp.concatenate([sols[m][rs, GDN_D:], qes[m][rs, :]], axis=0), s_ref[i, h])
           for m, i, h, rs in heads]
    v_news, qss = [], []
    for m in range(n_items):
        part = wqs[m * hg:(m + 1) * hg]
        v_news.append(sols[m][:, :GDN_D] - jnp.concatenate([r[:bs] for r in part], axis=0))
        qss.append(jnp.concatenate([r[bs:] for r in part], axis=0))
    outs = [qs + _dot(attn, vn) for qs, attn, vn in zip(qss, attns, v_news)]

    def last_rows(gc):
        return [gc[hh * bs + c_real - 1:hh * bs + c_real, :] for hh in range(hg)]
    kds = [kx * jnp.exp(jnp.concatenate([jnp.broadcast_to(r, (bs, GROUP_ROWS)) for r in last_rows(gc)],
                                        axis=0) - gc)
           for kx, gc in zip(kxs, gcs)]
    upds = [_dot_tn(kds[m][rs, :], v_news[m][rs, :]) for m, i, h, rs in heads]
    for (m, i, h, rs), upd in zip(heads, upds):
        g_last = gcs[m][rs.start + c_real - 1:rs.start + c_real, :]
        s_ref[i, h] = s_ref[i, h] * jnp.exp(g_last) + upd

    nw = nw_ref[...]
    for m, i, h, rs in heads:
        o = outs[m][rs, :]
        o = o * lax.rsqrt(jnp.mean(o * o, axis=-1, keepdims=True) + EPS) * nw
        gate = _silu(z_ref[i, :, h * GDN_D:(h + 1) * GDN_D].astype(F32))
        o_ref[i, :, h * GDN_D:(h + 1) * GDN_D] = (o * gate).astype(o_ref.dtype)


def _gdn_core(proj3, gates, a_log, dt_bias, norm_w, s0, c_real, cp, bb):
    B, Tp, _ = proj3.shape
    key = GDN_K_HEADS * GDN_D
    vd = GDN_V_HEADS * GDN_D
    cdim = 2 * key + vd
    nc = Tp // cp
    hg = GROUP_ROWS // cp
    ng = GDN_V_HEADS // hg
    has_state = s0 is not None
    lanes = lambda a: jnp.tile(jnp.repeat(a.reshape(ng, hg), cp, axis=1), (bb, 1))
    full = lambda shape: pl.BlockSpec(shape, lambda b, n: (0,) * len(shape))
    in_specs = [pl.BlockSpec((bb, cp, cdim), lambda b, n: (b, n, 0)),
                pl.BlockSpec((bb, cp, vd), lambda b, n: (b, n, cdim // vd)),
                pl.BlockSpec((None, 2, bb * ng, GROUP_ROWS), lambda b, n: (b * nc + n, 0, 0, 0)),
                full((bb * ng, GROUP_ROWS)), full((bb * ng, GROUP_ROWS)), full((1, GDN_D))]
    args = [proj3, proj3, gates, lanes(a_log), lanes(dt_bias), norm_w.reshape(1, GDN_D)]
    s_spec = pl.BlockSpec((bb, GDN_V_HEADS, GDN_D, GDN_D), lambda b, n: (b, 0, 0, 0))
    if has_state:
        in_specs.append(s_spec)
        args.append(s0)
    return pl.pallas_call(
        functools.partial(_gdn_core_kernel, c_real=c_real, has_state=has_state),
        grid=(B // bb, nc),
        in_specs=in_specs,
        out_specs=[pl.BlockSpec((bb, cp, vd), lambda b, n: (b, n, 0)), s_spec],
        out_shape=[jax.ShapeDtypeStruct((B, Tp, vd), proj3.dtype),
                   jax.ShapeDtypeStruct((B, GDN_V_HEADS, GDN_D, GDN_D), F32)],
        compiler_params=_cparams("parallel", "arbitrary"),
        name="gdn_core",
    )(*args)


def _rope_tables(pos0, t_real, t_pad):
    half = RET_DK // 2
    inv_freq = ROPE_BASE ** (-np.arange(half, dtype=np.float64) / half)
    pos = pos0 + np.arange(t_pad, dtype=np.float64)
    ang = pos[:, None] * inv_freq[None, :]
    live = (np.arange(t_pad) < t_real)[:, None]
    return (jnp.asarray(np.where(live, np.cos(ang), 0.0), F32),
            jnp.asarray(np.where(live, np.sin(ang), 0.0), F32))


def _trunk(x, mods, mod_final, pos0, s_ret, s_gdn, s_gconv, s_fconv, p, wb, tiles, act_dtype):
    B, T, D = x.shape
    Tp = tiles.Tp
    x3 = x if T == Tp else jnp.pad(x, ((0, 0), (0, Tp - T), (0, 0)))
    M = B * Tp
    stateful = s_ret is not None
    new_fconv = []

    mod3 = mods[0][:, None, :]
    cos, sin = _rope_tables(pos0, T, Tp)
    proj = _ret_proj(tiles, x3, p['norm_mix'][0], mod3, wb['ret_in'], cos, sin, act_dtype)
    cp = min(RET_CHUNK, Tp)
    o3, new_ret = _ret_core(proj.reshape(B, Tp, -1), s_ret[0] if stateful else None, min(T, cp), cp)
    x3, cst = _ffn(tiles, o3.reshape(M, -1), wb['ret_out'], x3, p['norm_ffn'][0], mod3,
                   wb['ffn_up'][0], p['w_ffn_dw'][0], p['b_ffn_dw'][0], wb['ffn_down'][0],
                   s_fconv[0] if stateful else None, min(T, tiles.tt))
    new_fconv.append(cst)

    mod3 = mods[1][:, None, :]
    cp = min(GDN_CHUNK, Tp)
    nc = Tp // cp
    ng = GDN_V_HEADS * cp // GROUP_ROWS
    bb = min(B, max(SUBLANES // ng, GDN_SEQS_PER_STEP))
    proj, bat, new_gconv = _gdn_proj(tiles, x3, p['norm_mix'][1], mod3, wb['gdn_in'], wb['gdn_tail'],
                                     p['w_gdn_conv'][0], s_gconv[0] if stateful else None,
                                     min(T, tiles.tt), act_dtype)
    gates = bat.reshape(2, ng, GROUP_ROWS // cp, B // bb, bb, nc, cp)
    gates = gates.transpose(3, 5, 0, 4, 1, 2, 6).reshape(B // bb * nc, 2, bb * ng, GROUP_ROWS)
    o3, new_gdn = _gdn_core(proj.reshape(B, Tp, -1), gates, p['gdn_a_log'][0], p['gdn_dt_bias'][0],
                            p['gdn_norm'][0], s_gdn[0] if stateful else None, min(T, cp), cp, bb)
    y3, cst = _ffn(tiles, o3.reshape(M, -1), wb['gdn_out'], x3, p['norm_ffn'][1], mod3,
                   wb['ffn_up'][1], p['w_ffn_dw'][1], p['b_ffn_dw'][1], wb['ffn_down'][1],
                   s_fconv[1] if stateful else None, min(T, tiles.tt),
                   final=(p['norm_final'], mod_final[:, None, :]))
    new_fconv.append(cst)
    return (y3[:, :T], new_ret[None], new_gdn[None], new_gconv[None], jnp.stack(new_fconv))


def kernel(x_prompt, x_sample, state_ret, state_gdn, state_gdn_conv, state_ffn_conv, c_prompt, c_sample, w_ada, b_ada, w_ada_final, b_ada_final, norm_mix, norm_ffn, norm_final, w_ret_in, w_ret_out, w_gdn_in, w_gdn_conv, gdn_a_log, gdn_dt_bias, gdn_norm, w_gdn_out, w_ffn_up, w_ffn_dw, b_ffn_dw, w_ffn_down):
    p = {'norm_mix': norm_mix, 'norm_ffn': norm_ffn, 'norm_final': norm_final,
         'w_gdn_conv': w_gdn_conv, 'gdn_a_log': gdn_a_log, 'gdn_dt_bias': gdn_dt_bias,
         'gdn_norm': gdn_norm, 'w_ffn_dw': w_ffn_dw, 'b_ffn_dw': b_ffn_dw}
    cdim = (2 * GDN_K_HEADS + GDN_V_HEADS) * GDN_D
    vd = GDN_V_HEADS * GDN_D
    wb = {'ret_in': w_ret_in[0].astype(BF16), 'ret_out': w_ret_out[0].astype(BF16),
          'gdn_in': w_gdn_in[0][:, :cdim + vd].astype(BF16), 'gdn_out': w_gdn_out[0].astype(BF16),
          'gdn_tail': w_gdn_in[0][:, cdim + vd:],
          'ffn_up': w_ffn_up.astype(BF16), 'ffn_down': w_ffn_down.astype(BF16)}

    bp, tp = x_prompt.shape[:2]
    bs_ = x_sample.shape[0]
    c_all = jnp.concatenate([c_prompt, c_sample], axis=0)
    mods = _ada_mod(c_all, w_ada, b_ada)
    mod_final = _ada_mod(c_all, w_ada_final[None], b_ada_final[None])[0]

    out_p = _trunk(x_prompt, mods[:, :bp], mod_final[:bp], 0, None, None, None, None, p, wb,
                   _Tiles(bp, tp, 1, min(512, tp)), BF16)
    out_s = _trunk(x_sample, mods[:, bp:], mod_final[bp:], PAST_LEN, state_ret, state_gdn,
                   state_gdn_conv, state_ffn_conv, p, wb,
                   _Tiles(bs_, SUBLANES, min(32, bs_), SUBLANES), F32)
    y_p, ret_p, gdn_p, gconv_p, fconv_p = out_p
    y_s, ret_s, gdn_s, gconv_s, fconv_s = out_s
    return (y_p, y_s, ret_p, gdn_p, gconv_p, fconv_p, ret_s, gdn_s, gconv_s, fconv_s)
```

```python
import functools
import math

import numpy as np
import jax
import jax.numpy as jnp
from jax import lax
from jax.experimental import pallas as pl
from jax.experimental.pallas import tpu as pltpu

F32 = jnp.float32
BF16 = jnp.bfloat16
EPS = 1e-6
ROPE_BASE = 10000.0
PAST_LEN = 16384

RET_HEADS = 4
RET_DK = 256
RET_DV = 512
RET_CHUNK = 128
GDN_K_HEADS = 8
GDN_V_HEADS = 16
GDN_D = 128
GDN_CHUNK = 64
GDN_CONV_W = 4
FFN_CONV_W = 3
SUBLANES = 8
LANES = 128
INV_BASE = 8
GROUP_ROWS = 128
GDN_SEQS_PER_STEP = 2
RET_SEQS_PER_STEP = 2
PROJ_COLS = 512
FFN_COLS = 256
VMEM_LIMIT = 48 * 1024 * 1024

_HIGHEST = lax.Precision.HIGHEST


def _cparams(*sem):
    return pltpu.CompilerParams(dimension_semantics=sem, vmem_limit_bytes=VMEM_LIMIT)


def _silu_of_half(hx):
    return hx + hx * jnp.tanh(hx)


def _silu(x):
    return _silu_of_half(0.5 * x)


def _softplus(x):
    return jnp.maximum(x, 0.0) + jnp.log1p(jnp.exp(-jnp.abs(x)))


def _ada_norm(x, gamma, shift, scale):
    ms = jnp.mean(x * x, axis=-1, keepdims=True)
    xn = x * lax.rsqrt(ms + EPS) * gamma
    return xn * (1.0 + scale) + shift


def _dot(a, b):
    return jnp.dot(a.astype(BF16), b.astype(BF16), preferred_element_type=F32)


def _dot_nt(a, b):
    return lax.dot_general(a.astype(BF16), b.astype(BF16), (((1,), (1,)), ((), ())),
                           preferred_element_type=F32)


def _dot_tn(a, b):
    return lax.dot_general(a.astype(BF16), b.astype(BF16), (((0,), (0,)), ((), ())),
                           preferred_element_type=F32)


def _ada_kernel(c_ref, w_ref, b_ref, o_ref):
    cs = _silu(c_ref[...])
    o_ref[...] = _dot(cs, w_ref[...]) + b_ref[...]


def _ada_mod(c, w, b, tn=1024):
    L, D, N = w.shape
    Mc = c.shape[0]
    return pl.pallas_call(
        _ada_kernel,
        grid=(L, N // tn),
        in_specs=[pl.BlockSpec((Mc, D), lambda l, j: (0, 0)),
                  pl.BlockSpec((None, D, tn), lambda l, j: (l, 0, j)),
                  pl.BlockSpec((None, 1, tn), lambda l, j: (l, 0, j))],
        out_specs=pl.BlockSpec((None, Mc, tn), lambda l, j: (l, 0, j)),
        out_shape=jax.ShapeDtypeStruct((L, Mc, N), F32),
        compiler_params=_cparams("parallel", "parallel"),
        name="ada_mod",
    )(c, w, b.reshape(L, 1, N))


class _Tiles:
    def __init__(self, B, Tp, bt, tt):
        assert bt == 1 or tt == Tp
        self.B, self.Tp, self.bt, self.tt = B, Tp, bt, tt
        self.tps = Tp // tt
        self.n = (B // bt) * self.tps
        self.tm = bt * tt

    def x_spec(self, D):
        tps = self.tps
        return pl.BlockSpec((self.bt, self.tt, D), lambda i: (i // tps, i % tps, 0))

    def seq_spec(self, r, D):
        tps = self.tps
        return pl.BlockSpec((self.bt, r, D), lambda i: (i // tps, 0, 0))

    def rows_spec(self, N):
        return pl.BlockSpec((self.tm, N), lambda i: (i, 0))

    def pos_spec(self, N):
        tps = self.tps
        return pl.BlockSpec((self.tt, N), lambda i: (i % tps, 0))

    def tail_rows_spec(self, r, N):
        return pl.BlockSpec((self.bt, r, N), lambda i: (i, 0, 0))

    def last_tile(self, a):
        return a[self.tps - 1::self.tps]


def _resident(shape):
    return pl.BlockSpec(shape, lambda i: (0,) * len(shape), pipeline_mode=pl.Buffered(1))


def _mod(m3, k, D):
    return m3[:, :, k * D:(k + 1) * D]


def _normed_rows(x_ref, gam_ref, mod_ref, k_shift, h_ref):
    D = x_ref.shape[-1]
    m3 = mod_ref[...]
    h3 = _ada_norm(x_ref[...], gam_ref[...], _mod(m3, k_shift, D), _mod(m3, k_shift + 1, D))
    h_ref[...] = h3.reshape(h_ref.shape).astype(BF16)
    return h3


def _causal_taps(g3, prev_ref, sl, width):
    ext = jnp.concatenate([prev_ref[:, :, sl], g3], axis=1)
    return [pltpu.roll(ext, k, axis=1)[:, SUBLANES:, :] for k in range(width - 1, 0, -1)]


def _start_of_sequence(tiles):
    return (pl.program_id(0) % tiles.tps) == 0


def _ret_proj_kernel(x_ref, gam_ref, mod_ref, w_ref, cos_ref, sin_ref, o_ref, h_ref):
    bt, tt, _ = x_ref.shape
    _normed_rows(x_ref, gam_ref, mod_ref, 0, h_ref)
    cos, sin = cos_ref[...], sin_ref[...]
    qk = RET_HEADS * RET_DK
    half = RET_DK // 2
    for j in range(w_ref.shape[1] // PROJ_COLS):
        sl = slice(j * PROJ_COLS, (j + 1) * PROJ_COLS)
        y = jnp.dot(h_ref[...], w_ref[:, sl], preferred_element_type=F32)
        if sl.start < 2 * qk:
            y3 = y.reshape(bt, tt, PROJ_COLS)
            parts = []
            for c in range(PROJ_COLS // RET_DK):
                x1 = y3[:, :, c * RET_DK:c * RET_DK + half]
                x2 = y3[:, :, c * RET_DK + half:(c + 1) * RET_DK]
                parts += [x1 * cos - x2 * sin, x1 * sin + x2 * cos]
            y = jnp.concatenate(parts, axis=-1).reshape(bt * tt, PROJ_COLS)
            if sl.start >= qk:
                y = y * (RET_DK ** -0.5)
        o_ref[:, sl] = y.astype(o_ref.dtype)


def _ret_proj(tiles, x3, gamma, mod3, w, cos, sin, out_dtype):
    B, Tp, D = x3.shape
    N = w.shape[1]
    return pl.pallas_call(
        _ret_proj_kernel,
        grid=(tiles.n,),
        in_specs=[tiles.x_spec(D), _resident((1, D)), tiles.seq_spec(1, mod3.shape[-1]),
                  _resident((D, N)), tiles.pos_spec(cos.shape[1]), tiles.pos_spec(sin.shape[1])],
        out_specs=tiles.rows_spec(N),
        out_shape=jax.ShapeDtypeStruct((B * Tp, N), out_dtype),
        scratch_shapes=[pltpu.VMEM((tiles.tm, D), BF16)],
        compiler_params=_cparams("parallel"),
        name="ret_proj",
    )(x3, gamma.reshape(1, D), mod3, w, cos, sin)


def _gdn_proj_kernel(*refs, tiles, c_real, has_state):
    it = iter(refs)
    x_ref, gam_ref, mod_ref, w_ref, wtt_ref, wc_ref = (next(it) for _ in range(6))
    if has_state:
        cs_ref = next(it)
    o_ref, ott_ref, cst_ref = next(it), next(it), next(it)
    h_ref, prev_ref = next(it), next(it)
    bt, tt, _ = x_ref.shape
    key = GDN_K_HEADS * GDN_D
    cdim = wc_ref.shape[1]
    keep = GDN_CONV_W - 1

    h3 = _normed_rows(x_ref, gam_ref, mod_ref, 0, h_ref)
    ott_ref[...] = lax.dot_general(wtt_ref[...], h3.reshape(bt * tt, -1), (((1,), (1,)), ((), ())),
                                   precision=_HIGHEST, preferred_element_type=F32)
    if has_state:
        prev_ref[...] = jnp.zeros_like(prev_ref)
        prev_ref[:, SUBLANES - keep:, :] = cs_ref[...]
    else:
        @pl.when(_start_of_sequence(tiles))
        def _():
            prev_ref[:, 0:SUBLANES, :] = jnp.zeros((prev_ref.shape[0], SUBLANES, LANES), F32)

    def head_norm(a, col):
        if col >= 2 * key:
            return a
        inv = lax.rsqrt(jnp.sum(a * a, axis=-1, keepdims=True) + EPS)
        return a * (inv * (GDN_D ** -0.5) if col < key else inv)

    def conv_by_phase(col):
        slab = col // LANES
        n8 = tt // SUBLANES
        cur =[prev_ref[slab, pl.ds(SUBLANES + b, n8, stride=SUBLANES), :] for b in range(SUBLANES)]
        old = {b: prev_ref[slab, pl.ds(b, n8, stride=SUBLANES), :]
               for b in range(SUBLANES - keep, SUBLANES)}
        wch = 0.5 * wc_ref[:, col:col + LANES]
        outs = []
        for b in range(SUBLANES):
            acc = wch[keep:keep + 1, :] * cur[b]
            for k in range(1, keep + 1):
                src = cur[b - k] if b >= k else old[b - k + SUBLANES]
                acc = acc + wch[keep - k:keep - k + 1, :] * src
            outs.append(head_norm(_silu_of_half(acc), col))
        cst_ref[0, :, col:col + LANES] = prev_ref[slab, SUBLANES + c_real - keep:SUBLANES + c_real, :]
        prev_ref[slab, 0:SUBLANES, :] = prev_ref[slab, tt:tt + SUBLANES, :]
        for b in range(SUBLANES):
            prev_ref[slab, pl.ds(SUBLANES + b, n8, stride=SUBLANES), :] = outs[b]
        o_ref[:, col:col + LANES] = prev_ref[slab, SUBLANES:, :].astype(o_ref.dtype)

    for j in range(o_ref.shape[1] // PROJ_COLS):
        sl = slice(j * PROJ_COLS, (j + 1) * PROJ_COLS)
        y = jnp.dot(h_ref[...], w_ref[:, sl], preferred_element_type=F32)
        if sl.start < cdim and not has_state:
            for c in range(0, PROJ_COLS, LANES):
                prev_ref[(sl.start + c) // LANES, SUBLANES:, :] = y[:, c:c + LANES]
            for c in range(0, PROJ_COLS, LANES):
                conv_by_phase(sl.start + c)
            continue
        if sl.start < cdim:
            y3 = y.reshape(bt, tt, PROJ_COLS)
            taps = _causal_taps(y3, prev_ref, sl, GDN_CONV_W)
            wch = 0.5 * wc_ref[:, sl]
            acc = wch[keep:keep + 1, :] * y3
            for i, tap in enumerate(taps):
                acc = acc + wch[i:i + 1, :] * tap
            acc = _silu_of_half(acc)
            acc = jnp.concatenate([head_norm(acc[:, :, c:c + GDN_D], sl.start + c)
                                   for c in range(0, PROJ_COLS, GDN_D)], axis=-1)
            cst_ref[:, :, sl] = y3[:, c_real - keep:c_real, :]
            y = acc.reshape(bt * tt, PROJ_COLS)
        o_ref[:, sl] = y.astype(o_ref.dtype)


def _gdn_proj(tiles, x3, gamma, mod3, w, w_tail, w_conv, conv_state, c_real, out_dtype):
    B, Tp, D = x3.shape
    nt = w_tail.shape[1]
    cdim = w_conv.shape[1]
    N = cdim + GDN_V_HEADS * GDN_D
    keep = GDN_CONV_W - 1
    has_state = conv_state is not None
    in_specs = [tiles.x_spec(D), _resident((1, D)), tiles.seq_spec(1, mod3.shape[-1]),
                _resident(w.shape), _resident((nt, D)), _resident((GDN_CONV_W, cdim))]
    args = [x3, gamma.reshape(1, D), mod3, w, w_tail.T, w_conv]
    if has_state:
        in_specs.append(tiles.seq_spec(keep, cdim))
        args.append(conv_state)
    proj, ott, cst = pl.pallas_call(
        functools.partial(_gdn_proj_kernel, tiles=tiles, c_real=c_real, has_state=has_state),
        grid=(tiles.n,),
        in_specs=in_specs,
        out_specs=[tiles.rows_spec(N), pl.BlockSpec((nt, tiles.tm), lambda i: (0, i)),
                   tiles.tail_rows_spec(keep, cdim)],
        out_shape=[jax.ShapeDtypeStruct((B * Tp, N), out_dtype),
                   jax.ShapeDtypeStruct((nt, B * Tp), F32),
                   jax.ShapeDtypeStruct((tiles.n * tiles.bt, keep, cdim), F32)],
        scratch_shapes=[pltpu.VMEM((tiles.tm, D), BF16),
                        pltpu.VMEM((tiles.bt, SUBLANES, cdim) if has_state
                                   else (cdim // LANES, SUBLANES + tiles.tt, LANES), F32)],
        compiler_params=_cparams("arbitrary"),
        name="gdn_proj",
    )(*args)
    return proj, ott, tiles.last_tile(cst)


def _ffn_kernel(*refs, tiles, c_real, has_state, final):
    it = iter(refs)
    o_ref, wo_ref, x_ref, gam_ref, mod_ref = (next(it) for _ in range(5))
    wup_ref, wdw_ref, bdw_ref, wd_ref = (next(it) for _ in range(4))
    if has_state:
        cs_ref = next(it)
    if final:
        gamf_ref, modf_ref = next(it), next(it)
    y_ref, cst_ref = next(it), next(it)
    xm_ref, h_ref, act_ref, prev_ref = (next(it) for _ in range(4))
    bt, tt, D = x_ref.shape
    fd = wd_ref.shape[0]
    keep = FFN_CONV_W - 1

    xm_ref[...] = x_ref[...] + _mod(mod_ref[...], 0, D) * _dot(o_ref[...], wo_ref[...]).reshape(bt, tt, D)
    _normed_rows(xm_ref, gam_ref, mod_ref, 1, h_ref)
    if has_state:
        prev_ref[...] = jnp.zeros_like(prev_ref)
        prev_ref[:, SUBLANES - keep:, :] = cs_ref[...]
    else:
        @pl.when(_start_of_sequence(tiles))
        def _():
            prev_ref[...] = jnp.zeros_like(prev_ref)

    for f in range(fd // FFN_COLS):
        sl = slice(f * FFN_COLS, (f + 1) * FFN_COLS)
        h = h_ref[...]
        g3 = jnp.dot(h, wup_ref[:, sl], preferred_element_type=F32).reshape(bt, tt, FFN_COLS)
        val = jnp.dot(h, wup_ref[:, fd + sl.start:fd + sl.stop], preferred_element_type=F32)
        s2, s1 = _causal_taps(g3, prev_ref, sl, FFN_CONV_W)
        wh = 0.5 * wdw_ref[:, sl]
        conv = wh[0:1, :] * s2 + wh[1:2, :] * s1 + wh[2:3, :] * g3 + 0.5 * bdw_ref[:, sl]
        cst_ref[:, :, sl] = g3[:, c_real - keep:c_real, :]
        if not has_state:
            prev_ref[:, :, sl] = g3[:, tt - SUBLANES:, :]
        act_ref[:, sl] = (_silu_of_half(conv).reshape(bt * tt, FFN_COLS) * val).astype(BF16)

    acc = jnp.dot(act_ref[...], wd_ref[...], preferred_element_type=F32).reshape(bt, tt, D)
    xn = xm_ref[...] + _mod(mod_ref[...], 3, D) * acc
    if final:
        mf = modf_ref[...]
        xn = _ada_norm(xn, gamf_ref[...], _mod(mf, 0, D), _mod(mf, 1, D))
    y_ref[...] = xn


def _ffn(tiles, o, w_out, x3, gamma, mod3, w_up, w_dw, b_dw, w_down, conv_state, c_real, final=None):
    B, Tp, D = x3.shape
    Fd = w_down.shape[0]
    K = o.shape[1]
    keep = FFN_CONV_W - 1
    has_state = conv_state is not None
    in_specs = [tiles.rows_spec(K), _resident((K, D)),
                tiles.x_spec(D), _resident((1, D)), tiles.seq_spec(1, mod3.shape[-1]),
                _resident((D, 2 * Fd)), _resident((FFN_CONV_W, Fd)), _resident((1, Fd)),
                _resident((Fd, D))]
    args = [o, w_out, x3, gamma.reshape(1, D), mod3, w_up, w_dw, b_dw.reshape(1, Fd), w_down]
    if has_state:
        in_specs.append(tiles.seq_spec(keep, Fd))
        args.append(conv_state)
    if final is not None:
        in_specs += [_resident((1, D)), tiles.seq_spec(1, final[1].shape[-1])]
        args += [final[0].reshape(1, D), final[1]]
    y3, cst = pl.pallas_call(
        functools.partial(_ffn_kernel, tiles=tiles, c_real=c_real, has_state=has_state,
                          final=final is not None),
        grid=(tiles.n,),
        in_specs=in_specs,
        out_specs=[tiles.x_spec(D), tiles.tail_rows_spec(keep, Fd)],
        out_shape=[jax.ShapeDtypeStruct((B, Tp, D), F32),
                   jax.ShapeDtypeStruct((tiles.n * tiles.bt, keep, Fd), F32)],
        scratch_shapes=[pltpu.VMEM((tiles.bt, tiles.tt, D), F32), pltpu.VMEM((tiles.tm, D), BF16),
                        pltpu.VMEM((tiles.tm, Fd), BF16), pltpu.VMEM((tiles.bt, SUBLANES, Fd), F32)],
        compiler_params=_cparams("arbitrary"),
        name="out_proj_ffn",
    )(*args)
    return y3, tiles.last_tile(cst)


def _ret_core_kernel(*refs, c_real, has_state):
    it = iter(refs)
    q_ref, k_ref, v_ref, g_ref = (next(it) for _ in range(4))
    if has_state:
        s0_ref = next(it)
    o_ref, s_ref = next(it), next(it)
    nb, cp, _ = q_ref.shape

    @pl.when(pl.program_id(1) == 0)
    def _():
        if has_state:
            s_ref[...] = s0_ref[...]
        else:
            s_ref[...] = jnp.zeros_like(s_ref)

    ri = lax.broadcasted_iota(jnp.int32, (cp, cp), 0)
    ci = lax.broadcasted_iota(jnp.int32, (cp, cp), 1)
    causal = ri >= ci
    diff = jnp.where(causal, ri - ci, 0).astype(F32)
    rowi = lax.broadcasted_iota(jnp.int32, (cp, 1), 0)
    row = rowi.astype(F32)
    lgs = [math.log(1.0 - 2.0 ** (-5.0 - h)) for h in range(RET_HEADS)]
    decays = [jnp.where(causal, jnp.exp(diff * lg), 0.0) for lg in lgs]
    k_decays = [jnp.where(rowi < c_real, jnp.exp((c_real - 1.0 - row) * lg), 0.0) for lg in lgs]
    q_decays = [jnp.exp((row + 1.0) * lg) for lg in lgs]

    items = [(i, h) for i in range(nb) for h in range(RET_HEADS)]
    qs = [q_ref[i, :, h * RET_DK:(h + 1) * RET_DK] for i, h in items]
    ks = [k_ref[i, :, h * RET_DK:(h + 1) * RET_DK].astype(F32) for i, h in items]
    vs = [v_ref[i, :, h * RET_DV:(h + 1) * RET_DV] for i, h in items]
    ss = [s_ref[i, h] for i, h in items]
    scores = [_dot_nt(q, k) for q, k in zip(qs, ks)]
    cross = [_dot(q, s) for q, s in zip(qs, ss)]
    upds = [_dot_tn(k * k_decays[h], v) for k, v, (i, h) in zip(ks, vs, items)]
    inner = [_dot(sc * decays[h], v) for sc, v, (i, h) in zip(scores, vs, items)]
    for m, (i, h) in enumerate(items):
        s_ref[i, h] = ss[m] * math.exp(c_real * lgs[h]) + upds[m]
        o = inner[m] + cross[m] * q_decays[h]
        o = o * lax.rsqrt(jnp.mean(o * o, axis=-1, keepdims=True) + EPS)
        gate = _silu(g_ref[i, :, h * RET_DV:(h + 1) * RET_DV].astype(F32))
        o_ref[i, :, h * RET_DV:(h + 1) * RET_DV] = (o * gate).astype(o_ref.dtype)


def _ret_core(proj3, s0, c_real, cp):
    B, Tp, _ = proj3.shape
    qk = RET_HEADS * RET_DK
    vd = RET_HEADS * RET_DV
    has_state = s0 is not None
    nb = min(B, RET_SEQS_PER_STEP)
    in_specs = [pl.BlockSpec((nb, cp, qk), lambda b, n: (b, n, 0)),
                pl.BlockSpec((nb, cp, qk), lambda b, n: (b, n, 1)),
                pl.BlockSpec((nb, cp, vd), lambda b, n: (b, n, 1)),
                pl.BlockSpec((nb, cp, vd), lambda b, n: (b, n, 2))]
    args = [proj3, proj3, proj3, proj3]
    s_spec = pl.BlockSpec((nb, RET_HEADS, RET_DK, RET_DV), lambda b, n: (b, 0, 0, 0))
    if has_state:
        in_specs.append(s_spec)
        args.append(s0)
    return pl.pallas_call(
        functools.partial(_ret_core_kernel, c_real=c_real, has_state=has_state),
        grid=(B // nb, Tp // cp),
        in_specs=in_specs,
        out_specs=[pl.BlockSpec((nb, cp, vd), lambda b, n: (b, n, 0)), s_spec],
        out_shape=[jax.ShapeDtypeStruct((B, Tp, vd), proj3.dtype),
                   jax.ShapeDtypeStruct((B, RET_HEADS, RET_DK, RET_DV), F32)],
        compiler_params=_cparams("parallel", "arbitrary"),
        name="ret_core",
    )(*args)


def _block_inverse_many(ls, ri, ci, bs):
    eye = (ri == ci).astype(F32)
    base = (ri // INV_BASE) == (ci // INV_BASE)
    pws = [-jnp.where(base, l, 0.0) for l in ls]
    ps = [eye + m for m in pws]
    span = 2
    while span < INV_BASE:
        pws = [_dot(pw, pw) for pw in pws]
        ps = [p + _dot(p, pw) for p, pw in zip(ps, pws)]
        span *= 2
    size = INV_BASE
    while size < bs:
        off = ((ri // (2 * size)) == (ci // (2 * size))) & ((ri // size) != (ci // size))
        xs = [_dot(jnp.where(off, l, 0.0), p) for l, p in zip(ls, ps)]
        ps = [p - _dot(p, x) for p, x in zip(ps, xs)]
        size *= 2
    return ps


def _gdn_core_kernel(*refs, c_real, has_state):
    it = iter(refs)
    qkv_ref, z_ref, gt_ref, al_ref, dt_ref, nw_ref = (next(it) for _ in range(6))
    if has_state:
        s0_ref = next(it)
    o_ref, s_ref = next(it), next(it)
    bb, cp, _ = qkv_ref.shape
    bs = cp
    hg = GROUP_ROWS // bs
    ng = GDN_V_HEADS // hg
    rep = GDN_V_HEADS // GDN_K_HEADS
    key = GDN_K_HEADS * GDN_D
    n_items = bb * ng
    assert n_items % SUBLANES == 0

    @pl.when(pl.program_id(1) == 0)
    def _():
        if has_state:
            s_ref[...] = s0_ref[...]
        else:
            s_ref[...] = jnp.zeros_like(s_ref)

    ri = lax.broadcasted_iota(jnp.int32, (GROUP_ROWS, GROUP_ROWS), 0)
    ci = lax.broadcasted_iota(jnp.int32, (GROUP_ROWS, GROUP_ROWS), 1)
    same = (ri // bs) == (ci // bs)
    incl = same & (ri >= ci)
    strict = same & (ri > ci)
    eye = (ri == ci).astype(F32)

    live = (lax.broadcasted_iota(jnp.int32, (1, GROUP_ROWS), 1) % bs) < c_real
    beta_rows = jnp.where(live, jax.nn.sigmoid(gt_ref[0]), 0.0)
    g_rows = jnp.where(live, -jnp.exp(al_ref[...]) * _softplus(gt_ref[1] + dt_ref[...]), 0.0)
    gsum_rows = jnp.dot(g_rows, (same & (ri <= ci)).astype(F32), precision=_HIGHEST,
                        preferred_element_type=F32)
    cols = jnp.concatenate([beta_rows, gsum_rows,
                            jnp.zeros((GROUP_ROWS - 2 * n_items, GROUP_ROWS), F32)], axis=0).T

    items = [(i, j) for i in range(bb) for j in range(ng)]

    def stack(i, j, col_of_head):
        return jnp.concatenate(
            [qkv_ref[i, :, col_of_head(j * hg + hh):col_of_head(j * hg + hh) + GDN_D].astype(F32)
             for hh in range(hg)], axis=0)

    qxs = [stack(i, j, lambda h: (h // rep) * GDN_D) for i, j in items]
    kxs = [stack(i, j, lambda h: key + (h // rep) * GDN_D) for i, j in items]
    vxs = [stack(i, j, lambda h: 2 * key + h * GDN_D) for i, j in items]
    bcs = [jnp.broadcast_to(cols[:, m:m + 1], (GROUP_ROWS, GROUP_ROWS)) for m in range(n_items)]
    gcs = [jnp.broadcast_to(cols[:, n_items + m:n_items + m + 1], (GROUP_ROWS, GROUP_ROWS))
           for m in range(n_items)]
    kbs = [kx * bc for kx, bc in zip(kxs, bcs)]
    kks = [_dot_nt(kb, kx) for kb, kx in zip(kbs, kxs)]
    qks = [_dot_nt(qx, kx) for qx, kx in zip(qxs, kxs)]
    decays = [jnp.exp(jnp.where(incl, gc - gsum_rows[m:m + 1, :], -jnp.inf))
              for m, gc in enumerate(gcs)]
    ls = [jnp.where(strict, kk * d, 0.0) for kk, d in zip(kks, decays)]
    attns = [qk * d for qk, d in zip(qks, decays)]
    ps = _block_inverse_many(ls, ri, ci, bs)
    egs = [jnp.exp(gc) for gc in gcs]
    rhss = [jnp.concatenate([vx * bc, kb * eg], axis=-1) for vx, bc, kb, eg in zip(vxs, bcs, kbs, egs)]
    sols = [rhs + _dot(p - eye, rhs) for rhs, p in zip(rhss, ps)]
    qes = [qx * eg for qx, eg in zip(qxs, egs)]

    heads = [(m, i, j * hg + hh, slice(hh * bs, (hh + 1) * bs))
             for m, (i, j) in enumerate(items) for hh in range(hg)]
    wqs = [_dot(jnp.concatenate([sols[m][rs, GDN_D:], qes[m][rs, :]], axis=0), s_ref[i, h])
           for m, i, h, rs in heads]
    v_news, qss = [], []
    for m in range(n_items):
        part = wqs[m * hg:(m + 1) * hg]
        v_news.append(sols[m][:, :GDN_D] - jnp.concatenate([r[:bs] for r in part], axis=0))
        qss.append(jnp.concatenate([r[bs:] for r in part], axis=0))
    outs = [qs + _dot(attn, vn) for qs, attn, vn in zip(qss, attns, v_news)]

    def last_rows(gc):
        return [gc[hh * bs + c_real - 1:hh * bs + c_real, :] for hh in range(hg)]
    kds = [kx * jnp.exp(jnp.concatenate([jnp.broadcast_to(r, (bs, GROUP_ROWS)) for r in last_rows(gc)],
                                        axis=0) - gc)
           for kx, gc in zip(kxs, gcs)]
    upds = [_dot_tn(kds[m][rs, :], v_news[m][rs, :]) for m, i, h, rs in heads]
    for (m, i, h, rs), upd in zip(heads, upds):
        g_last = gcs[m][rs.start + c_real - 1:rs.start + c_real, :]
        s_ref[i, h] = s_ref[i, h] * jnp.exp(g_last) + upd

    nw = nw_ref[...]
    for m, i, h, rs in heads:
        o = outs[m][rs, :]
        o = o * lax.rsqrt(jnp.mean(o * o, axis=-1, keepdims=True) + EPS) * nw
        gate = _silu(z_ref[i, :, h * GDN_D:(h + 1) * GDN_D].astype(F32))
        o_ref[i, :, h * GDN_D:(h + 1) * GDN_D] = (o * gate).astype(o_ref.dtype)


def _gdn_core(proj3, gates, a_log, dt_bias, norm_w, s0, c_real, cp, bb):
    B, Tp, _ = proj3.shape
    key = GDN_K_HEADS * GDN_D
    vd = GDN_V_HEADS * GDN_D
    cdim = 2 * key + vd
    nc = Tp // cp
    hg = GROUP_ROWS // cp
    ng = GDN_V_HEADS // hg
    has_state = s0 is not None
    lanes = lambda a: jnp.tile(jnp.repeat(a.reshape(ng, hg), cp, axis=1), (bb, 1))
    full = lambda shape: pl.BlockSpec(shape, lambda b, n: (0,) * len(shape))
    in_specs = [pl.BlockSpec((bb, cp, cdim), lambda b, n: (b, n, 0)),
                pl.BlockSpec((bb, cp, vd), lambda b, n: (b, n, cdim // vd)),
                pl.BlockSpec((None, 2, bb * ng, GROUP_ROWS), lambda b, n: (b * nc + n, 0, 0, 0)),
                full((bb * ng, GROUP_ROWS)), full((bb * ng, GROUP_ROWS)), full((1, GDN_D))]
    args = [proj3, proj3, gates, lanes(a_log), lanes(dt_bias), norm_w.reshape(1, GDN_D)]
    s_spec = pl.BlockSpec((bb, GDN_V_HEADS, GDN_D, GDN_D), lambda b, n: (b, 0, 0, 0))
    if has_state:
        in_specs.append(s_spec)
        args.append(s0)
    return pl.pallas_call(
        functools.partial(_gdn_core_kernel, c_real=c_real, has_state=has_state),
        grid=(B // bb, nc),
        in_specs=in_specs,
        out_specs=[pl.BlockSpec((bb, cp, vd), lambda b, n: (b, n, 0)), s_spec],
        out_shape=[jax.ShapeDtypeStruct((B, Tp, vd), proj3.dtype),
                   jax.ShapeDtypeStruct((B, GDN_V_HEADS, GDN_D, GDN_D), F32)],
        compiler_params=_cparams("parallel", "arbitrary"),
        name="gdn_core",
    )(*args)


def _rope_tables(pos0, t_real, t_pad):
    half = RET_DK // 2
    inv_freq = ROPE_BASE ** (-np.arange(half, dtype=np.float64) / half)
    pos = pos0 + np.arange(t_pad, dtype=np.float64)
    ang = pos[:, None] * inv_freq[None, :]
    live = (np.arange(t_pad) < t_real)[:, None]
    return (jnp.asarray(np.where(live, np.cos(ang), 0.0), F32),
            jnp.asarray(np.where(live, np.sin(ang), 0.0), F32))


def _trunk(x, mods, mod_final, pos0, s_ret, s_gdn, s_gconv, s_fconv, p, wb, tiles, act_dtype):
    B, T, D = x.shape
    Tp = tiles.Tp
    x3 = x if T == Tp else jnp.pad(x, ((0, 0), (0, Tp - T), (0, 0)))
    M = B * Tp
    stateful = s_ret is not None
    new_fconv = []

    mod_mix, mod_ffn = mods[0][:, None, :2 * D], mods[0][:, None, 2 * D:]
    cos, sin = _rope_tables(pos0, T, Tp)
    proj = _ret_proj(tiles, x3, p['norm_mix'][0], mod_mix, wb['ret_in'], cos, sin, act_dtype)
    cp = min(RET_CHUNK, Tp)
    o3, new_ret = _ret_core(proj.reshape(B, Tp, -1), s_ret[0] if stateful else None, min(T, cp), cp)
    x3, cst = _ffn(tiles, o3.reshape(M, -1), wb['ret_out'], x3, p['norm_ffn'][0], mod_ffn,
                   wb['ffn_up'][0], p['w_ffn_dw'][0], p['b_ffn_dw'][0], wb['ffn_down'][0],
                   s_fconv[0] if stateful else None, min(T, tiles.tt))
    new_fconv.append(cst)

    mod_mix, mod_ffn = mods[1][:, None, :2 * D], mods[1][:, None, 2 * D:]
    cp = min(GDN_CHUNK, Tp)
    nc = Tp // cp
    ng = GDN_V_HEADS * cp // GROUP_ROWS
    bb = min(B, max(SUBLANES // ng, GDN_SEQS_PER_STEP))
    proj, bat, new_gconv = _gdn_proj(tiles, x3, p['norm_mix'][1], mod_mix, wb['gdn_in'], wb['gdn_tail'],
                                     p['w_gdn_conv'][0], s_gconv[0] if stateful else None,
                                     min(T, tiles.tt), act_dtype)
    gates = bat.reshape(2, ng, GROUP_ROWS // cp, B // bb, bb, nc, cp)
    gates = gates.transpose(3, 5, 0, 4, 1, 2, 6).reshape(B // bb * nc, 2, bb * ng, GROUP_ROWS)
    o3, new_gdn = _gdn_core(proj.reshape(B, Tp, -1), gates, p['gdn_a_log'][0], p['gdn_dt_bias'][0],
                            p['gdn_norm'][0], s_gdn[0] if stateful else None, min(T, cp), cp, bb)
    y3, cst = _ffn(tiles, o3.reshape(M, -1), wb['gdn_out'], x3, p['norm_ffn'][1], mod_ffn,
                   wb['ffn_up'][1], p['w_ffn_dw'][1], p['b_ffn_dw'][1], wb['ffn_down'][1],
                   s_fconv[1] if stateful else None, min(T, tiles.tt),
                   final=(p['norm_final'], mod_final[:, None, :]))
    new_fconv.append(cst)
    return (y3[:, :T], new_ret[None], new_gdn[None], new_gconv[None], jnp.stack(new_fconv))


def kernel(x_prompt, x_sample, state_ret, state_gdn, state_gdn_conv, state_ffn_conv, c_prompt, c_sample, w_ada, b_ada, w_ada_final, b_ada_final, norm_mix, norm_ffn, norm_final, w_ret_in, w_ret_out, w_gdn_in, w_gdn_conv, gdn_a_log, gdn_dt_bias, gdn_norm, w_gdn_out, w_ffn_up, w_ffn_dw, b_ffn_dw, w_ffn_down):
    p = {'norm_mix': norm_mix, 'norm_ffn': norm_ffn, 'norm_final': norm_final,
         'w_gdn_conv': w_gdn_conv, 'gdn_a_log': gdn_a_log, 'gdn_dt_bias': gdn_dt_bias,
         'gdn_norm': gdn_norm, 'w_ffn_dw': w_ffn_dw, 'b_ffn_dw': b_ffn_dw}
    cdim = (2 * GDN_K_HEADS + GDN_V_HEADS) * GDN_D
    vd = GDN_V_HEADS * GDN_D
    wb = {'ret_in': w_ret_in[0].astype(BF16), 'ret_out': w_ret_out[0].astype(BF16),
          'gdn_in': w_gdn_in[0].astype(BF16), 'gdn_out': w_gdn_out[0].astype(BF16),
          'gdn_tail': w_gdn_in[0][:, cdim + vd:],
          'ffn_up': [w.astype(BF16) for w in w_ffn_up], 'ffn_down': [w.astype(BF16) for w in w_ffn_down]}

    bp, tp = x_prompt.shape[:2]
    bs_ = x_sample.shape[0]
    c_all = jnp.concatenate([c_prompt, c_sample], axis=0)
    mods = _ada_mod(c_all, w_ada, b_ada)
    mod_final = _ada_mod(c_all, w_ada_final[None], b_ada_final[None])[0]

    out_p = _trunk(x_prompt, mods[:, :bp], mod_final[:bp], 0, None, None, None, None, p, wb,
                   _Tiles(bp, tp, 1, min(512, tp)), BF16)
    out_s = _trunk(x_sample, mods[:, bp:], mod_final[bp:], PAST_LEN, state_ret, state_gdn,
                   state_gdn_conv, state_ffn_conv, p, wb,
                   _Tiles(bs_, SUBLANES, min(32, bs_), SUBLANES), BF16)
    y_p, ret_p, gdn_p, gconv_p, fconv_p = out_p
    y_s, ret_s, gdn_s, gconv_s, fconv_s = out_s
    return (y_p, y_s, ret_p, gdn_p, gconv_p, fconv_p, ret_s, gdn_s, gconv_s, fconv_s)
```

```python
import functools
import math

import numpy as np
import jax
import jax.numpy as jnp
from jax import lax
from jax.experimental import pallas as pl
from jax.experimental.pallas import tpu as pltpu

F32 = jnp.float32
BF16 = jnp.bfloat16
EPS = 1e-6
ROPE_BASE = 10000.0
PAST_LEN = 16384

RET_HEADS = 4
RET_DK = 256
RET_DV = 512
RET_CHUNK = 128
GDN_K_HEADS = 8
GDN_V_HEADS = 16
GDN_D = 128
GDN_CHUNK = 64
GDN_CONV_W = 4
FFN_CONV_W = 3
SUBLANES = 8
LANES = 128
INV_BASE = 8
GROUP_ROWS = 128
GDN_SEQS_PER_STEP = 2
RET_SEQS_PER_STEP = 2
PROJ_COLS = 512
FFN_COLS = 256
VMEM_LIMIT = 48 * 1024 * 1024

_HIGHEST = lax.Precision.HIGHEST


def _cparams(*sem):
    return pltpu.CompilerParams(dimension_semantics=sem, vmem_limit_bytes=VMEM_LIMIT)


def _silu_of_half(hx):
    return hx + hx * jnp.tanh(hx)


def _silu(x):
    return _silu_of_half(0.5 * x)


def _softplus(x):
    return jnp.maximum(x, 0.0) + jnp.log1p(jnp.exp(-jnp.abs(x)))


def _ada_norm(x, gamma, shift, scale):
    ms = jnp.mean(x * x, axis=-1, keepdims=True)
    xn = x * lax.rsqrt(ms + EPS) * gamma
    return xn * (1.0 + scale) + shift


def _dot(a, b):
    return jnp.dot(a.astype(BF16), b.astype(BF16), preferred_element_type=F32)


def _dot_nt(a, b):
    return lax.dot_general(a.astype(BF16), b.astype(BF16), (((1,), (1,)), ((), ())),
                           preferred_element_type=F32)


def _dot_tn(a, b):
    return lax.dot_general(a.astype(BF16), b.astype(BF16), (((0,), (0,)), ((), ())),
                           preferred_element_type=F32)


def _cast_kernel(w_ref, o_ref):
    o_ref[...] = w_ref[...].astype(o_ref.dtype)


def _layer_bf16(w, layer, rows=256):
    _, K, N = w.shape
    return pl.pallas_call(
        _cast_kernel,
        grid=(K // rows,),
        in_specs=[pl.BlockSpec((None, rows, N), lambda i: (layer, i, 0))],
        out_specs=pl.BlockSpec((rows, N), lambda i: (i, 0)),
        out_shape=jax.ShapeDtypeStruct((K, N), BF16),
        compiler_params=_cparams("parallel"),
        name="cast_bf16",
    )(w)


def _ada_kernel(c_ref, w_ref, b_ref, o_ref):
    cs = _silu(c_ref[...])
    o_ref[...] = _dot(cs, w_ref[...]) + b_ref[...]


def _ada_mod(c, w, b, tn=1024):
    L, D, N = w.shape
    Mc = c.shape[0]
    return pl.pallas_call(
        _ada_kernel,
        grid=(L, N // tn),
        in_specs=[pl.BlockSpec((Mc, D), lambda l, j: (0, 0)),
                  pl.BlockSpec((None, D, tn), lambda l, j: (l, 0, j)),
                  pl.BlockSpec((None, 1, tn), lambda l, j: (l, 0, j))],
        out_specs=pl.BlockSpec((None, Mc, tn), lambda l, j: (l, 0, j)),
        out_shape=jax.ShapeDtypeStruct((L, Mc, N), F32),
        compiler_params=_cparams("parallel", "parallel"),
        name="ada_mod",
    )(c, w, b.reshape(L, 1, N))


class _Tiles:
    def __init__(self, B, Tp, bt, tt):
        assert bt == 1 or tt == Tp
        self.B, self.Tp, self.bt, self.tt = B, Tp, bt, tt
        self.tps = Tp // tt
        self.n = (B // bt) * self.tps
        self.tm = bt * tt

    def x_spec(self, D):
        tps = self.tps
        return pl.BlockSpec((self.bt, self.tt, D), lambda i: (i // tps, i % tps, 0))

    def seq_spec(self, r, D):
        tps = self.tps
        return pl.BlockSpec((self.bt, r, D), lambda i: (i // tps, 0, 0))

    def rows_spec(self, N):
        return pl.BlockSpec((self.tm, N), lambda i: (i, 0))

    def pos_spec(self, N):
        tps = self.tps
        return pl.BlockSpec((self.tt, N), lambda i: (i % tps, 0))

    def tail_rows_spec(self, r, N):
        return pl.BlockSpec((self.bt, r, N), lambda i: (i, 0, 0))

    def last_tile(self, a):
        return a[self.tps - 1::self.tps]


def _resident(shape):
    return pl.BlockSpec(shape, lambda i: (0,) * len(shape), pipeline_mode=pl.Buffered(1))


def _mod(m3, k, D):
    return m3[:, :, k * D:(k + 1) * D]


def _normed_rows(x_ref, gam_ref, mod_ref, k_shift, h_ref):
    D = x_ref.shape[-1]
    m3 = mod_ref[...]
    h3 = _ada_norm(x_ref[...], gam_ref[...], _mod(m3, k_shift, D), _mod(m3, k_shift + 1, D))
    h_ref[...] = h3.reshape(h_ref.shape).astype(BF16)
    return h3


def _causal_taps(g3, prev_ref, sl, width):
    ext = jnp.concatenate([prev_ref[:, :, sl], g3], axis=1)
    return [pltpu.roll(ext, k, axis=1)[:, SUBLANES:, :] for k in range(width - 1, 0, -1)]


def _start_of_sequence(tiles):
    return (pl.program_id(0) % tiles.tps) == 0


def _ret_proj_kernel(x_ref, gam_ref, mod_ref, w_ref, cos_ref, sin_ref, o_ref, h_ref):
    bt, tt, _ = x_ref.shape
    _normed_rows(x_ref, gam_ref, mod_ref, 0, h_ref)
    cos, sin = cos_ref[...], sin_ref[...]
    qk = RET_HEADS * RET_DK
    half = RET_DK // 2
    for j in range(w_ref.shape[1] // PROJ_COLS):
        sl = slice(j * PROJ_COLS, (j + 1) * PROJ_COLS)
        y = jnp.dot(h_ref[...], w_ref[:, sl], preferred_element_type=F32)
        if sl.start < 2 * qk:
            y3 = y.reshape(bt, tt, PROJ_COLS)
            parts = []
            for c in range(PROJ_COLS // RET_DK):
                x1 = y3[:, :, c * RET_DK:c * RET_DK + half]
                x2 = y3[:, :, c * RET_DK + half:(c + 1) * RET_DK]
                parts += [x1 * cos - x2 * sin, x1 * sin + x2 * cos]
            y = jnp.concatenate(parts, axis=-1).reshape(bt * tt, PROJ_COLS)
            if sl.start >= qk:
                y = y * (RET_DK ** -0.5)
        o_ref[:, sl] = y.astype(o_ref.dtype)


def _ret_proj(tiles, x3, gamma, mod3, w, cos, sin, out_dtype):
    B, Tp, D = x3.shape
    N = w.shape[1]
    return pl.pallas_call(
        _ret_proj_kernel,
        grid=(tiles.n,),
        in_specs=[tiles.x_spec(D), _resident((1, D)), tiles.seq_spec(1, mod3.shape[-1]),
                  _resident((D, N)), tiles.pos_spec(cos.shape[1]), tiles.pos_spec(sin.shape[1])],
        out_specs=tiles.rows_spec(N),
        out_shape=jax.ShapeDtypeStruct((B * Tp, N), out_dtype),
        scratch_shapes=[pltpu.VMEM((tiles.tm, D), BF16)],
        compiler_params=_cparams("parallel"),
        name="ret_proj",
    )(x3, gamma.reshape(1, D), mod3, w, cos, sin)


def _gdn_proj_kernel(*refs, tiles, c_real, has_state):
    it = iter(refs)
    x_ref, gam_ref, mod_ref, w_ref, wtt_ref, wc_ref = (next(it) for _ in range(6))
    if has_state:
        cs_ref = next(it)
    o_ref, ott_ref, cst_ref = next(it), next(it), next(it)
    h_ref, prev_ref = next(it), next(it)
    bt, tt, _ = x_ref.shape
    key = GDN_K_HEADS * GDN_D
    cdim = wc_ref.shape[1]
    keep = GDN_CONV_W - 1

    h3 = _normed_rows(x_ref, gam_ref, mod_ref, 0, h_ref)
    nt = ott_ref.shape[0]
    h_hi = h_ref[...]
    h_lo = (h3.reshape(bt * tt, -1) - h_hi.astype(F32)).astype(BF16)
    w_hl = wtt_ref[...]
    a = _dot_nt(w_hl, h_hi)
    ott_ref[...] = a[:nt] + a[nt:] + _dot_nt(w_hl[:nt], h_lo)
    if has_state:
        prev_ref[...] = jnp.zeros_like(prev_ref)
        prev_ref[:, SUBLANES - keep:, :] = cs_ref[...]
    else:
        @pl.when(_start_of_sequence(tiles))
        def _():
            prev_ref[:, 0:SUBLANES, :] = jnp.zeros((prev_ref.shape[0], SUBLANES, LANES), F32)

    def head_norm(a, col):
        if col >= 2 * key:
            return a
        inv = lax.rsqrt(jnp.sum(a * a, axis=-1, keepdims=True) + EPS)
        return a * (inv * (GDN_D ** -0.5) if col < key else inv)

    def conv_by_phase(col):
        slab = col // LANES
        n8 = tt // SUBLANES
        cur =[prev_ref[slab, pl.ds(SUBLANES + b, n8, stride=SUBLANES), :] for b in range(SUBLANES)]
        old = {b: prev_ref[slab, pl.ds(b, n8, stride=SUBLANES), :]
               for b in range(SUBLANES - keep, SUBLANES)}
        wch = 0.5 * wc_ref[:, col:col + LANES]
        outs = []
        for b in range(SUBLANES):
            acc = wch[keep:keep + 1, :] * cur[b]
            for k in range(1, keep + 1):
                src = cur[b - k] if b >= k else old[b - k + SUBLANES]
                acc = acc + wch[keep - k:keep - k + 1, :] * src
            outs.append(head_norm(_silu_of_half(acc), col))
        cst_ref[0, :, col:col + LANES] = prev_ref[slab, SUBLANES + c_real - keep:SUBLANES + c_real, :]
        prev_ref[slab, 0:SUBLANES, :] = prev_ref[slab, tt:tt + SUBLANES, :]
        for b in range(SUBLANES):
            prev_ref[slab, pl.ds(SUBLANES + b, n8, stride=SUBLANES), :] = outs[b]
        o_ref[:, col:col + LANES] = prev_ref[slab, SUBLANES:, :].astype(o_ref.dtype)

    for j in range(o_ref.shape[1] // PROJ_COLS):
        sl = slice(j * PROJ_COLS, (j + 1) * PROJ_COLS)
        y = jnp.dot(h_ref[...], w_ref[:, sl], preferred_element_type=F32)
        if sl.start < cdim and not has_state:
            for c in range(0, PROJ_COLS, LANES):
                prev_ref[(sl.start + c) // LANES, SUBLANES:, :] = y[:, c:c + LANES]
            for c in range(0, PROJ_COLS, LANES):
                conv_by_phase(sl.start + c)
            continue
        if sl.start < cdim:
            y3 = y.reshape(bt, tt, PROJ_COLS)
            taps = _causal_taps(y3, prev_ref, sl, GDN_CONV_W)
            wch = 0.5 * wc_ref[:, sl]
            acc = wch[keep:keep + 1, :] * y3
            for i, tap in enumerate(taps):
                acc = acc + wch[i:i + 1, :] * tap
            acc = _silu_of_half(acc)
            acc = jnp.concatenate([head_norm(acc[:, :, c:c + GDN_D], sl.start + c)
                                   for c in range(0, PROJ_COLS, GDN_D)], axis=-1)
            cst_ref[:, :, sl] = y3[:, c_real - keep:c_real, :]
            y = acc.reshape(bt * tt, PROJ_COLS)
        o_ref[:, sl] = y.astype(o_ref.dtype)


def _gdn_proj(tiles, x3, gamma, mod3, w, w_tail, w_conv, conv_state, c_real, out_dtype):
    B, Tp, D = x3.shape
    nt = w_tail.shape[1]
    cdim = w_conv.shape[1]
    N = cdim + GDN_V_HEADS * GDN_D
    keep = GDN_CONV_W - 1
    has_state = conv_state is not None
    wt_hi = w_tail.T.astype(BF16)
    wt_lo = (w_tail.T - wt_hi.astype(F32)).astype(BF16)
    in_specs = [tiles.x_spec(D), _resident((1, D)), tiles.seq_spec(1, mod3.shape[-1]),
                _resident(w.shape), _resident((2 * nt, D)), _resident((GDN_CONV_W, cdim))]
    args = [x3, gamma.reshape(1, D), mod3, w, jnp.concatenate([wt_hi, wt_lo], axis=0), w_conv]
    if has_state:
        in_specs.append(tiles.seq_spec(keep, cdim))
        args.append(conv_state)
    proj, ott, cst = pl.pallas_call(
        functools.partial(_gdn_proj_kernel, tiles=tiles, c_real=c_real, has_state=has_state),
        grid=(tiles.n,),
        in_specs=in_specs,
        out_specs=[tiles.rows_spec(N), pl.BlockSpec((nt, tiles.tm), lambda i: (0, i)),
                   tiles.tail_rows_spec(keep, cdim)],
        out_shape=[jax.ShapeDtypeStruct((B * Tp, N), out_dtype),
                   jax.ShapeDtypeStruct((nt, B * Tp), F32),
                   jax.ShapeDtypeStruct((tiles.n * tiles.bt, keep, cdim), F32)],
        scratch_shapes=[pltpu.VMEM((tiles.tm, D), BF16),
                        pltpu.VMEM((tiles.bt, SUBLANES, cdim) if has_state
                                   else (cdim // LANES, SUBLANES + tiles.tt, LANES), F32)],
        compiler_params=_cparams("arbitrary"),
        name="gdn_proj",
    )(*args)
    return proj, ott, tiles.last_tile(cst)


def _ffn_kernel(*refs, tiles, c_real, has_state, final):
    it = iter(refs)
    o_ref, wo_ref, x_ref, gam_ref, mod_ref = (next(it) for _ in range(5))
    wup_ref, wdw_ref, bdw_ref, wd_ref = (next(it) for _ in range(4))
    if has_state:
        cs_ref = next(it)
    if final:
        gamf_ref, modf_ref = next(it), next(it)
    y_ref, cst_ref = next(it), next(it)
    xm_ref, h_ref, act_ref, prev_ref = (next(it) for _ in range(4))
    bt, tt, D = x_ref.shape
    fd = wd_ref.shape[0]
    keep = FFN_CONV_W - 1

    xm_ref[...] = x_ref[...] + _mod(mod_ref[...], 0, D) * _dot(o_ref[...], wo_ref[...]).reshape(bt, tt, D)
    _normed_rows(xm_ref, gam_ref, mod_ref, 1, h_ref)
    if has_state:
        prev_ref[...] = jnp.zeros_like(prev_ref)
        prev_ref[:, SUBLANES - keep:, :] = cs_ref[...]
    else:
        @pl.when(_start_of_sequence(tiles))
        def _():
            prev_ref[...] = jnp.zeros_like(prev_ref)

    for f in range(fd // FFN_COLS):
        sl = slice(f * FFN_COLS, (f + 1) * FFN_COLS)
        h = h_ref[...]
        g3 = jnp.dot(h, wup_ref[:, sl], preferred_element_type=F32).reshape(bt, tt, FFN_COLS)
        val = jnp.dot(h, wup_ref[:, fd + sl.start:fd + sl.stop], preferred_element_type=F32)
        s2, s1 = _causal_taps(g3, prev_ref, sl, FFN_CONV_W)
        wh = 0.5 * wdw_ref[:, sl]
        conv = wh[0:1, :] * s2 + wh[1:2, :] * s1 + wh[2:3, :] * g3 + 0.5 * bdw_ref[:, sl]
        cst_ref[:, :, sl] = g3[:, c_real - keep:c_real, :]
        if not has_state:
            prev_ref[:, :, sl] = g3[:, tt - SUBLANES:, :]
        act_ref[:, sl] = (_silu_of_half(conv).reshape(bt * tt, FFN_COLS) * val).astype(BF16)

    acc = jnp.dot(act_ref[...], wd_ref[...], preferred_element_type=F32).reshape(bt, tt, D)
    xn = xm_ref[...] + _mod(mod_ref[...], 3, D) * acc
    if final:
        mf = modf_ref[...]
        xn = _ada_norm(xn, gamf_ref[...], _mod(mf, 0, D), _mod(mf, 1, D))
    y_ref[...] = xn


def _ffn(tiles, o, w_out, x3, gamma, mod3, w_up, w_dw, b_dw, w_down, conv_state, c_real, final=None):
    B, Tp, D = x3.shape
    Fd = w_down.shape[0]
    K = o.shape[1]
    keep = FFN_CONV_W - 1
    has_state = conv_state is not None
    in_specs = [tiles.rows_spec(K), _resident((K, D)),
                tiles.x_spec(D), _resident((1, D)), tiles.seq_spec(1, mod3.shape[-1]),
                _resident((D, 2 * Fd)), _resident((FFN_CONV_W, Fd)), _resident((1, Fd)),
                _resident((Fd, D))]
    args = [o, w_out, x3, gamma.reshape(1, D), mod3, w_up, w_dw, b_dw.reshape(1, Fd), w_down]
    if has_state:
        in_specs.append(tiles.seq_spec(keep, Fd))
        args.append(conv_state)
    if final is not None:
        in_specs += [_resident((1, D)), tiles.seq_spec(1, final[1].shape[-1])]
        args += [final[0].reshape(1, D), final[1]]
    y3, cst = pl.pallas_call(
        functools.partial(_ffn_kernel, tiles=tiles, c_real=c_real, has_state=has_state,
                          final=final is not None),
        grid=(tiles.n,),
        in_specs=in_specs,
        out_specs=[tiles.x_spec(D), tiles.tail_rows_spec(keep, Fd)],
        out_shape=[jax.ShapeDtypeStruct((B, Tp, D), F32),
                   jax.ShapeDtypeStruct((tiles.n * tiles.bt, keep, Fd), F32)],
        scratch_shapes=[pltpu.VMEM((tiles.bt, tiles.tt, D), F32), pltpu.VMEM((tiles.tm, D), BF16),
                        pltpu.VMEM((tiles.tm, Fd), BF16), pltpu.VMEM((tiles.bt, SUBLANES, Fd), F32)],
        compiler_params=_cparams("arbitrary"),
        name="out_proj_ffn",
    )(*args)
    return y3, tiles.last_tile(cst)


def _ret_core_kernel(*refs, c_real, has_state):
    it = iter(refs)
    q_ref, k_ref, v_ref, g_ref = (next(it) for _ in range(4))
    if has_state:
        s0_ref = next(it)
    o_ref, s_ref = next(it), next(it)
    nb, cp, _ = q_ref.shape

    @pl.when(pl.program_id(1) == 0)
    def _():
        if has_state:
            s_ref[...] = s0_ref[...]
        else:
            s_ref[...] = jnp.zeros_like(s_ref)

    ri = lax.broadcasted_iota(jnp.int32, (cp, cp), 0)
    ci = lax.broadcasted_iota(jnp.int32, (cp, cp), 1)
    causal = ri >= ci
    diff = jnp.where(causal, ri - ci, 0).astype(F32)
    rowi = lax.broadcasted_iota(jnp.int32, (cp, 1), 0)
    row = rowi.astype(F32)
    lgs = [math.log(1.0 - 2.0 ** (-5.0 - h)) for h in range(RET_HEADS)]
    decays = [jnp.where(causal, jnp.exp(diff * lg), 0.0) for lg in lgs]
    k_decays = [jnp.where(rowi < c_real, jnp.exp((c_real - 1.0 - row) * lg), 0.0) for lg in lgs]
    q_decays = [jnp.exp((row + 1.0) * lg) for lg in lgs]

    items = [(i, h) for i in range(nb) for h in range(RET_HEADS)]
    qs = [q_ref[i, :, h * RET_DK:(h + 1) * RET_DK] for i, h in items]
    ks = [k_ref[i, :, h * RET_DK:(h + 1) * RET_DK].astype(F32) for i, h in items]
    vs = [v_ref[i, :, h * RET_DV:(h + 1) * RET_DV] for i, h in items]
    ss = [s_ref[i, h] for i, h in items]
    scores = [_dot_nt(q, k) for q, k in zip(qs, ks)]
    cross = [_dot(q, s) for q, s in zip(qs, ss)]
    upds = [_dot_tn(k * k_decays[h], v) for k, v, (i, h) in zip(ks, vs, items)]
    inner = [_dot(sc * decays[h], v) for sc, v, (i, h) in zip(scores, vs, items)]
    for m, (i, h) in enumerate(items):
        s_ref[i, h] = ss[m] * math.exp(c_real * lgs[h]) + upds[m]
        o = inner[m] + cross[m] * q_decays[h]
        o = o * lax.rsqrt(jnp.mean(o * o, axis=-1, keepdims=True) + EPS)
        gate = _silu(g_ref[i, :, h * RET_DV:(h + 1) * RET_DV].astype(F32))
        o_ref[i, :, h * RET_DV:(h + 1) * RET_DV] = (o * gate).astype(o_ref.dtype)


def _ret_core(proj3, s0, c_real, cp):
    B, Tp, _ = proj3.shape
    qk = RET_HEADS * RET_DK
    vd = RET_HEADS * RET_DV
    has_state = s0 is not None
    nb = min(B, RET_SEQS_PER_STEP)
    in_specs = [pl.BlockSpec((nb, cp, qk), lambda b, n: (b, n, 0)),
                pl.BlockSpec((nb, cp, qk), lambda b, n: (b, n, 1)),
                pl.BlockSpec((nb, cp, vd), lambda b, n: (b, n, 1)),
                pl.BlockSpec((nb, cp, vd), lambda b, n: (b, n, 2))]
    args = [proj3, proj3, proj3, proj3]
    s_spec = pl.BlockSpec((nb, RET_HEADS, RET_DK, RET_DV), lambda b, n: (b, 0, 0, 0))
    if has_state:
        in_specs.append(s_spec)
        args.append(s0)
    return pl.pallas_call(
        functools.partial(_ret_core_kernel, c_real=c_real, has_state=has_state),
        grid=(B // nb, Tp // cp),
        in_specs=in_specs,
        out_specs=[pl.BlockSpec((nb, cp, vd), lambda b, n: (b, n, 0)), s_spec],
        out_shape=[jax.ShapeDtypeStruct((B, Tp, vd), proj3.dtype),
                   jax.ShapeDtypeStruct((B, RET_HEADS, RET_DK, RET_DV), F32)],
        compiler_params=_cparams("parallel", "arbitrary"),
        name="ret_core",
    )(*args)


def _block_inverse_many(ls, ri, ci, bs):
    eye = (ri == ci).astype(F32)
    base = (ri // INV_BASE) == (ci // INV_BASE)
    pws = [-jnp.where(base, l, 0.0) for l in ls]
    ps = [eye + m for m in pws]
    span = 2
    while span < INV_BASE:
        pws = [_dot(pw, pw) for pw in pws]
        ps = [p + _dot(p, pw) for p, pw in zip(ps, pws)]
        span *= 2
    size = INV_BASE
    while size < bs:
        off = ((ri // (2 * size)) == (ci // (2 * size))) & ((ri // size) != (ci // size))
        xs = [_dot(jnp.where(off, l, 0.0), p) for l, p in zip(ls, ps)]
        ps = [p - _dot(p, x) for p, x in zip(ps, xs)]
        size *= 2
    return ps


def _gdn_core_kernel(*refs, c_real, has_state):
    it = iter(refs)
    qkv_ref, z_ref, gt_ref, al_ref, dt_ref, nw_ref = (next(it) for _ in range(6))
    if has_state:
        s0_ref = next(it)
    o_ref, s_ref = next(it), next(it)
    bb, cp, _ = qkv_ref.shape
    bs = cp
    hg = GROUP_ROWS // bs
    ng = GDN_V_HEADS // hg
    rep = GDN_V_HEADS // GDN_K_HEADS
    key = GDN_K_HEADS * GDN_D
    n_items = bb * ng
    assert n_items % SUBLANES == 0

    @pl.when(pl.program_id(1) == 0)
    def _():
        if has_state:
            s_ref[...] = s0_ref[...]
        else:
            s_ref[...] = jnp.zeros_like(s_ref)

    ri = lax.broadcasted_iota(jnp.int32, (GROUP_ROWS, GROUP_ROWS), 0)
    ci = lax.broadcasted_iota(jnp.int32, (GROUP_ROWS, GROUP_ROWS), 1)
    same = (ri // bs) == (ci // bs)
    incl = same & (ri >= ci)
    strict = same & (ri > ci)
    eye = (ri == ci).astype(F32)

    live = (lax.broadcasted_iota(jnp.int32, (1, GROUP_ROWS), 1) % bs) < c_real
    beta_rows = jnp.where(live, jax.nn.sigmoid(gt_ref[0]), 0.0)
    g_rows = jnp.where(live, -jnp.exp(al_ref[...]) * _softplus(gt_ref[1] + dt_ref[...]), 0.0)
    gsum_rows = jnp.dot(g_rows, (same & (ri <= ci)).astype(F32), precision=_HIGHEST,
                        preferred_element_type=F32)
    cols = jnp.concatenate([beta_rows, gsum_rows,
                            jnp.zeros((GROUP_ROWS - 2 * n_items, GROUP_ROWS), F32)], axis=0).T

    items = [(i, j) for i in range(bb) for j in range(ng)]

    def stack(i, j, col_of_head):
        return jnp.concatenate(
            [qkv_ref[i, :, col_of_head(j * hg + hh):col_of_head(j * hg + hh) + GDN_D].astype(F32)
             for hh in range(hg)], axis=0)

    qxs = [stack(i, j, lambda h: (h // rep) * GDN_D) for i, j in items]
    kxs = [stack(i, j, lambda h: key + (h // rep) * GDN_D) for i, j in items]
    vxs = [stack(i, j, lambda h: 2 * key + h * GDN_D) for i, j in items]
    bcs = [jnp.broadcast_to(cols[:, m:m + 1], (GROUP_ROWS, GROUP_ROWS)) for m in range(n_items)]
    gcs = [jnp.broadcast_to(cols[:, n_items + m:n_items + m + 1], (GROUP_ROWS, GROUP_ROWS))
           for m in range(n_items)]
    kbs = [kx * bc for kx, bc in zip(kxs, bcs)]
    kks = [_dot_nt(kb, kx) for kb, kx in zip(kbs, kxs)]
    qks = [_dot_nt(qx, kx) for qx, kx in zip(qxs, kxs)]
    decays = [jnp.exp(jnp.where(incl, gc - gsum_rows[m:m + 1, :], -jnp.inf))
              for m, gc in enumerate(gcs)]
    ls = [jnp.where(strict, kk * d, 0.0) for kk, d in zip(kks, decays)]
    attns = [qk * d for qk, d in zip(qks, decays)]
    ps = _block_inverse_many(ls, ri, ci, bs)
    egs = [jnp.exp(gc) for gc in gcs]
    rhss = [jnp.concatenate([vx * bc, kb * eg], axis=-1) for vx, bc, kb, eg in zip(vxs, bcs, kbs, egs)]
    sols = [rhs + _dot(p - eye, rhs) for rhs, p in zip(rhss, ps)]
    qes = [qx * eg for qx, eg in zip(qxs, egs)]

    heads = [(m, i, j * hg + hh, slice(hh * bs, (hh + 1) * bs))
             for m, (i, j) in enumerate(items) for hh in range(hg)]
    wqs = [_dot(jnp.concatenate([sols[m][rs, GDN_D:], qes[m][rs, :]], axis=0), s_ref[i, h])
           for m, i, h, rs in heads]
    v_news, qss = [], []
    for m in range(n_items):
        part = wqs[m * hg:(m + 1) * hg]
        v_news.append(sols[m][:, :GDN_D] - jnp.concatenate([r[:bs] for r in part], axis=0))
        qss.append(jnp.concatenate([r[bs:] for r in part], axis=0))
    outs = [qs + _dot(attn, vn) for qs, attn, vn in zip(qss, attns, v_news)]

    def last_rows(gc):
        return [gc[hh * bs + c_real - 1:hh * bs + c_real, :] for hh in range(hg)]
    kds = [kx * jnp.exp(jnp.concatenate([jnp.broadcast_to(r, (bs, GROUP_ROWS)) for r in last_rows(gc)],
                                        axis=0) - gc)
           for kx, gc in zip(kxs, gcs)]
    upds = [_dot_tn(kds[m][rs, :], v_news[m][rs, :]) for m, i, h, rs in heads]
    for (m, i, h, rs), upd in zip(heads, upds):
        g_last = gcs[m][rs.start + c_real - 1:rs.start + c_real, :]
        s_ref[i, h] = s_ref[i, h] * jnp.exp(g_last) + upd

    nw = nw_ref[...]
    for m, i, h, rs in heads:
        o = outs[m][rs, :]
        o = o * lax.rsqrt(jnp.mean(o * o, axis=-1, keepdims=True) + EPS) * nw
        gate = _silu(z_ref[i, :, h * GDN_D:(h + 1) * GDN_D].astype(F32))
        o_ref[i, :, h * GDN_D:(h + 1) * GDN_D] = (o * gate).astype(o_ref.dtype)


def _gdn_core(proj3, gates, a_log, dt_bias, norm_w, s0, c_real, cp, bb):
    B, Tp, _ = proj3.shape
    key = GDN_K_HEADS * GDN_D
    vd = GDN_V_HEADS * GDN_D
    cdim = 2 * key + vd
    nc = Tp // cp
    hg = GROUP_ROWS // cp
    ng = GDN_V_HEADS // hg
    has_state = s0 is not None
    lanes = lambda a: jnp.tile(jnp.repeat(a.reshape(ng, hg), cp, axis=1), (bb, 1))
    full = lambda shape: pl.BlockSpec(shape, lambda b, n: (0,) * len(shape))
    in_specs = [pl.BlockSpec((bb, cp, cdim), lambda b, n: (b, n, 0)),
                pl.BlockSpec((bb, cp, vd), lambda b, n: (b, n, cdim // vd)),
                pl.BlockSpec((None, 2, bb * ng, GROUP_ROWS), lambda b, n: (b * nc + n, 0, 0, 0)),
                full((bb * ng, GROUP_ROWS)), full((bb * ng, GROUP_ROWS)), full((1, GDN_D))]
    args = [proj3, proj3, gates, lanes(a_log), lanes(dt_bias), norm_w.reshape(1, GDN_D)]
    s_spec = pl.BlockSpec((bb, GDN_V_HEADS, GDN_D, GDN_D), lambda b, n: (b, 0, 0, 0))
    if has_state:
        in_specs.append(s_spec)
        args.append(s0)
    return pl.pallas_call(
        functools.partial(_gdn_core_kernel, c_real=c_real, has_state=has_state),
        grid=(B // bb, nc),
        in_specs=in_specs,
        out_specs=[pl.BlockSpec((bb, cp, vd), lambda b, n: (b, n, 0)), s_spec],
        out_shape=[jax.ShapeDtypeStruct((B, Tp, vd), proj3.dtype),
                   jax.ShapeDtypeStruct((B, GDN_V_HEADS, GDN_D, GDN_D), F32)],
        compiler_params=_cparams("parallel", "arbitrary"),
        name="gdn_core",
    )(*args)


def _rope_tables(pos0, t_real, t_pad):
    half = RET_DK // 2
    inv_freq = ROPE_BASE ** (-np.arange(half, dtype=np.float64) / half)
    pos = pos0 + np.arange(t_pad, dtype=np.float64)
    ang = pos[:, None] * inv_freq[None, :]
    live = (np.arange(t_pad) < t_real)[:, None]
    return (jnp.asarray(np.where(live, np.cos(ang), 0.0), F32),
            jnp.asarray(np.where(live, np.sin(ang), 0.0), F32))


def _trunk(x, mods, mod_final, pos0, s_ret, s_gdn, s_gconv, s_fconv, p, wb, tiles, act_dtype):
    B, T, D = x.shape
    Tp = tiles.Tp
    x3 = x if T == Tp else jnp.pad(x, ((0, 0), (0, Tp - T), (0, 0)))
    M = B * Tp
    stateful = s_ret is not None
    new_fconv = []

    mod_mix, mod_ffn = mods[0][:, None, :2 * D], mods[0][:, None, 2 * D:]
    cos, sin = _rope_tables(pos0, T, Tp)
    proj = _ret_proj(tiles, x3, p['norm_mix'][0], mod_mix, wb['ret_in'], cos, sin, act_dtype)
    cp = min(RET_CHUNK, Tp)
    o3, new_ret = _ret_core(proj.reshape(B, Tp, -1), s_ret[0] if stateful else None, min(T, cp), cp)
    x3, cst = _ffn(tiles, o3.reshape(M, -1), wb['ret_out'], x3, p['norm_ffn'][0], mod_ffn,
                   wb['ffn_up'][0], p['w_ffn_dw'][0], p['b_ffn_dw'][0], wb['ffn_down'][0],
                   s_fconv[0] if stateful else None, min(T, tiles.tt))
    new_fconv.append(cst)

    mod_mix, mod_ffn = mods[1][:, None, :2 * D], mods[1][:, None, 2 * D:]
    cp = min(GDN_CHUNK, Tp)
    nc = Tp // cp
    ng = GDN_V_HEADS * cp // GROUP_ROWS
    bb = min(B, max(SUBLANES // ng, GDN_SEQS_PER_STEP))
    proj, bat, new_gconv = _gdn_proj(tiles, x3, p['norm_mix'][1], mod_mix, wb['gdn_in'], wb['gdn_tail'],
                                     p['w_gdn_conv'][0], s_gconv[0] if stateful else None,
                                     min(T, tiles.tt), act_dtype)
    gates = bat.reshape(2, ng, GROUP_ROWS // cp, B // bb, bb, nc, cp)
    gates = gates.transpose(3, 5, 0, 4, 1, 2, 6).reshape(B // bb * nc, 2, bb * ng, GROUP_ROWS)
    o3, new_gdn = _gdn_core(proj.reshape(B, Tp, -1), gates, p['gdn_a_log'][0], p['gdn_dt_bias'][0],
                            p['gdn_norm'][0], s_gdn[0] if stateful else None, min(T, cp), cp, bb)
    y3, cst = _ffn(tiles, o3.reshape(M, -1), wb['gdn_out'], x3, p['norm_ffn'][1], mod_ffn,
                   wb['ffn_up'][1], p['w_ffn_dw'][1], p['b_ffn_dw'][1], wb['ffn_down'][1],
                   s_fconv[1] if stateful else None, min(T, tiles.tt),
                   final=(p['norm_final'], mod_final[:, None, :]))
    new_fconv.append(cst)
    return (y3[:, :T], new_ret[None], new_gdn[None], new_gconv[None], jnp.stack(new_fconv))


def kernel(x_prompt, x_sample, state_ret, state_gdn, state_gdn_conv, state_ffn_conv, c_prompt, c_sample, w_ada, b_ada, w_ada_final, b_ada_final, norm_mix, norm_ffn, norm_final, w_ret_in, w_ret_out, w_gdn_in, w_gdn_conv, gdn_a_log, gdn_dt_bias, gdn_norm, w_gdn_out, w_ffn_up, w_ffn_dw, b_ffn_dw, w_ffn_down):
    p = {'norm_mix': norm_mix, 'norm_ffn': norm_ffn, 'norm_final': norm_final,
         'w_gdn_conv': w_gdn_conv, 'gdn_a_log': gdn_a_log, 'gdn_dt_bias': gdn_dt_bias,
         'gdn_norm': gdn_norm, 'w_ffn_dw': w_ffn_dw, 'b_ffn_dw': b_ffn_dw}
    cdim = (2 * GDN_K_HEADS + GDN_V_HEADS) * GDN_D
    vd = GDN_V_HEADS * GDN_D
    wb = {'ret_in': _layer_bf16(w_ret_in, 0), 'ret_out': _layer_bf16(w_ret_out, 0),
          'gdn_in': _layer_bf16(w_gdn_in, 0), 'gdn_out': _layer_bf16(w_gdn_out, 0),
          'gdn_tail': w_gdn_in[0][:, cdim + vd:],
          'ffn_up': [_layer_bf16(w_ffn_up, l) for l in range(2)],
          'ffn_down': [_layer_bf16(w_ffn_down, l) for l in range(2)]}

    bp, tp = x_prompt.shape[:2]
    bs_ = x_sample.shape[0]
    c_all = jnp.concatenate([c_prompt, c_sample], axis=0)
    mods = _ada_mod(c_all, w_ada, b_ada)
    mod_final = _ada_mod(c_all, w_ada_final[None], b_ada_final[None])[0]

    out_p = _trunk(x_prompt, mods[:, :bp], mod_final[:bp], 0, None, None, None, None, p, wb,
                   _Tiles(bp, tp, 1, min(512, tp)), BF16)
    out_s = _trunk(x_sample, mods[:, bp:], mod_final[bp:], PAST_LEN, state_ret, state_gdn,
                   state_gdn_conv, state_ffn_conv, p, wb,
                   _Tiles(bs_, SUBLANES, min(32, bs_), SUBLANES), BF16)
    y_p, ret_p, gdn_p, gconv_p, fconv_p = out_p
    y_s, ret_s, gdn_s, gconv_s, fconv_s = out_s
    return (y_p, y_s, ret_p, gdn_p, gconv_p, fconv_p, ret_s, gdn_s, gconv_s, fconv_s)
```

```python
import functools
import math

import numpy as np
import jax
import jax.numpy as jnp
from jax import lax
from jax.experimental import pallas as pl
from jax.experimental.pallas import tpu as pltpu

F32 = jnp.float32
BF16 = jnp.bfloat16
EPS = 1e-6
ROPE_BASE = 10000.0
PAST_LEN = 16384

RET_HEADS = 4
RET_DK = 256
RET_DV = 512
RET_CHUNK = 256
GDN_K_HEADS = 8
GDN_V_HEADS = 16
GDN_D = 128
GDN_CHUNK = 64
GDN_CONV_W = 4
FFN_CONV_W = 3
SUBLANES = 8
LANES = 128
INV_BASE = 8
GROUP_ROWS = 128
GDN_SEQS_PER_STEP = 4
RET_SEQS_PER_STEP = 2
PROJ_COLS = 512
FFN_COLS = 256
VMEM_LIMIT = 48 * 1024 * 1024

_HIGHEST = lax.Precision.HIGHEST


def _cparams(*sem):
    return pltpu.CompilerParams(dimension_semantics=sem, vmem_limit_bytes=VMEM_LIMIT)


def _silu_of_half(hx):
    return hx + hx * jnp.tanh(hx)


def _silu(x):
    return _silu_of_half(0.5 * x)


def _softplus(x):
    return jnp.maximum(x, 0.0) + jnp.log1p(jnp.exp(-jnp.abs(x)))


def _ada_norm(x, gamma, shift, scale):
    ms = jnp.mean(x * x, axis=-1, keepdims=True)
    xn = x * lax.rsqrt(ms + EPS) * gamma
    return xn * (1.0 + scale) + shift


def _dot(a, b):
    return jnp.dot(a.astype(BF16), b.astype(BF16), preferred_element_type=F32)


def _dot_nt(a, b):
    return lax.dot_general(a.astype(BF16), b.astype(BF16), (((1,), (1,)), ((), ())),
                           preferred_element_type=F32)


def _dot_tn(a, b):
    return lax.dot_general(a.astype(BF16), b.astype(BF16), (((0,), (0,)), ((), ())),
                           preferred_element_type=F32)


def _cast_kernel(w_ref, o_ref, *, transpose):
    w = w_ref[...]
    o_ref[...] = (w.T if transpose else w).astype(o_ref.dtype)


def _layer_bf16(w, layer, rows=256, transposed_cols=None):
    if transposed_cols is None:
        _, K, N = w.shape
        grid, in_block, out_block = K // rows, (None, rows, N), (rows, N)
        in_map, out_map = (lambda i: (layer, i, 0)), (lambda i: (i, 0))
    else:
        K, N = w.shape[2], transposed_cols
        grid, in_block, out_block = N // rows, (None, rows, K), (K, rows)
        in_map, out_map = (lambda i: (layer, i, 0)), (lambda i: (0, i))
    return pl.pallas_call(
        functools.partial(_cast_kernel, transpose=transposed_cols is not None),
        grid=(grid,),
        in_specs=[pl.BlockSpec(in_block, in_map)],
        out_specs=pl.BlockSpec(out_block, out_map),
        out_shape=jax.ShapeDtypeStruct((K, N), BF16),
        compiler_params=_cparams("parallel"),
        name="cast_bf16",
    )(w)


def _ada_kernel(c_ref, w_ref, b_ref, o_ref):
    cs = _silu(c_ref[...])
    o_ref[...] = _dot(cs, w_ref[...]) + b_ref[...]


def _ada_mod(c, w, b, tn=1024):
    L, D, N = w.shape
    Mc = c.shape[0]
    return pl.pallas_call(
        _ada_kernel,
        grid=(L, N // tn),
        in_specs=[pl.BlockSpec((Mc, D), lambda l, j: (0, 0)),
                  pl.BlockSpec((None, D, tn), lambda l, j: (l, 0, j)),
                  pl.BlockSpec((None, 1, tn), lambda l, j: (l, 0, j))],
        out_specs=pl.BlockSpec((None, Mc, tn), lambda l, j: (l, 0, j)),
        out_shape=jax.ShapeDtypeStruct((L, Mc, N), F32),
        compiler_params=_cparams("parallel", "parallel"),
        name="ada_mod",
    )(c, w, b.reshape(L, 1, N))


class _Tiles:
    def __init__(self, B, Tp, bt, tt):
        assert bt == 1 or tt == Tp
        self.B, self.Tp, self.bt, self.tt = B, Tp, bt, tt
        self.tps = Tp // tt
        self.n = (B // bt) * self.tps
        self.tm = bt * tt

    def x_spec(self, D):
        tps = self.tps
        return pl.BlockSpec((self.bt, self.tt, D), lambda i: (i // tps, i % tps, 0))

    def seq_spec(self, r, D):
        tps = self.tps
        return pl.BlockSpec((self.bt, r, D), lambda i: (i // tps, 0, 0))

    def rows_spec(self, N):
        return pl.BlockSpec((self.tm, N), lambda i: (i, 0))

    def pos_spec(self, N):
        tps = self.tps
        return pl.BlockSpec((self.tt, N), lambda i: (i % tps, 0))

    def tail_rows_spec(self, r, N):
        return pl.BlockSpec((self.bt, r, N), lambda i: (i, 0, 0))

    def last_tile(self, a):
        return a[self.tps - 1::self.tps]


def _resident(shape):
    return pl.BlockSpec(shape, lambda i: (0,) * len(shape), pipeline_mode=pl.Buffered(1))


def _mod(m3, k, D):
    return m3[:, :, k * D:(k + 1) * D]


def _normed_rows(x_ref, gam_ref, mod_ref, k_shift, h_ref):
    D = x_ref.shape[-1]
    m3 = mod_ref[...]
    h3 = _ada_norm(x_ref[...], gam_ref[...], _mod(m3, k_shift, D), _mod(m3, k_shift + 1, D))
    h_ref[...] = h3.reshape(h_ref.shape).astype(BF16)
    return h3


def _causal_taps(g3, prev_ref, sl, width):
    ext = jnp.concatenate([prev_ref[:, :, sl], g3], axis=1)
    return [pltpu.roll(ext, k, axis=1)[:, SUBLANES:, :] for k in range(width - 1, 0, -1)]


def _start_of_sequence(tiles):
    return (pl.program_id(0) % tiles.tps) == 0


def _ret_proj_kernel(x_ref, gam_ref, mod_ref, w_ref, cos_ref, sin_ref, o_ref, h_ref):
    bt, tt, _ = x_ref.shape
    _normed_rows(x_ref, gam_ref, mod_ref, 0, h_ref)
    cos, sin = cos_ref[...], sin_ref[...]
    qk = RET_HEADS * RET_DK
    half = RET_DK // 2
    for j in range(w_ref.shape[1] // PROJ_COLS):
        sl = slice(j * PROJ_COLS, (j + 1) * PROJ_COLS)
        y = jnp.dot(h_ref[...], w_ref[:, sl], preferred_element_type=F32)
        if sl.start < 2 * qk:
            y3 = y.reshape(bt, tt, PROJ_COLS)
            parts = []
            for c in range(PROJ_COLS // RET_DK):
                x1 = y3[:, :, c * RET_DK:c * RET_DK + half]
                x2 = y3[:, :, c * RET_DK + half:(c + 1) * RET_DK]
                parts += [x1 * cos - x2 * sin, x1 * sin + x2 * cos]
            y = jnp.concatenate(parts, axis=-1).reshape(bt * tt, PROJ_COLS)
            if sl.start >= qk:
                y = y * (RET_DK ** -0.5)
        o_ref[:, sl] = y.astype(o_ref.dtype)


def _ret_proj(tiles, x3, gamma, mod3, w, cos, sin, out_dtype):
    B, Tp, D = x3.shape
    N = w.shape[1]
    return pl.pallas_call(
        _ret_proj_kernel,
        grid=(tiles.n,),
        in_specs=[tiles.x_spec(D), _resident((1, D)), tiles.seq_spec(1, mod3.shape[-1]),
                  _resident((D, N)), tiles.pos_spec(cos.shape[1]), tiles.pos_spec(sin.shape[1])],
        out_specs=tiles.rows_spec(N),
        out_shape=jax.ShapeDtypeStruct((B * Tp, N), out_dtype),
        scratch_shapes=[pltpu.VMEM((tiles.tm, D), BF16)],
        compiler_params=_cparams("parallel"),
        name="ret_proj",
    )(x3, gamma.reshape(1, D), mod3, w, cos, sin)


def _gdn_proj_kernel(*refs, tiles, c_real, has_state):
    it = iter(refs)
    x_ref, gam_ref, mod_ref, w_ref, wtt_ref, wc_ref = (next(it) for _ in range(6))
    if has_state:
        cs_ref = next(it)
    o_ref, ott_ref, cst_ref = next(it), next(it), next(it)
    h_ref, prev_ref = next(it), next(it)
    bt, tt, _ = x_ref.shape
    key = GDN_K_HEADS * GDN_D
    cdim = wc_ref.shape[1]
    keep = GDN_CONV_W - 1

    h3 = _normed_rows(x_ref, gam_ref, mod_ref, 0, h_ref)
    nt = ott_ref.shape[0]
    h_hi = h_ref[...]
    h_lo = (h3.reshape(bt * tt, -1) - h_hi.astype(F32)).astype(BF16)
    w_hl = wtt_ref[...]
    a = _dot_nt(w_hl, h_hi)
    ott_ref[...] = a[:nt] + a[nt:] + _dot_nt(w_hl[:nt], h_lo)
    if has_state:
        prev_ref[...] = jnp.zeros_like(prev_ref)
        prev_ref[:, SUBLANES - keep:, :] = cs_ref[...]
    else:
        @pl.when(_start_of_sequence(tiles))
        def _():
            prev_ref[:, 0:SUBLANES, :] = jnp.zeros((prev_ref.shape[0], SUBLANES, LANES), F32)

    def head_norm(a, col):
        if col >= 2 * key:
            return a
        inv = lax.rsqrt(jnp.sum(a * a, axis=-1, keepdims=True) + EPS)
        return a * (inv * (GDN_D ** -0.5) if col < key else inv)

    def conv_by_phase(col):
        slab = col // LANES
        n8 = tt // SUBLANES
        cur =[prev_ref[slab, pl.ds(SUBLANES + b, n8, stride=SUBLANES), :] for b in range(SUBLANES)]
        old = {b: prev_ref[slab, pl.ds(b, n8, stride=SUBLANES), :]
               for b in range(SUBLANES - keep, SUBLANES)}
        wch = 0.5 * wc_ref[:, col:col + LANES]
        outs = []
        for b in range(SUBLANES):
            acc = wch[keep:keep + 1, :] * cur[b]
            for k in range(1, keep + 1):
                src = cur[b - k] if b >= k else old[b - k + SUBLANES]
                acc = acc + wch[keep - k:keep - k + 1, :] * src
            outs.append(head_norm(_silu_of_half(acc), col))
        cst_ref[0, :, col:col + LANES] = prev_ref[slab, SUBLANES + c_real - keep:SUBLANES + c_real, :]
        prev_ref[slab, 0:SUBLANES, :] = prev_ref[slab, tt:tt + SUBLANES, :]
        for b in range(SUBLANES):
            prev_ref[slab, pl.ds(SUBLANES + b, n8, stride=SUBLANES), :] = outs[b]
        o_ref[:, col:col + LANES] = prev_ref[slab, SUBLANES:, :].astype(o_ref.dtype)

    for j in range(o_ref.shape[1] // PROJ_COLS):
        sl = slice(j * PROJ_COLS, (j + 1) * PROJ_COLS)
        y = jnp.dot(h_ref[...], w_ref[:, sl], preferred_element_type=F32)
        if sl.start < cdim and not has_state:
            for c in range(0, PROJ_COLS, LANES):
                prev_ref[(sl.start + c) // LANES, SUBLANES:, :] = y[:, c:c + LANES]
            for c in range(0, PROJ_COLS, LANES):
                conv_by_phase(sl.start + c)
            continue
        if sl.start < cdim:
            y3 = y.reshape(bt, tt, PROJ_COLS)
            taps = _causal_taps(y3, prev_ref, sl, GDN_CONV_W)
            wch = 0.5 * wc_ref[:, sl]
            acc = wch[keep:keep + 1, :] * y3
            for i, tap in enumerate(taps):
                acc = acc + wch[i:i + 1, :] * tap
            acc = _silu_of_half(acc)
            acc = jnp.concatenate([head_norm(acc[:, :, c:c + GDN_D], sl.start + c)
                                   for c in range(0, PROJ_COLS, GDN_D)], axis=-1)
            cst_ref[:, :, sl] = y3[:, c_real - keep:c_real, :]
            y = acc.reshape(bt * tt, PROJ_COLS)
        o_ref[:, sl] = y.astype(o_ref.dtype)


def _gdn_proj(tiles, x3, gamma, mod3, w, w_tail_t, w_conv, conv_state, c_real, out_dtype):
    B, Tp, D = x3.shape
    nt = w_tail_t.shape[0]
    cdim = w_conv.shape[1]
    N = w.shape[1]
    keep = GDN_CONV_W - 1
    has_state = conv_state is not None
    wt_hi = w_tail_t.astype(BF16)
    wt_lo = (w_tail_t - wt_hi.astype(F32)).astype(BF16)
    in_specs = [tiles.x_spec(D), _resident((1, D)), tiles.seq_spec(1, mod3.shape[-1]),
                _resident(w.shape), _resident((2 * nt, D)), _resident((GDN_CONV_W, cdim))]
    args = [x3, gamma.reshape(1, D), mod3, w, jnp.concatenate([wt_hi, wt_lo], axis=0), w_conv]
    if has_state:
        in_specs.append(tiles.seq_spec(keep, cdim))
        args.append(conv_state)
    proj, ott, cst = pl.pallas_call(
        functools.partial(_gdn_proj_kernel, tiles=tiles, c_real=c_real, has_state=has_state),
        grid=(tiles.n,),
        in_specs=in_specs,
        out_specs=[tiles.rows_spec(N), pl.BlockSpec((nt, tiles.tm), lambda i: (0, i)),
                   tiles.tail_rows_spec(keep, cdim)],
        out_shape=[jax.ShapeDtypeStruct((B * Tp, N), out_dtype),
                   jax.ShapeDtypeStruct((nt, B * Tp), F32),
                   jax.ShapeDtypeStruct((tiles.n * tiles.bt, keep, cdim), F32)],
        scratch_shapes=[pltpu.VMEM((tiles.tm, D), BF16),
                        pltpu.VMEM((tiles.bt, SUBLANES, cdim) if has_state
                                   else (cdim // LANES, SUBLANES + tiles.tt, LANES), F32)],
        compiler_params=_cparams("arbitrary"),
        name="gdn_proj",
    )(*args)
    return proj, ott, tiles.last_tile(cst)


def _ffn_kernel(*refs, tiles, c_real, has_state, final):
    it = iter(refs)
    o_ref, wo_ref, x_ref, gam_ref, mod_ref = (next(it) for _ in range(5))
    wup_ref, wdw_ref, bdw_ref, wd_ref = (next(it) for _ in range(4))
    if has_state:
        cs_ref = next(it)
    if final:
        gamf_ref, modf_ref = next(it), next(it)
    y_ref, cst_ref = next(it), next(it)
    xm_ref, h_ref, act_ref, prev_ref = (next(it) for _ in range(4))
    bt, tt, D = x_ref.shape
    fd = wd_ref.shape[0]
    keep = FFN_CONV_W - 1

    xm_ref[...] = x_ref[...] + _mod(mod_ref[...], 0, D) * _dot(o_ref[...], wo_ref[...]).reshape(bt, tt, D)
    _normed_rows(xm_ref, gam_ref, mod_ref, 1, h_ref)
    if has_state:
        prev_ref[...] = jnp.zeros_like(prev_ref)
        prev_ref[:, SUBLANES - keep:, :] = cs_ref[...]
    else:
        @pl.when(_start_of_sequence(tiles))
        def _():
            prev_ref[...] = jnp.zeros_like(prev_ref)

    for f in range(fd // FFN_COLS):
        sl = slice(f * FFN_COLS, (f + 1) * FFN_COLS)
        h = h_ref[...]
        g3 = jnp.dot(h, wup_ref[:, sl], preferred_element_type=F32).reshape(bt, tt, FFN_COLS)
        val = jnp.dot(h, wup_ref[:, fd + sl.start:fd + sl.stop], preferred_element_type=F32)
        s2, s1 = _causal_taps(g3, prev_ref, sl, FFN_CONV_W)
        wh = 0.5 * wdw_ref[:, sl]
        conv = wh[0:1, :] * s2 + wh[1:2, :] * s1 + wh[2:3, :] * g3 + 0.5 * bdw_ref[:, sl]
        cst_ref[:, :, sl] = g3[:, c_real - keep:c_real, :]
        if not has_state:
            prev_ref[:, :, sl] = g3[:, tt - SUBLANES:, :]
        act_ref[:, sl] = (_silu_of_half(conv).reshape(bt * tt, FFN_COLS) * val).astype(BF16)

    acc = jnp.dot(act_ref[...], wd_ref[...], preferred_element_type=F32).reshape(bt, tt, D)
    xn = xm_ref[...] + _mod(mod_ref[...], 3, D) * acc
    if final:
        mf = modf_ref[...]
        xn = _ada_norm(xn, gamf_ref[...], _mod(mf, 0, D), _mod(mf, 1, D))
    y_ref[...] = xn


def _ffn(tiles, o, w_out, x3, gamma, mod3, w_up, w_dw, b_dw, w_down, conv_state, c_real, final=None):
    B, Tp, D = x3.shape
    Fd = w_down.shape[0]
    K = o.shape[1]
    keep = FFN_CONV_W - 1
    has_state = conv_state is not None
    in_specs = [tiles.rows_spec(K), _resident((K, D)),
                tiles.x_spec(D), _resident((1, D)), tiles.seq_spec(1, mod3.shape[-1]),
                _resident((D, 2 * Fd)), _resident((FFN_CONV_W, Fd)), _resident((1, Fd)),
                _resident((Fd, D))]
    args = [o, w_out, x3, gamma.reshape(1, D), mod3, w_up, w_dw, b_dw.reshape(1, Fd), w_down]
    if has_state:
        in_specs.append(tiles.seq_spec(keep, Fd))
        args.append(conv_state)
    if final is not None:
        in_specs += [_resident((1, D)), tiles.seq_spec(1, final[1].shape[-1])]
        args += [final[0].reshape(1, D), final[1]]
    y3, cst = pl.pallas_call(
        functools.partial(_ffn_kernel, tiles=tiles, c_real=c_real, has_state=has_state,
                          final=final is not None),
        grid=(tiles.n,),
        in_specs=in_specs,
        out_specs=[tiles.x_spec(D), tiles.tail_rows_spec(keep, Fd)],
        out_shape=[jax.ShapeDtypeStruct((B, Tp, D), F32),
                   jax.ShapeDtypeStruct((tiles.n * tiles.bt, keep, Fd), F32)],
        scratch_shapes=[pltpu.VMEM((tiles.bt, tiles.tt, D), F32), pltpu.VMEM((tiles.tm, D), BF16),
                        pltpu.VMEM((tiles.tm, Fd), BF16), pltpu.VMEM((tiles.bt, SUBLANES, Fd), F32)],
        compiler_params=_cparams("arbitrary"),
        name="out_proj_ffn",
    )(*args)
    return y3, tiles.last_tile(cst)


def _ret_core_kernel(*refs, c_real, has_state):
    it = iter(refs)
    q_ref, k_ref, v_ref, g_ref = (next(it) for _ in range(4))
    if has_state:
        s0_ref = next(it)
    o_ref, s_ref = next(it), next(it)
    nb, cp, _ = q_ref.shape

    @pl.when(pl.program_id(1) == 0)
    def _():
        if has_state:
            s_ref[...] = s0_ref[...]
        else:
            s_ref[...] = jnp.zeros_like(s_ref)

    ri = lax.broadcasted_iota(jnp.int32, (cp, cp), 0)
    ci = lax.broadcasted_iota(jnp.int32, (cp, cp), 1)
    causal = ri >= ci
    diff = jnp.where(causal, ri - ci, 0).astype(F32)
    rowi = lax.broadcasted_iota(jnp.int32, (cp, 1), 0)
    row = rowi.astype(F32)
    lgs = [math.log(1.0 - 2.0 ** (-5.0 - h)) for h in range(RET_HEADS)]
    decays = [jnp.where(causal, jnp.exp(diff * lg), 0.0) for lg in lgs]
    k_decays = [jnp.where(rowi < c_real, jnp.exp((c_real - 1.0 - row) * lg), 0.0) for lg in lgs]
    q_decays = [jnp.exp((row + 1.0) * lg) for lg in lgs]

    items = [(i, h) for i in range(nb) for h in range(RET_HEADS)]
    qs = [q_ref[i, :, h * RET_DK:(h + 1) * RET_DK] for i, h in items]
    ks = [k_ref[i, :, h * RET_DK:(h + 1) * RET_DK].astype(F32) for i, h in items]
    vs = [v_ref[i, :, h * RET_DV:(h + 1) * RET_DV] for i, h in items]
    ss = [s_ref[i, h] for i, h in items]
    scores = [_dot_nt(q, k) for q, k in zip(qs, ks)]
    cross = [_dot(q, s) for q, s in zip(qs, ss)]
    upds = [_dot_tn(k * k_decays[h], v) for k, v, (i, h) in zip(ks, vs, items)]
    inner = [_dot(sc * decays[h], v) for sc, v, (i, h) in zip(scores, vs, items)]
    for m, (i, h) in enumerate(items):
        s_ref[i, h] = ss[m] * math.exp(c_real * lgs[h]) + upds[m]
        o = inner[m] + cross[m] * q_decays[h]
        o = o * lax.rsqrt(jnp.mean(o * o, axis=-1, keepdims=True) + EPS)
        gate = _silu(g_ref[i, :, h * RET_DV:(h + 1) * RET_DV].astype(F32))
        o_ref[i, :, h * RET_DV:(h + 1) * RET_DV] = (o * gate).astype(o_ref.dtype)


def _ret_core(proj3, s0, c_real, cp):
    B, Tp, _ = proj3.shape
    qk = RET_HEADS * RET_DK
    vd = RET_HEADS * RET_DV
    has_state = s0 is not None
    nb = min(B, RET_SEQS_PER_STEP)
    in_specs = [pl.BlockSpec((nb, cp, qk), lambda b, n: (b, n, 0)),
                pl.BlockSpec((nb, cp, qk), lambda b, n: (b, n, 1)),
                pl.BlockSpec((nb, cp, vd), lambda b, n: (b, n, 1)),
                pl.BlockSpec((nb, cp, vd), lambda b, n: (b, n, 2))]
    args = [proj3, proj3, proj3, proj3]
    s_spec = pl.BlockSpec((nb, RET_HEADS, RET_DK, RET_DV), lambda b, n: (b, 0, 0, 0))
    if has_state:
        in_specs.append(s_spec)
        args.append(s0)
    return pl.pallas_call(
        functools.partial(_ret_core_kernel, c_real=c_real, has_state=has_state),
        grid=(B // nb, Tp // cp),
        in_specs=in_specs,
        out_specs=[pl.BlockSpec((nb, cp, vd), lambda b, n: (b, n, 0)), s_spec],
        out_shape=[jax.ShapeDtypeStruct((B, Tp, vd), proj3.dtype),
                   jax.ShapeDtypeStruct((B, RET_HEADS, RET_DK, RET_DV), F32)],
        compiler_params=_cparams("parallel", "arbitrary"),
        name="ret_core",
    )(*args)


def _block_inverse_many(ls, ri, ci, bs):
    eye = (ri == ci).astype(F32)
    base = (ri // INV_BASE) == (ci // INV_BASE)
    pws = [-jnp.where(base, l, 0.0) for l in ls]
    ps = [eye + m for m in pws]
    span = 2
    while span < INV_BASE:
        pws = [_dot(pw, pw) for pw in pws]
        ps = [p + _dot(p, pw) for p, pw in zip(ps, pws)]
        span *= 2
    size = INV_BASE
    while size < bs:
        off = ((ri // (2 * size)) == (ci // (2 * size))) & ((ri // size) != (ci // size))
        xs = [_dot(jnp.where(off, l, 0.0), p) for l, p in zip(ls, ps)]
        ps = [p - _dot(p, x) for p, x in zip(ps, xs)]
        size *= 2
    return ps


def _gdn_core_kernel(*refs, c_real, has_state):
    it = iter(refs)
    qkv_ref, z_ref, gt_ref, al_ref, dt_ref, nw_ref = (next(it) for _ in range(6))
    if has_state:
        s0_ref = next(it)
    o_ref, s_ref = next(it), next(it)
    bb, cp, _ = qkv_ref.shape
    bs = cp
    hg = GROUP_ROWS // bs
    ng = GDN_V_HEADS // hg
    rep = GDN_V_HEADS // GDN_K_HEADS
    key = GDN_K_HEADS * GDN_D
    n_items = bb * ng
    assert n_items % SUBLANES == 0

    @pl.when(pl.program_id(1) == 0)
    def _():
        if has_state:
            s_ref[...] = s0_ref[...]
        else:
            s_ref[...] = jnp.zeros_like(s_ref)

    ri = lax.broadcasted_iota(jnp.int32, (GROUP_ROWS, GROUP_ROWS), 0)
    ci = lax.broadcasted_iota(jnp.int32, (GROUP_ROWS, GROUP_ROWS), 1)
    same = (ri // bs) == (ci // bs)
    incl = same & (ri >= ci)
    strict = same & (ri > ci)
    eye = (ri == ci).astype(F32)

    live = (lax.broadcasted_iota(jnp.int32, (1, GROUP_ROWS), 1) % bs) < c_real
    beta_rows = jnp.where(live, jax.nn.sigmoid(gt_ref[0]), 0.0)
    g_rows = jnp.where(live, -jnp.exp(al_ref[...]) * _softplus(gt_ref[1] + dt_ref[...]), 0.0)
    gsum_rows = jnp.dot(g_rows, (same & (ri <= ci)).astype(F32), precision=_HIGHEST,
                        preferred_element_type=F32)
    cols = jnp.concatenate([beta_rows, gsum_rows,
                            jnp.zeros((GROUP_ROWS - 2 * n_items, GROUP_ROWS), F32)], axis=0).T

    items = [(i, j) for i in range(bb) for j in range(ng)]

    def stack(i, j, col_of_head):
        return jnp.concatenate(
            [qkv_ref[i, :, col_of_head(j * hg + hh):col_of_head(j * hg + hh) + GDN_D].astype(F32)
             for hh in range(hg)], axis=0)

    qxs = [stack(i, j, lambda h: (h // rep) * GDN_D) for i, j in items]
    kxs = [stack(i, j, lambda h: key + (h // rep) * GDN_D) for i, j in items]
    vxs = [stack(i, j, lambda h: 2 * key + h * GDN_D) for i, j in items]
    bcs = [jnp.broadcast_to(cols[:, m:m + 1], (GROUP_ROWS, GROUP_ROWS)) for m in range(n_items)]
    gcs = [jnp.broadcast_to(cols[:, n_items + m:n_items + m + 1], (GROUP_ROWS, GROUP_ROWS))
           for m in range(n_items)]
    kbs = [kx * bc for kx, bc in zip(kxs, bcs)]
    kks = [_dot_nt(kb, kx) for kb, kx in zip(kbs, kxs)]
    qks = [_dot_nt(qx, kx) for qx, kx in zip(qxs, kxs)]
    decays = [jnp.exp(jnp.where(incl, gc - gsum_rows[m:m + 1, :], -jnp.inf))
              for m, gc in enumerate(gcs)]
    ls = [jnp.where(strict, kk * d, 0.0) for kk, d in zip(kks, decays)]
    attns = [qk * d for qk, d in zip(qks, decays)]
    ps = _block_inverse_many(ls, ri, ci, bs)
    egs = [jnp.exp(gc) for gc in gcs]
    rhss = [jnp.concatenate([vx * bc, kb * eg], axis=-1) for vx, bc, kb, eg in zip(vxs, bcs, kbs, egs)]
    sols = [rhs + _dot(p - eye, rhs) for rhs, p in zip(rhss, ps)]
    qes = [qx * eg for qx, eg in zip(qxs, egs)]

    heads = [(m, i, j * hg + hh, slice(hh * bs, (hh + 1) * bs))
             for m, (i, j) in enumerate(items) for hh in range(hg)]
    wqs = [_dot(jnp.concatenate([sols[m][rs, GDN_D:], qes[m][rs, :]], axis=0), s_ref[i, h])
           for m, i, h, rs in heads]
    v_news, qss = [], []
    for m in range(n_items):
        part = wqs[m * hg:(m + 1) * hg]
        v_news.append(sols[m][:, :GDN_D] - jnp.concatenate([r[:bs] for r in part], axis=0))
        qss.append(jnp.concatenate([r[bs:] for r in part], axis=0))
    outs = [qs + _dot(attn, vn) for qs, attn, vn in zip(qss, attns, v_news)]

    def last_rows(gc):
        return [gc[hh * bs + c_real - 1:hh * bs + c_real, :] for hh in range(hg)]
    kds = [kx * jnp.exp(jnp.concatenate([jnp.broadcast_to(r, (bs, GROUP_ROWS)) for r in last_rows(gc)],
                                        axis=0) - gc)
           for kx, gc in zip(kxs, gcs)]
    upds = [_dot_tn(kds[m][rs, :], v_news[m][rs, :]) for m, i, h, rs in heads]
    for (m, i, h, rs), upd in zip(heads, upds):
        g_last = gcs[m][rs.start + c_real - 1:rs.start + c_real, :]
        s_ref[i, h] = s_ref[i, h] * jnp.exp(g_last) + upd

    nw = nw_ref[...]
    for m, i, h, rs in heads:
        o = outs[m][rs, :]
        o = o * lax.rsqrt(jnp.mean(o * o, axis=-1, keepdims=True) + EPS) * nw
        gate = _silu(z_ref[i, :, h * GDN_D:(h + 1) * GDN_D].astype(F32))
        o_ref[i, :, h * GDN_D:(h + 1) * GDN_D] = (o * gate).astype(o_ref.dtype)


def _gdn_core(proj3, gates, a_log, dt_bias, norm_w, s0, c_real, cp, bb):
    B, Tp, _ = proj3.shape
    key = GDN_K_HEADS * GDN_D
    vd = GDN_V_HEADS * GDN_D
    cdim = 2 * key + vd
    nc = Tp // cp
    hg = GROUP_ROWS // cp
    ng = GDN_V_HEADS // hg
    has_state = s0 is not None
    lanes = lambda a: jnp.tile(jnp.repeat(a.reshape(ng, hg), cp, axis=1), (bb, 1))
    full = lambda shape: pl.BlockSpec(shape, lambda b, n: (0,) * len(shape))
    in_specs = [pl.BlockSpec((bb, cp, cdim), lambda b, n: (b, n, 0)),
                pl.BlockSpec((bb, cp, vd), lambda b, n: (b, n, cdim // vd)),
                pl.BlockSpec((None, 2, bb * ng, GROUP_ROWS), lambda b, n: (b * nc + n, 0, 0, 0)),
                full((bb * ng, GROUP_ROWS)), full((bb * ng, GROUP_ROWS)), full((1, GDN_D))]
    args = [proj3, proj3, gates, lanes(a_log), lanes(dt_bias), norm_w.reshape(1, GDN_D)]
    s_spec = pl.BlockSpec((bb, GDN_V_HEADS, GDN_D, GDN_D), lambda b, n: (b, 0, 0, 0))
    if has_state:
        in_specs.append(s_spec)
        args.append(s0)
    return pl.pallas_call(
        functools.partial(_gdn_core_kernel, c_real=c_real, has_state=has_state),
        grid=(B // bb, nc),
        in_specs=in_specs,
        out_specs=[pl.BlockSpec((bb, cp, vd), lambda b, n: (b, n, 0)), s_spec],
        out_shape=[jax.ShapeDtypeStruct((B, Tp, vd), proj3.dtype),
                   jax.ShapeDtypeStruct((B, GDN_V_HEADS, GDN_D, GDN_D), F32)],
        compiler_params=_cparams("parallel", "arbitrary"),
        name="gdn_core",
    )(*args)


def _rope_tables(pos0, t_real, t_pad):
    half = RET_DK // 2
    inv_freq = ROPE_BASE ** (-np.arange(half, dtype=np.float64) / half)
    pos = pos0 + np.arange(t_pad, dtype=np.float64)
    ang = pos[:, None] * inv_freq[None, :]
    live = (np.arange(t_pad) < t_real)[:, None]
    return (jnp.asarray(np.where(live, np.cos(ang), 0.0), F32),
            jnp.asarray(np.where(live, np.sin(ang), 0.0), F32))


def _trunk(x, mods, mod_final, pos0, s_ret, s_gdn, s_gconv, s_fconv, p, wb, tiles, act_dtype):
    B, T, D = x.shape
    Tp = tiles.Tp
    x3 = x if T == Tp else jnp.pad(x, ((0, 0), (0, Tp - T), (0, 0)))
    M = B * Tp
    stateful = s_ret is not None
    new_fconv = []

    mod_mix, mod_ffn = mods[0][:, None, :2 * D], mods[0][:, None, 2 * D:]
    cos, sin = _rope_tables(pos0, T, Tp)
    proj = _ret_proj(tiles, x3, p['norm_mix'][0], mod_mix, wb['ret_in'], cos, sin, act_dtype)
    cp = min(RET_CHUNK, Tp)
    o3, new_ret = _ret_core(proj.reshape(B, Tp, -1), s_ret[0] if stateful else None, min(T, cp), cp)
    x3, cst = _ffn(tiles, o3.reshape(M, -1), wb['ret_out'], x3, p['norm_ffn'][0], mod_ffn,
                   wb['ffn_up'][0], p['w_ffn_dw'][0], p['b_ffn_dw'][0], wb['ffn_down'][0],
                   s_fconv[0] if stateful else None, min(T, tiles.tt))
    new_fconv.append(cst)

    mod_mix, mod_ffn = mods[1][:, None, :2 * D], mods[1][:, None, 2 * D:]
    cp = min(GDN_CHUNK, Tp)
    nc = Tp // cp
    ng = GDN_V_HEADS * cp // GROUP_ROWS
    bb = min(B, max(SUBLANES // ng, GDN_SEQS_PER_STEP))
    proj, bat, new_gconv = _gdn_proj(tiles, x3, p['norm_mix'][1], mod_mix, wb['gdn_in'], wb['gdn_tail_t'],
                                     p['w_gdn_conv'][0], s_gconv[0] if stateful else None,
                                     min(T, tiles.tt), act_dtype)
    gates = bat.reshape(2, ng, GROUP_ROWS // cp, B // bb, bb, nc, cp)
    gates = gates.transpose(3, 5, 0, 4, 1, 2, 6).reshape(B // bb * nc, 2, bb * ng, GROUP_ROWS)
    o3, new_gdn = _gdn_core(proj.reshape(B, Tp, -1), gates, p['gdn_a_log'][0], p['gdn_dt_bias'][0],
                            p['gdn_norm'][0], s_gdn[0] if stateful else None, min(T, cp), cp, bb)
    y3, cst = _ffn(tiles, o3.reshape(M, -1), wb['gdn_out'], x3, p['norm_ffn'][1], mod_ffn,
                   wb['ffn_up'][1], p['w_ffn_dw'][1], p['b_ffn_dw'][1], wb['ffn_down'][1],
                   s_fconv[1] if stateful else None, min(T, tiles.tt),
                   final=(p['norm_final'], mod_final[:, None, :]))
    new_fconv.append(cst)
    return (y3[:, :T], new_ret[None], new_gdn[None], new_gconv[None], jnp.stack(new_fconv))


def kernel(x_prompt, x_sample, state_ret, state_gdn, state_gdn_conv, state_ffn_conv, c_prompt, c_sample, w_ada, b_ada, w_ada_final, b_ada_final, norm_mix, norm_ffn, norm_final, w_ret_in, w_ret_out, w_gdn_in, w_gdn_conv, gdn_a_log, gdn_dt_bias, gdn_norm, w_gdn_out, w_ffn_up, w_ffn_dw, b_ffn_dw, w_ffn_down):
    p = {'norm_mix': norm_mix, 'norm_ffn': norm_ffn, 'norm_final': norm_final,
         'w_gdn_conv': w_gdn_conv, 'gdn_a_log': gdn_a_log, 'gdn_dt_bias': gdn_dt_bias,
         'gdn_norm': gdn_norm, 'w_ffn_dw': w_ffn_dw, 'b_ffn_dw': b_ffn_dw}
    cdim = (2 * GDN_K_HEADS + GDN_V_HEADS) * GDN_D
    vd = GDN_V_HEADS * GDN_D
    w_gdn_in_t = jnp.swapaxes(w_gdn_in, 1, 2)
    wb = {'ret_in': _layer_bf16(w_ret_in, 0), 'ret_out': _layer_bf16(w_ret_out, 0),
          'gdn_in': _layer_bf16(w_gdn_in_t, 0, transposed_cols=cdim + vd),
          'gdn_out': _layer_bf16(w_gdn_out, 0), 'gdn_tail_t': w_gdn_in_t[0, cdim + vd:, :],
          'ffn_up': [_layer_bf16(w_ffn_up, l) for l in range(2)],
          'ffn_down': [_layer_bf16(w_ffn_down, l) for l in range(2)]}

    bp, tp = x_prompt.shape[:2]
    bs_ = x_sample.shape[0]
    c_all = jnp.concatenate([c_prompt, c_sample], axis=0)
    mods = _ada_mod(c_all, w_ada, b_ada)
    mod_final = _ada_mod(c_all, w_ada_final[None], b_ada_final[None])[0]

    out_p = _trunk(x_prompt, mods[:, :bp], mod_final[:bp], 0, None, None, None, None, p, wb,
                   _Tiles(bp, tp, 1, min(512, tp)), BF16)
    out_s = _trunk(x_sample, mods[:, bp:], mod_final[bp:], PAST_LEN, state_ret, state_gdn,
                   state_gdn_conv, state_ffn_conv, p, wb,
                   _Tiles(bs_, SUBLANES, min(32, bs_), SUBLANES), BF16)
    y_p, ret_p, gdn_p, gconv_p, fconv_p = out_p
    y_s, ret_s, gdn_s, gconv_s, fconv_s = out_s
    return (y_p, y_s, ret_p, gdn_p, gconv_p, fconv_p, ret_s, gdn_s, gconv_s, fconv_s)
```

```python
import functools
import math

import numpy as np
import jax
import jax.numpy as jnp
from jax import lax
from jax.experimental import pallas as pl
from jax.experimental.pallas import tpu as pltpu

F32 = jnp.float32
BF16 = jnp.bfloat16
EPS = 1e-6
ROPE_BASE = 10000.0
PAST_LEN = 16384

RET_HEADS = 4
RET_DK = 256
RET_DV = 512
RET_CHUNK = 256
GDN_K_HEADS = 8
GDN_V_HEADS = 16
GDN_D = 128
GDN_CHUNK = 64
GDN_CONV_W = 4
FFN_CONV_W = 3
SUBLANES = 8
LANES = 128
INV_BASE = 8
GROUP_ROWS = 128
GDN_SEQS_PER_STEP = 4
RET_SEQS_PER_STEP = 2
PROJ_COLS = 512
FFN_COLS = 256
VMEM_LIMIT = 48 * 1024 * 1024

_HIGHEST = lax.Precision.HIGHEST


def _cparams(*sem):
    return pltpu.CompilerParams(dimension_semantics=sem, vmem_limit_bytes=VMEM_LIMIT)


def _silu_of_half(hx):
    return hx + hx * jnp.tanh(hx)


def _silu(x):
    return _silu_of_half(0.5 * x)


def _softplus(x):
    return jnp.maximum(x, 0.0) + jnp.log1p(jnp.exp(-jnp.abs(x)))


def _ada_norm(x, gamma, shift, scale):
    ms = jnp.mean(x * x, axis=-1, keepdims=True)
    xn = x * lax.rsqrt(ms + EPS) * gamma
    return xn * (1.0 + scale) + shift


def _dot(a, b):
    return jnp.dot(a.astype(BF16), b.astype(BF16), preferred_element_type=F32)


def _dot_nt(a, b):
    return lax.dot_general(a.astype(BF16), b.astype(BF16), (((1,), (1,)), ((), ())),
                           preferred_element_type=F32)


def _dot_tn(a, b):
    return lax.dot_general(a.astype(BF16), b.astype(BF16), (((0,), (0,)), ((), ())),
                           preferred_element_type=F32)


def _cast_kernel(w_ref, o_ref, *, transpose):
    w = w_ref[...]
    o_ref[...] = (w.T if transpose else w).astype(o_ref.dtype)


def _layer_bf16(w, layer, transposed_cols=None):
    tiled = w.shape[1] if transposed_cols is None else transposed_cols
    rows = 512 if tiled % 512 == 0 else 256
    if transposed_cols is None:
        _, K, N = w.shape
        grid, in_block, out_block = K // rows, (None, rows, N), (rows, N)
        in_map, out_map = (lambda i: (layer, i, 0)), (lambda i: (i, 0))
    else:
        K, N = w.shape[2], transposed_cols
        grid, in_block, out_block = N // rows, (None, rows, K), (K, rows)
        in_map, out_map = (lambda i: (layer, i, 0)), (lambda i: (0, i))
    return pl.pallas_call(
        functools.partial(_cast_kernel, transpose=transposed_cols is not None),
        grid=(grid,),
        in_specs=[pl.BlockSpec(in_block, in_map)],
        out_specs=pl.BlockSpec(out_block, out_map),
        out_shape=jax.ShapeDtypeStruct((K, N), BF16),
        compiler_params=_cparams("parallel"),
        name="cast_bf16",
    )(w)


def _ada_kernel(c_ref, w_ref, b_ref, o_ref):
    cs = _silu(c_ref[...])
    o_ref[...] = _dot(cs, w_ref[...]) + b_ref[...]


def _ada_mod(c, w, b, tn=2048):
    L, D, N = w.shape
    Mc = c.shape[0]
    return pl.pallas_call(
        _ada_kernel,
        grid=(L, N // tn),
        in_specs=[pl.BlockSpec((Mc, D), lambda l, j: (0, 0)),
                  pl.BlockSpec((None, D, tn), lambda l, j: (l, 0, j)),
                  pl.BlockSpec((None, 1, tn), lambda l, j: (l, 0, j))],
        out_specs=pl.BlockSpec((None, Mc, tn), lambda l, j: (l, 0, j)),
        out_shape=jax.ShapeDtypeStruct((L, Mc, N), F32),
        compiler_params=_cparams("parallel", "parallel"),
        name="ada_mod",
    )(c, w, b.reshape(L, 1, N))


class _Tiles:
    def __init__(self, B, Tp, bt, tt):
        assert bt == 1 or tt == Tp
        self.B, self.Tp, self.bt, self.tt = B, Tp, bt, tt
        self.tps = Tp // tt
        self.n = (B // bt) * self.tps
        self.tm = bt * tt

    def x_spec(self, D):
        tps = self.tps
        return pl.BlockSpec((self.bt, self.tt, D), lambda i: (i // tps, i % tps, 0))

    def seq_spec(self, r, D):
        tps = self.tps
        return pl.BlockSpec((self.bt, r, D), lambda i: (i // tps, 0, 0))

    def rows_spec(self, N):
        return pl.BlockSpec((self.tm, N), lambda i: (i, 0))

    def pos_spec(self, N):
        tps = self.tps
        return pl.BlockSpec((self.tt, N), lambda i: (i % tps, 0))

    def tail_rows_spec(self, r, N):
        return pl.BlockSpec((self.bt, r, N), lambda i: (i, 0, 0))

    def last_tile(self, a):
        return a[self.tps - 1::self.tps]


def _resident(shape):
    return pl.BlockSpec(shape, lambda i: (0,) * len(shape), pipeline_mode=pl.Buffered(1))


def _mod(m3, k, D):
    return m3[:, :, k * D:(k + 1) * D]


def _normed_rows(x_ref, gam_ref, mod_ref, k_shift, h_ref):
    D = x_ref.shape[-1]
    m3 = mod_ref[...]
    h3 = _ada_norm(x_ref[...], gam_ref[...], _mod(m3, k_shift, D), _mod(m3, k_shift + 1, D))
    h_ref[...] = h3.reshape(h_ref.shape).astype(BF16)
    return h3


def _causal_taps(g3, prev_ref, sl, width):
    ext = jnp.concatenate([prev_ref[:, :, sl], g3], axis=1)
    return [pltpu.roll(ext, k, axis=1)[:, SUBLANES:, :] for k in range(width - 1, 0, -1)]


def _start_of_sequence(tiles):
    return (pl.program_id(0) % tiles.tps) == 0


def _ret_proj_kernel(x_ref, gam_ref, mod_ref, w_ref, cos_ref, sin_ref, o_ref, h_ref):
    bt, tt, _ = x_ref.shape
    _normed_rows(x_ref, gam_ref, mod_ref, 0, h_ref)
    cos, sin = cos_ref[...], sin_ref[...]
    qk = RET_HEADS * RET_DK
    half = RET_DK // 2
    for j in range(w_ref.shape[1] // PROJ_COLS):
        sl = slice(j * PROJ_COLS, (j + 1) * PROJ_COLS)
        y = jnp.dot(h_ref[...], w_ref[:, sl], preferred_element_type=F32)
        if sl.start < 2 * qk:
            y3 = y.reshape(bt, tt, PROJ_COLS)
            parts = []
            for c in range(PROJ_COLS // RET_DK):
                x1 = y3[:, :, c * RET_DK:c * RET_DK + half]
                x2 = y3[:, :, c * RET_DK + half:(c + 1) * RET_DK]
                parts += [x1 * cos - x2 * sin, x1 * sin + x2 * cos]
            y = jnp.concatenate(parts, axis=-1).reshape(bt * tt, PROJ_COLS)
            if sl.start >= qk:
                y = y * (RET_DK ** -0.5)
        o_ref[:, sl] = y.astype(o_ref.dtype)


def _ret_proj(tiles, x3, gamma, mod3, w, cos, sin, out_dtype):
    B, Tp, D = x3.shape
    N = w.shape[1]
    return pl.pallas_call(
        _ret_proj_kernel,
        grid=(tiles.n,),
        in_specs=[tiles.x_spec(D), _resident((1, D)), tiles.seq_spec(1, mod3.shape[-1]),
                  _resident((D, N)), tiles.pos_spec(cos.shape[1]), tiles.pos_spec(sin.shape[1])],
        out_specs=tiles.rows_spec(N),
        out_shape=jax.ShapeDtypeStruct((B * Tp, N), out_dtype),
        scratch_shapes=[pltpu.VMEM((tiles.tm, D), BF16)],
        compiler_params=_cparams("parallel"),
        name="ret_proj",
    )(x3, gamma.reshape(1, D), mod3, w, cos, sin)


def _gdn_proj_kernel(*refs, tiles, c_real, has_state):
    it = iter(refs)
    x_ref, gam_ref, mod_ref, w_ref, wtt_ref, wc_ref = (next(it) for _ in range(6))
    if has_state:
        cs_ref = next(it)
    o_ref, ott_ref, cst_ref = next(it), next(it), next(it)
    h_ref, prev_ref = next(it), next(it)
    bt, tt, _ = x_ref.shape
    key = GDN_K_HEADS * GDN_D
    cdim = wc_ref.shape[1]
    keep = GDN_CONV_W - 1

    h3 = _normed_rows(x_ref, gam_ref, mod_ref, 0, h_ref)
    nt = ott_ref.shape[0]
    h_hi = h_ref[...]
    h_lo = (h3.reshape(bt * tt, -1) - h_hi.astype(F32)).astype(BF16)
    w_hl = wtt_ref[...]
    a = _dot_nt(w_hl, h_hi)
    ott_ref[...] = a[:nt] + a[nt:] + _dot_nt(w_hl[:nt], h_lo)
    if has_state:
        prev_ref[...] = jnp.zeros_like(prev_ref)
        prev_ref[:, SUBLANES - keep:, :] = cs_ref[...]
    else:
        @pl.when(_start_of_sequence(tiles))
        def _():
            prev_ref[:, 0:SUBLANES, :] = jnp.zeros((prev_ref.shape[0], SUBLANES, LANES), F32)

    def head_norm(a, col):
        if col >= 2 * key:
            return a
        inv = lax.rsqrt(jnp.sum(a * a, axis=-1, keepdims=True) + EPS)
        return a * (inv * (GDN_D ** -0.5) if col < key else inv)

    def conv_by_phase(col):
        slab = col // LANES
        n8 = tt // SUBLANES
        cur =[prev_ref[slab, pl.ds(SUBLANES + b, n8, stride=SUBLANES), :] for b in range(SUBLANES)]
        old = {b: prev_ref[slab, pl.ds(b, n8, stride=SUBLANES), :]
               for b in range(SUBLANES - keep, SUBLANES)}
        wch = 0.5 * wc_ref[:, col:col + LANES]
        outs = []
        for b in range(SUBLANES):
            acc = wch[keep:keep + 1, :] * cur[b]
            for k in range(1, keep + 1):
                src = cur[b - k] if b >= k else old[b - k + SUBLANES]
                acc = acc + wch[keep - k:keep - k + 1, :] * src
            outs.append(head_norm(_silu_of_half(acc), col))
        cst_ref[0, :, col:col + LANES] = prev_ref[slab, SUBLANES + c_real - keep:SUBLANES + c_real, :]
        prev_ref[slab, 0:SUBLANES, :] = prev_ref[slab, tt:tt + SUBLANES, :]
        for b in range(SUBLANES):
            prev_ref[slab, pl.ds(SUBLANES + b, n8, stride=SUBLANES), :] = outs[b]
        o_ref[:, col:col + LANES] = prev_ref[slab, SUBLANES:, :].astype(o_ref.dtype)

    for j in range(o_ref.shape[1] // PROJ_COLS):
        sl = slice(j * PROJ_COLS, (j + 1) * PROJ_COLS)
        y = jnp.dot(h_ref[...], w_ref[:, sl], preferred_element_type=F32)
        if sl.start < cdim and not has_state:
            for c in range(0, PROJ_COLS, LANES):
                prev_ref[(sl.start + c) // LANES, SUBLANES:, :] = y[:, c:c + LANES]
            for c in range(0, PROJ_COLS, LANES):
                conv_by_phase(sl.start + c)
            continue
        if sl.start < cdim:
            y3 = y.reshape(bt, tt, PROJ_COLS)
            taps = _causal_taps(y3, prev_ref, sl, GDN_CONV_W)
            wch = 0.5 * wc_ref[:, sl]
            acc = wch[keep:keep + 1, :] * y3
            for i, tap in enumerate(taps):
                acc = acc + wch[i:i + 1, :] * tap
            acc = _silu_of_half(acc)
            acc = jnp.concatenate([head_norm(acc[:, :, c:c + GDN_D], sl.start + c)
                                   for c in range(0, PROJ_COLS, GDN_D)], axis=-1)
            cst_ref[:, :, sl] = y3[:, c_real - keep:c_real, :]
            y = acc.reshape(bt * tt, PROJ_COLS)
        o_ref[:, sl] = y.astype(o_ref.dtype)


def _gdn_proj(tiles, x3, gamma, mod3, w, w_tail_t, w_conv, conv_state, c_real, out_dtype):
    B, Tp, D = x3.shape
    nt = w_tail_t.shape[0]
    cdim = w_conv.shape[1]
    N = w.shape[1]
    keep = GDN_CONV_W - 1
    has_state = conv_state is not None
    wt_hi = w_tail_t.astype(BF16)
    wt_lo = (w_tail_t - wt_hi.astype(F32)).astype(BF16)
    in_specs = [tiles.x_spec(D), _resident((1, D)), tiles.seq_spec(1, mod3.shape[-1]),
                _resident(w.shape), _resident((2 * nt, D)), _resident((GDN_CONV_W, cdim))]
    args = [x3, gamma.reshape(1, D), mod3, w, jnp.concatenate([wt_hi, wt_lo], axis=0), w_conv]
    if has_state:
        in_specs.append(tiles.seq_spec(keep, cdim))
        args.append(conv_state)
    proj, ott, cst = pl.pallas_call(
        functools.partial(_gdn_proj_kernel, tiles=tiles, c_real=c_real, has_state=has_state),
        grid=(tiles.n,),
        in_specs=in_specs,
        out_specs=[tiles.rows_spec(N), pl.BlockSpec((nt, tiles.tm), lambda i: (0, i)),
                   tiles.tail_rows_spec(keep, cdim)],
        out_shape=[jax.ShapeDtypeStruct((B * Tp, N), out_dtype),
                   jax.ShapeDtypeStruct((nt, B * Tp), F32),
                   jax.ShapeDtypeStruct((tiles.n * tiles.bt, keep, cdim), F32)],
        scratch_shapes=[pltpu.VMEM((tiles.tm, D), BF16),
                        pltpu.VMEM((tiles.bt, SUBLANES, cdim) if has_state
                                   else (cdim // LANES, SUBLANES + tiles.tt, LANES), F32)],
        compiler_params=_cparams("arbitrary"),
        name="gdn_proj",
    )(*args)
    return proj, ott, tiles.last_tile(cst)


def _ffn_kernel(*refs, tiles, c_real, has_state, final):
    it = iter(refs)
    o_ref, wo_ref, x_ref, gam_ref, mod_ref = (next(it) for _ in range(5))
    wup_ref, wdw_ref, bdw_ref, wd_ref = (next(it) for _ in range(4))
    if has_state:
        cs_ref = next(it)
    if final:
        gamf_ref, modf_ref = next(it), next(it)
    y_ref, cst_ref = next(it), next(it)
    xm_ref, h_ref, act_ref, prev_ref = (next(it) for _ in range(4))
    bt, tt, D = x_ref.shape
    fd = wd_ref.shape[0]
    keep = FFN_CONV_W - 1

    xm_ref[...] = x_ref[...] + _mod(mod_ref[...], 0, D) * _dot(o_ref[...], wo_ref[...]).reshape(bt, tt, D)
    _normed_rows(xm_ref, gam_ref, mod_ref, 1, h_ref)
    if has_state:
        prev_ref[...] = jnp.zeros_like(prev_ref)
        prev_ref[:, SUBLANES - keep:, :] = cs_ref[...]
    else:
        @pl.when(_start_of_sequence(tiles))
        def _():
            prev_ref[...] = jnp.zeros_like(prev_ref)

    for f in range(fd // FFN_COLS):
        sl = slice(f * FFN_COLS, (f + 1) * FFN_COLS)
        h = h_ref[...]
        g3 = jnp.dot(h, wup_ref[:, sl], preferred_element_type=F32).reshape(bt, tt, FFN_COLS)
        val = jnp.dot(h, wup_ref[:, fd + sl.start:fd + sl.stop], preferred_element_type=F32)
        s2, s1 = _causal_taps(g3, prev_ref, sl, FFN_CONV_W)
        wh = 0.5 * wdw_ref[:, sl]
        conv = wh[0:1, :] * s2 + wh[1:2, :] * s1 + wh[2:3, :] * g3 + 0.5 * bdw_ref[:, sl]
        cst_ref[:, :, sl] = g3[:, c_real - keep:c_real, :]
        if not has_state:
            prev_ref[:, :, sl] = g3[:, tt - SUBLANES:, :]
        act_ref[:, sl] = (_silu_of_half(conv).reshape(bt * tt, FFN_COLS) * val).astype(BF16)

    acc = jnp.dot(act_ref[...], wd_ref[...], preferred_element_type=F32).reshape(bt, tt, D)
    xn = xm_ref[...] + _mod(mod_ref[...], 3, D) * acc
    if final:
        mf = modf_ref[...]
        xn = _ada_norm(xn, gamf_ref[...], _mod(mf, 0, D), _mod(mf, 1, D))
    y_ref[...] = xn


def _ffn(tiles, o, w_out, x3, gamma, mod3, w_up, w_dw, b_dw, w_down, conv_state, c_real, final=None):
    B, Tp, D = x3.shape
    Fd = w_down.shape[0]
    K = o.shape[1]
    keep = FFN_CONV_W - 1
    has_state = conv_state is not None
    in_specs = [tiles.rows_spec(K), _resident((K, D)),
                tiles.x_spec(D), _resident((1, D)), tiles.seq_spec(1, mod3.shape[-1]),
                _resident((D, 2 * Fd)), _resident((FFN_CONV_W, Fd)), _resident((1, Fd)),
                _resident((Fd, D))]
    args = [o, w_out, x3, gamma.reshape(1, D), mod3, w_up, w_dw, b_dw.reshape(1, Fd), w_down]
    if has_state:
        in_specs.append(tiles.seq_spec(keep, Fd))
        args.append(conv_state)
    if final is not None:
        in_specs += [_resident((1, D)), tiles.seq_spec(1, final[1].shape[-1])]
        args += [final[0].reshape(1, D), final[1]]
    y3, cst = pl.pallas_call(
        functools.partial(_ffn_kernel, tiles=tiles, c_real=c_real, has_state=has_state,
                          final=final is not None),
        grid=(tiles.n,),
        in_specs=in_specs,
        out_specs=[tiles.x_spec(D), tiles.tail_rows_spec(keep, Fd)],
        out_shape=[jax.ShapeDtypeStruct((B, Tp, D), F32),
                   jax.ShapeDtypeStruct((tiles.n * tiles.bt, keep, Fd), F32)],
        scratch_shapes=[pltpu.VMEM((tiles.bt, tiles.tt, D), F32), pltpu.VMEM((tiles.tm, D), BF16),
                        pltpu.VMEM((tiles.tm, Fd), BF16), pltpu.VMEM((tiles.bt, SUBLANES, Fd), F32)],
        compiler_params=_cparams("arbitrary"),
        name="out_proj_ffn",
    )(*args)
    return y3, tiles.last_tile(cst)


def _ret_core_kernel(*refs, c_real, has_state):
    it = iter(refs)
    q_ref, k_ref, v_ref, g_ref = (next(it) for _ in range(4))
    if has_state:
        s0_ref = next(it)
    o_ref, s_ref = next(it), next(it)
    nb, cp, _ = q_ref.shape

    @pl.when(pl.program_id(1) == 0)
    def _():
        if has_state:
            s_ref[...] = s0_ref[...]
        else:
            s_ref[...] = jnp.zeros_like(s_ref)

    ri = lax.broadcasted_iota(jnp.int32, (cp, cp), 0)
    ci = lax.broadcasted_iota(jnp.int32, (cp, cp), 1)
    causal = ri >= ci
    diff = jnp.where(causal, ri - ci, 0).astype(F32)
    rowi = lax.broadcasted_iota(jnp.int32, (cp, 1), 0)
    row = rowi.astype(F32)
    lgs = [math.log(1.0 - 2.0 ** (-5.0 - h)) for h in range(RET_HEADS)]
    decays = [jnp.where(causal, jnp.exp(diff * lg), 0.0) for lg in lgs]
    k_decays = [jnp.where(rowi < c_real, jnp.exp((c_real - 1.0 - row) * lg), 0.0) for lg in lgs]
    q_decays = [jnp.exp((row + 1.0) * lg) for lg in lgs]

    items = [(i, h) for i in range(nb) for h in range(RET_HEADS)]
    qs = [q_ref[i, :, h * RET_DK:(h + 1) * RET_DK] for i, h in items]
    ks = [k_ref[i, :, h * RET_DK:(h + 1) * RET_DK].astype(F32) for i, h in items]
    vs = [v_ref[i, :, h * RET_DV:(h + 1) * RET_DV] for i, h in items]
    ss = [s_ref[i, h] for i, h in items]
    scores = [_dot_nt(q, k) for q, k in zip(qs, ks)]
    cross = [_dot(q, s) for q, s in zip(qs, ss)]
    upds = [_dot_tn(k * k_decays[h], v) for k, v, (i, h) in zip(ks, vs, items)]
    inner = [_dot(sc * decays[h], v) for sc, v, (i, h) in zip(scores, vs, items)]
    for m, (i, h) in enumerate(items):
        s_ref[i, h] = ss[m] * math.exp(c_real * lgs[h]) + upds[m]
        o = inner[m] + cross[m] * q_decays[h]
        o = o * lax.rsqrt(jnp.mean(o * o, axis=-1, keepdims=True) + EPS)
        gate = _silu(g_ref[i, :, h * RET_DV:(h + 1) * RET_DV].astype(F32))
        o_ref[i, :, h * RET_DV:(h + 1) * RET_DV] = (o * gate).astype(o_ref.dtype)


def _ret_core(proj3, s0, c_real, cp):
    B, Tp, _ = proj3.shape
    qk = RET_HEADS * RET_DK
    vd = RET_HEADS * RET_DV
    has_state = s0 is not None
    nb = min(B, RET_SEQS_PER_STEP)
    in_specs = [pl.BlockSpec((nb, cp, qk), lambda b, n: (b, n, 0)),
                pl.BlockSpec((nb, cp, qk), lambda b, n: (b, n, 1)),
                pl.BlockSpec((nb, cp, vd), lambda b, n: (b, n, 1)),
                pl.BlockSpec((nb, cp, vd), lambda b, n: (b, n, 2))]
    args = [proj3, proj3, proj3, proj3]
    s_spec = pl.BlockSpec((nb, RET_HEADS, RET_DK, RET_DV), lambda b, n: (b, 0, 0, 0))
    if has_state:
        in_specs.append(s_spec)
        args.append(s0)
    return pl.pallas_call(
        functools.partial(_ret_core_kernel, c_real=c_real, has_state=has_state),
        grid=(B // nb, Tp // cp),
        in_specs=in_specs,
        out_specs=[pl.BlockSpec((nb, cp, vd), lambda b, n: (b, n, 0)), s_spec],
        out_shape=[jax.ShapeDtypeStruct((B, Tp, vd), proj3.dtype),
                   jax.ShapeDtypeStruct((B, RET_HEADS, RET_DK, RET_DV), F32)],
        compiler_params=_cparams("parallel", "arbitrary"),
        name="ret_core",
    )(*args)


def _block_inverse_many(ls, ri, ci, bs):
    eye = (ri == ci).astype(F32)
    base = (ri // INV_BASE) == (ci // INV_BASE)
    pws = [-jnp.where(base, l, 0.0) for l in ls]
    ps = [eye + m for m in pws]
    span = 2
    while span < INV_BASE:
        pws = [_dot(pw, pw) for pw in pws]
        ps = [p + _dot(p, pw) for p, pw in zip(ps, pws)]
        span *= 2
    size = INV_BASE
    while size < bs:
        off = ((ri // (2 * size)) == (ci // (2 * size))) & ((ri // size) != (ci // size))
        xs = [_dot(jnp.where(off, l, 0.0), p) for l, p in zip(ls, ps)]
        ps = [p - _dot(p, x) for p, x in zip(ps, xs)]
        size *= 2
    return ps


def _gdn_core_kernel(*refs, c_real, has_state):
    it = iter(refs)
    qkv_ref, z_ref, gt_ref, al_ref, dt_ref, nw_ref = (next(it) for _ in range(6))
    if has_state:
        s0_ref = next(it)
    o_ref, s_ref = next(it), next(it)
    bb, cp, _ = qkv_ref.shape
    bs = cp
    hg = GROUP_ROWS // bs
    ng = GDN_V_HEADS // hg
    rep = GDN_V_HEADS // GDN_K_HEADS
    key = GDN_K_HEADS * GDN_D
    n_items = bb * ng
    assert n_items % SUBLANES == 0

    @pl.when(pl.program_id(1) == 0)
    def _():
        if has_state:
            s_ref[...] = s0_ref[...]
        else:
            s_ref[...] = jnp.zeros_like(s_ref)

    ri = lax.broadcasted_iota(jnp.int32, (GROUP_ROWS, GROUP_ROWS), 0)
    ci = lax.broadcasted_iota(jnp.int32, (GROUP_ROWS, GROUP_ROWS), 1)
    same = (ri // bs) == (ci // bs)
    incl = same & (ri >= ci)
    strict = same & (ri > ci)
    eye = (ri == ci).astype(F32)

    live = (lax.broadcasted_iota(jnp.int32, (1, GROUP_ROWS), 1) % bs) < c_real
    beta_rows = jnp.where(live, jax.nn.sigmoid(gt_ref[0]), 0.0)
    g_rows = jnp.where(live, -jnp.exp(al_ref[...]) * _softplus(gt_ref[1] + dt_ref[...]), 0.0)
    gsum_rows = jnp.dot(g_rows, (same & (ri <= ci)).astype(F32), precision=_HIGHEST,
                        preferred_element_type=F32)
    rows = [beta_rows, gsum_rows]
    assert len(rows) * n_items <= GROUP_ROWS
    if len(rows) * n_items < GROUP_ROWS:
        rows.append(jnp.zeros((GROUP_ROWS - len(rows) * n_items, GROUP_ROWS), F32))
    cols = jnp.concatenate(rows, axis=0).T

    def column(q, m):
        c = q * n_items + m
        return jnp.broadcast_to(cols[:, c:c + 1], (GROUP_ROWS, GROUP_ROWS))

    items = [(i, j) for i in range(bb) for j in range(ng)]

    def stack(i, j, col_of_head):
        return jnp.concatenate(
            [qkv_ref[i, :, col_of_head(j * hg + hh):col_of_head(j * hg + hh) + GDN_D].astype(F32)
             for hh in range(hg)], axis=0)

    qxs = [stack(i, j, lambda h: (h // rep) * GDN_D) for i, j in items]
    kxs = [stack(i, j, lambda h: key + (h // rep) * GDN_D) for i, j in items]
    vxs = [stack(i, j, lambda h: 2 * key + h * GDN_D) for i, j in items]
    bcs = [column(0, m) for m in range(n_items)]
    gcs = [column(1, m) for m in range(n_items)]
    kbs = [kx * bc for kx, bc in zip(kxs, bcs)]
    kqs = [_dot_nt(jnp.concatenate([kb, qx], axis=0), kx) for kb, qx, kx in zip(kbs, qxs, kxs)]
    kks = [r[:GROUP_ROWS] for r in kqs]
    qks = [r[GROUP_ROWS:] for r in kqs]
    decays = [jnp.exp(jnp.where(incl, gc - gsum_rows[m:m + 1, :], -jnp.inf))
              for m, gc in enumerate(gcs)]
    ls = [jnp.where(strict, kk * d, 0.0) for kk, d in zip(kks, decays)]
    attns = [qk * d for qk, d in zip(qks, decays)]
    ps = _block_inverse_many(ls, ri, ci, bs)
    egs = [jnp.exp(gc) for gc in gcs]
    rhss = [jnp.concatenate([vx * bc, kb * eg], axis=-1) for vx, bc, kb, eg in zip(vxs, bcs, kbs, egs)]
    sols = [rhs + _dot(p - eye, rhs) for rhs, p in zip(rhss, ps)]
    qes = [qx * eg for qx, eg in zip(qxs, egs)]

    heads = [(m, i, j * hg + hh, slice(hh * bs, (hh + 1) * bs))
             for m, (i, j) in enumerate(items) for hh in range(hg)]
    wqs = [_dot(jnp.concatenate([sols[m][rs, GDN_D:], qes[m][rs, :]], axis=0), s_ref[i, h])
           for m, i, h, rs in heads]
    v_news, qss = [], []
    for m in range(n_items):
        part = wqs[m * hg:(m + 1) * hg]
        v_news.append(sols[m][:, :GDN_D] - jnp.concatenate([r[:bs] for r in part], axis=0))
        qss.append(jnp.concatenate([r[bs:] for r in part], axis=0))
    outs = [qs + _dot(attn, vn) for qs, attn, vn in zip(qss, attns, v_news)]

    def last_rows(gc):
        return [gc[hh * bs + c_real - 1:hh * bs + c_real, :] for hh in range(hg)]
    kds = [kx * jnp.exp(jnp.concatenate([jnp.broadcast_to(r, (bs, GROUP_ROWS)) for r in last_rows(gc)],
                                        axis=0) - gc)
           for kx, gc in zip(kxs, gcs)]
    upds = [_dot_tn(kds[m][rs, :], v_news[m][rs, :]) for m, i, h, rs in heads]
    for (m, i, h, rs), upd in zip(heads, upds):
        g_last = gcs[m][rs.start + c_real - 1:rs.start + c_real, :]
        s_ref[i, h] = s_ref[i, h] * jnp.exp(g_last) + upd

    nw = nw_ref[...]
    for m, i, h, rs in heads:
        o = outs[m][rs, :]
        o = o * lax.rsqrt(jnp.mean(o * o, axis=-1, keepdims=True) + EPS) * nw
        gate = _silu(z_ref[i, :, h * GDN_D:(h + 1) * GDN_D].astype(F32))
        o_ref[i, :, h * GDN_D:(h + 1) * GDN_D] = (o * gate).astype(o_ref.dtype)


def _gdn_core(proj3, gates, a_log, dt_bias, norm_w, s0, c_real, cp, bb):
    B, Tp, _ = proj3.shape
    key = GDN_K_HEADS * GDN_D
    vd = GDN_V_HEADS * GDN_D
    cdim = 2 * key + vd
    nc = Tp // cp
    hg = GROUP_ROWS // cp
    ng = GDN_V_HEADS // hg
    has_state = s0 is not None
    lanes = lambda a: jnp.tile(jnp.repeat(a.reshape(ng, hg), cp, axis=1), (bb, 1))
    full = lambda shape: pl.BlockSpec(shape, lambda b, n: (0,) * len(shape))
    in_specs = [pl.BlockSpec((bb, cp, cdim), lambda b, n: (b, n, 0)),
                pl.BlockSpec((bb, cp, vd), lambda b, n: (b, n, cdim // vd)),
                pl.BlockSpec((None, 2, bb * ng, GROUP_ROWS), lambda b, n: (b * nc + n, 0, 0, 0)),
                full((bb * ng, GROUP_ROWS)), full((bb * ng, GROUP_ROWS)), full((1, GDN_D))]
    args = [proj3, proj3, gates, lanes(a_log), lanes(dt_bias), norm_w.reshape(1, GDN_D)]
    s_spec = pl.BlockSpec((bb, GDN_V_HEADS, GDN_D, GDN_D), lambda b, n: (b, 0, 0, 0))
    if has_state:
        in_specs.append(s_spec)
        args.append(s0)
    return pl.pallas_call(
        functools.partial(_gdn_core_kernel, c_real=c_real, has_state=has_state),
        grid=(B // bb, nc),
        in_specs=in_specs,
        out_specs=[pl.BlockSpec((bb, cp, vd), lambda b, n: (b, n, 0)), s_spec],
        out_shape=[jax.ShapeDtypeStruct((B, Tp, vd), proj3.dtype),
                   jax.ShapeDtypeStruct((B, GDN_V_HEADS, GDN_D, GDN_D), F32)],
        compiler_params=_cparams("parallel", "arbitrary"),
        name="gdn_core",
    )(*args)


def _rope_tables(pos0, t_real, t_pad):
    half = RET_DK // 2
    inv_freq = ROPE_BASE ** (-np.arange(half, dtype=np.float64) / half)
    pos = pos0 + np.arange(t_pad, dtype=np.float64)
    ang = pos[:, None] * inv_freq[None, :]
    live = (np.arange(t_pad) < t_real)[:, None]
    return (jnp.asarray(np.where(live, np.cos(ang), 0.0), F32),
            jnp.asarray(np.where(live, np.sin(ang), 0.0), F32))


def _trunk(x, mods, mod_final, pos0, s_ret, s_gdn, s_gconv, s_fconv, p, wb, tiles, act_dtype):
    B, T, D = x.shape
    Tp = tiles.Tp
    x3 = x if T == Tp else jnp.pad(x, ((0, 0), (0, Tp - T), (0, 0)))
    M = B * Tp
    stateful = s_ret is not None
    new_fconv = []

    mod_mix, mod_ffn = mods[0][:, None, :2 * D], mods[0][:, None, 2 * D:]
    cos, sin = _rope_tables(pos0, T, Tp)
    proj = _ret_proj(tiles, x3, p['norm_mix'][0], mod_mix, wb['ret_in'], cos, sin, act_dtype)
    cp = min(RET_CHUNK, Tp)
    o3, new_ret = _ret_core(proj.reshape(B, Tp, -1), s_ret[0] if stateful else None, min(T, cp), cp)
    x3, cst = _ffn(tiles, o3.reshape(M, -1), wb['ret_out'], x3, p['norm_ffn'][0], mod_ffn,
                   wb['ffn_up'][0], p['w_ffn_dw'][0], p['b_ffn_dw'][0], wb['ffn_down'][0],
                   s_fconv[0] if stateful else None, min(T, tiles.tt))
    new_fconv.append(cst)

    mod_mix, mod_ffn = mods[1][:, None, :2 * D], mods[1][:, None, 2 * D:]
    cp = min(GDN_CHUNK, Tp)
    nc = Tp // cp
    ng = GDN_V_HEADS * cp // GROUP_ROWS
    bb = min(B, max(SUBLANES // ng, GDN_SEQS_PER_STEP))
    proj, bat, new_gconv = _gdn_proj(tiles, x3, p['norm_mix'][1], mod_mix, wb['gdn_in'], wb['gdn_tail_t'],
                                     p['w_gdn_conv'][0], s_gconv[0] if stateful else None,
                                     min(T, tiles.tt), act_dtype)
    gates = bat.reshape(2, ng, GROUP_ROWS // cp, B // bb, bb, nc, cp)
    gates = gates.transpose(3, 5, 0, 4, 1, 2, 6).reshape(B // bb * nc, 2, bb * ng, GROUP_ROWS)
    o3, new_gdn = _gdn_core(proj.reshape(B, Tp, -1), gates, p['gdn_a_log'][0], p['gdn_dt_bias'][0],
                            p['gdn_norm'][0], s_gdn[0] if stateful else None, min(T, cp), cp, bb)
    y3, cst = _ffn(tiles, o3.reshape(M, -1), wb['gdn_out'], x3, p['norm_ffn'][1], mod_ffn,
                   wb['ffn_up'][1], p['w_ffn_dw'][1], p['b_ffn_dw'][1], wb['ffn_down'][1],
                   s_fconv[1] if stateful else None, min(T, tiles.tt),
                   final=(p['norm_final'], mod_final[:, None, :]))
    new_fconv.append(cst)
    return (y3[:, :T], new_ret[None], new_gdn[None], new_gconv[None], jnp.stack(new_fconv))


def kernel(x_prompt, x_sample, state_ret, state_gdn, state_gdn_conv, state_ffn_conv, c_prompt, c_sample, w_ada, b_ada, w_ada_final, b_ada_final, norm_mix, norm_ffn, norm_final, w_ret_in, w_ret_out, w_gdn_in, w_gdn_conv, gdn_a_log, gdn_dt_bias, gdn_norm, w_gdn_out, w_ffn_up, w_ffn_dw, b_ffn_dw, w_ffn_down):
    p = {'norm_mix': norm_mix, 'norm_ffn': norm_ffn, 'norm_final': norm_final,
         'w_gdn_conv': w_gdn_conv, 'gdn_a_log': gdn_a_log, 'gdn_dt_bias': gdn_dt_bias,
         'gdn_norm': gdn_norm, 'w_ffn_dw': w_ffn_dw, 'b_ffn_dw': b_ffn_dw}
    cdim = (2 * GDN_K_HEADS + GDN_V_HEADS) * GDN_D
    vd = GDN_V_HEADS * GDN_D
    w_gdn_in_t = jnp.swapaxes(w_gdn_in, 1, 2)
    wb = {'ret_in': _layer_bf16(w_ret_in, 0), 'ret_out': _layer_bf16(w_ret_out, 0),
          'gdn_in': _layer_bf16(w_gdn_in_t, 0, transposed_cols=cdim + vd),
          'gdn_out': _layer_bf16(w_gdn_out, 0), 'gdn_tail_t': w_gdn_in_t[0, cdim + vd:, :],
          'ffn_up': [_layer_bf16(w_ffn_up, l) for l in range(2)],
          'ffn_down': [_layer_bf16(w_ffn_down, l) for l in range(2)]}

    bp, tp = x_prompt.shape[:2]
    bs_ = x_sample.shape[0]
    c_all = jnp.concatenate([c_prompt, c_sample], axis=0)
    mods = _ada_mod(c_all, w_ada, b_ada)
    mod_final = _ada_mod(c_all, w_ada_final[None], b_ada_final[None])[0]

    out_p = _trunk(x_prompt, mods[:, :bp], mod_final[:bp], 0, None, None, None, None, p, wb,
                   _Tiles(bp, tp, 1, min(512, tp)), BF16)
    out_s = _trunk(x_sample, mods[:, bp:], mod_final[bp:], PAST_LEN, state_ret, state_gdn,
                   state_gdn_conv, state_ffn_conv, p, wb,
                   _Tiles(bs_, SUBLANES, min(32, bs_), SUBLANES), BF16)
    y_p, ret_p, gdn_p, gconv_p, fconv_p = out_p
    y_s, ret_s, gdn_s, gconv_s, fconv_s = out_s
    return (y_p, y_s, ret_p, gdn_p, gconv_p, fconv_p, ret_s, gdn_s, gconv_s, fconv_s)
```

```python
import functools
import math

import numpy as np
import jax
import jax.numpy as jnp
from jax import lax
from jax.experimental import pallas as pl
from jax.experimental.pallas import tpu as pltpu

F32 = jnp.float32
BF16 = jnp.bfloat16
EPS = 1e-6
ROPE_BASE = 10000.0
PAST_LEN = 16384

RET_HEADS = 4
RET_DK = 256
RET_DV = 512
RET_CHUNK = 256
GDN_K_HEADS = 8
GDN_V_HEADS = 16
GDN_D = 128
GDN_CHUNK = 64
GDN_CONV_W = 4
FFN_CONV_W = 3
SUBLANES = 8
LANES = 128
INV_BASE = 8
GROUP_ROWS = 128
GDN_SEQS_PER_STEP = 4
RET_SEQS_PER_STEP = 2
PROJ_COLS = 512
FFN_COLS = 256
PROMPT_TILE_ROWS = 512
SHORT_TILE_SEQS = 32
VMEM_LIMIT = 48 * 1024 * 1024

_HIGHEST = lax.Precision.HIGHEST


def _cparams(*sem):
    return pltpu.CompilerParams(dimension_semantics=sem, vmem_limit_bytes=VMEM_LIMIT)


def _silu_of_half(hx):
    return hx + hx * jnp.tanh(hx)


def _silu(x):
    return _silu_of_half(0.5 * x)


def _softplus(x):
    return jnp.maximum(x, 0.0) + jnp.log1p(jnp.exp(-jnp.abs(x)))


def _ada_norm(x, gamma, shift, scale):
    ms = jnp.mean(x * x, axis=-1, keepdims=True)
    xn = x * lax.rsqrt(ms + EPS) * gamma
    return xn * (1.0 + scale) + shift


def _dot(a, b):
    return jnp.dot(a.astype(BF16), b.astype(BF16), preferred_element_type=F32)


def _dot_nt(a, b):
    return lax.dot_general(a.astype(BF16), b.astype(BF16), (((1,), (1,)), ((), ())),
                           preferred_element_type=F32)


def _dot_tn(a, b):
    return lax.dot_general(a.astype(BF16), b.astype(BF16), (((0,), (0,)), ((), ())),
                           preferred_element_type=F32)


def _cast_kernel(w_ref, o_ref, *, transpose):
    w = w_ref[...]
    o_ref[...] = (w.T if transpose else w).astype(o_ref.dtype)


def _layer_bf16(w, layer, transposed_cols=None):
    tiled = w.shape[1] if transposed_cols is None else transposed_cols
    rows = 512 if tiled % 512 == 0 else 256
    if transposed_cols is None:
        _, K, N = w.shape
        grid, in_block, out_block = K // rows, (None, rows, N), (rows, N)
        in_map, out_map = (lambda i: (layer, i, 0)), (lambda i: (i, 0))
    else:
        K, N = w.shape[2], transposed_cols
        grid, in_block, out_block = N // rows, (None, rows, K), (K, rows)
        in_map, out_map = (lambda i: (layer, i, 0)), (lambda i: (0, i))
    return pl.pallas_call(
        functools.partial(_cast_kernel, transpose=transposed_cols is not None),
        grid=(grid,),
        in_specs=[pl.BlockSpec(in_block, in_map)],
        out_specs=pl.BlockSpec(out_block, out_map),
        out_shape=jax.ShapeDtypeStruct((K, N), BF16),
        compiler_params=_cparams("parallel"),
        name="cast_bf16",
    )(w)


def _ada_kernel(c_ref, w_ref, b_ref, o_ref):
    cs = _silu(c_ref[...])
    o_ref[...] = _dot(cs, w_ref[...]) + b_ref[...]


def _ada_mod(c, w, b, tn=2048):
    L, D, N = w.shape
    Mc = c.shape[0]
    return pl.pallas_call(
        _ada_kernel,
        grid=(L, N // tn),
        in_specs=[pl.BlockSpec((Mc, D), lambda l, j: (0, 0)),
                  pl.BlockSpec((None, D, tn), lambda l, j: (l, 0, j)),
                  pl.BlockSpec((None, 1, tn), lambda l, j: (l, 0, j))],
        out_specs=pl.BlockSpec((None, Mc, tn), lambda l, j: (l, 0, j)),
        out_shape=jax.ShapeDtypeStruct((L, Mc, N), F32),
        compiler_params=_cparams("parallel", "parallel"),
        name="ada_mod",
    )(c, w, b.reshape(L, 1, N))


class _Tiles:
    def __init__(self, B, Tp, bt, tt):
        assert bt == 1 or tt == Tp
        self.B, self.Tp, self.bt, self.tt = B, Tp, bt, tt
        self.tps = Tp // tt
        self.n = (B // bt) * self.tps
        self.tm = bt * tt

    def x_spec(self, D):
        tps = self.tps
        return pl.BlockSpec((self.bt, self.tt, D), lambda i: (i // tps, i % tps, 0))

    def seq_spec(self, r, D):
        tps = self.tps
        return pl.BlockSpec((self.bt, r, D), lambda i: (i // tps, 0, 0))

    def rows_spec(self, N):
        return pl.BlockSpec((self.tm, N), lambda i: (i, 0))

    def pos_spec(self, N):
        tps = self.tps
        return pl.BlockSpec((self.tt, N), lambda i: (i % tps, 0))

    def tail_rows_spec(self, r, N):
        return pl.BlockSpec((self.bt, r, N), lambda i: (i, 0, 0))

    def last_tile(self, a):
        return a[self.tps - 1::self.tps]


def _resident(shape):
    return pl.BlockSpec(shape, lambda i: (0,) * len(shape), pipeline_mode=pl.Buffered(1))


def _mod(m3, k, D):
    return m3[:, :, k * D:(k + 1) * D]


def _normed_rows(x_ref, gam_ref, mod_ref, k_shift, h_ref):
    D = x_ref.shape[-1]
    m3 = mod_ref[...]
    h3 = _ada_norm(x_ref[...], gam_ref[...], _mod(m3, k_shift, D), _mod(m3, k_shift + 1, D))
    h_ref[...] = h3.reshape(h_ref.shape).astype(BF16)
    return h3


def _causal_taps(g3, prev_ref, sl, width):
    ext = jnp.concatenate([prev_ref[:, :, sl], g3], axis=1)
    return [pltpu.roll(ext, k, axis=1)[:, SUBLANES:, :] for k in range(width - 1, 0, -1)]


def _start_of_sequence(tiles):
    return (pl.program_id(0) % tiles.tps) == 0


def _ret_proj_kernel(x_ref, gam_ref, mod_ref, w_ref, cos_ref, sin_ref, o_ref, h_ref):
    bt, tt, _ = x_ref.shape
    _normed_rows(x_ref, gam_ref, mod_ref, 0, h_ref)
    cos, sin = cos_ref[...], sin_ref[...]
    qk = RET_HEADS * RET_DK
    half = RET_DK // 2
    for j in range(w_ref.shape[1] // PROJ_COLS):
        sl = slice(j * PROJ_COLS, (j + 1) * PROJ_COLS)
        y = jnp.dot(h_ref[...], w_ref[:, sl], preferred_element_type=F32)
        if sl.start < 2 * qk:
            y3 = y.reshape(bt, tt, PROJ_COLS)
            parts = []
            for c in range(PROJ_COLS // RET_DK):
                x1 = y3[:, :, c * RET_DK:c * RET_DK + half]
                x2 = y3[:, :, c * RET_DK + half:(c + 1) * RET_DK]
                parts += [x1 * cos - x2 * sin, x1 * sin + x2 * cos]
            y = jnp.concatenate(parts, axis=-1).reshape(bt * tt, PROJ_COLS)
            if sl.start >= qk:
                y = y * (RET_DK ** -0.5)
        o_ref[:, sl] = y.astype(o_ref.dtype)


def _ret_proj(tiles, x3, gamma, mod3, w, cos, sin, out_dtype):
    B, Tp, D = x3.shape
    N = w.shape[1]
    return pl.pallas_call(
        _ret_proj_kernel,
        grid=(tiles.n,),
        in_specs=[tiles.x_spec(D), _resident((1, D)), tiles.seq_spec(1, mod3.shape[-1]),
                  _resident((D, N)), tiles.pos_spec(cos.shape[1]), tiles.pos_spec(sin.shape[1])],
        out_specs=tiles.rows_spec(N),
        out_shape=jax.ShapeDtypeStruct((B * Tp, N), out_dtype),
        scratch_shapes=[pltpu.VMEM((tiles.tm, D), BF16)],
        compiler_params=_cparams("parallel"),
        name="ret_proj",
    )(x3, gamma.reshape(1, D), mod3, w, cos, sin)


def _gdn_proj_kernel(*refs, tiles, c_real, has_state):
    it = iter(refs)
    x_ref, gam_ref, mod_ref, w_ref, wtt_ref, wc_ref = (next(it) for _ in range(6))
    if has_state:
        cs_ref = next(it)
    o_ref, ott_ref, cst_ref = next(it), next(it), next(it)
    h_ref, prev_ref = next(it), next(it)
    bt, tt, _ = x_ref.shape
    key = GDN_K_HEADS * GDN_D
    cdim = wc_ref.shape[1]
    keep = GDN_CONV_W - 1

    h3 = _normed_rows(x_ref, gam_ref, mod_ref, 0, h_ref)
    nt = ott_ref.shape[0]
    h_hi = h_ref[...]
    h_lo = (h3.reshape(bt * tt, -1) - h_hi.astype(F32)).astype(BF16)
    w_hl = wtt_ref[...]
    a = _dot_nt(w_hl, h_hi)
    ott_ref[...] = a[:nt] + a[nt:] + _dot_nt(w_hl[:nt], h_lo)
    if has_state:
        prev_ref[...] = jnp.zeros_like(prev_ref)
        prev_ref[:, SUBLANES - keep:, :] = cs_ref[...]
    else:
        @pl.when(_start_of_sequence(tiles))
        def _():
            prev_ref[:, 0:SUBLANES, :] = jnp.zeros((prev_ref.shape[0], SUBLANES, LANES), F32)

    def head_norm(a, col):
        if col >= 2 * key:
            return a
        inv = lax.rsqrt(jnp.sum(a * a, axis=-1, keepdims=True) + EPS)
        return a * (inv * (GDN_D ** -0.5) if col < key else inv)

    def conv_by_phase(col):
        slab = col // LANES
        n8 = tt // SUBLANES
        cur =[prev_ref[slab, pl.ds(SUBLANES + b, n8, stride=SUBLANES), :] for b in range(SUBLANES)]
        old = {b: prev_ref[slab, pl.ds(b, n8, stride=SUBLANES), :]
               for b in range(SUBLANES - keep, SUBLANES)}
        wch = 0.5 * wc_ref[:, col:col + LANES]
        outs = []
        for b in range(SUBLANES):
            acc = wch[keep:keep + 1, :] * cur[b]
            for k in range(1, keep + 1):
                src = cur[b - k] if b >= k else old[b - k + SUBLANES]
                acc = acc + wch[keep - k:keep - k + 1, :] * src
            outs.append(head_norm(_silu_of_half(acc), col))
        cst_ref[0, :, col:col + LANES] = prev_ref[slab, SUBLANES + c_real - keep:SUBLANES + c_real, :]
        prev_ref[slab, 0:SUBLANES, :] = prev_ref[slab, tt:tt + SUBLANES, :]
        for b in range(SUBLANES):
            prev_ref[slab, pl.ds(SUBLANES + b, n8, stride=SUBLANES), :] = outs[b]
        o_ref[:, col:col + LANES] = prev_ref[slab, SUBLANES:, :].astype(o_ref.dtype)

    for j in range(o_ref.shape[1] // PROJ_COLS):
        sl = slice(j * PROJ_COLS, (j + 1) * PROJ_COLS)
        y = jnp.dot(h_ref[...], w_ref[:, sl], preferred_element_type=F32)
        if sl.start < cdim and not has_state:
            for c in range(0, PROJ_COLS, LANES):
                prev_ref[(sl.start + c) // LANES, SUBLANES:, :] = y[:, c:c + LANES]
            for c in range(0, PROJ_COLS, LANES):
                conv_by_phase(sl.start + c)
            continue
        if sl.start < cdim:
            y3 = y.reshape(bt, tt, PROJ_COLS)
            taps = _causal_taps(y3, prev_ref, sl, GDN_CONV_W)
            wch = 0.5 * wc_ref[:, sl]
            acc = wch[keep:keep + 1, :] * y3
            for i, tap in enumerate(taps):
                acc = acc + wch[i:i + 1, :] * tap
            acc = _silu_of_half(acc)
            acc = jnp.concatenate([head_norm(acc[:, :, c:c + GDN_D], sl.start + c)
                                   for c in range(0, PROJ_COLS, GDN_D)], axis=-1)
            cst_ref[:, :, sl] = y3[:, c_real - keep:c_real, :]
            y = acc.reshape(bt * tt, PROJ_COLS)
        o_ref[:, sl] = y.astype(o_ref.dtype)


def _gdn_proj(tiles, x3, gamma, mod3, w, w_tail_t, w_conv, conv_state, c_real, out_dtype):
    B, Tp, D = x3.shape
    nt = w_tail_t.shape[0]
    cdim = w_conv.shape[1]
    N = w.shape[1]
    keep = GDN_CONV_W - 1
    has_state = conv_state is not None
    wt_hi = w_tail_t.astype(BF16)
    wt_lo = (w_tail_t - wt_hi.astype(F32)).astype(BF16)
    in_specs = [tiles.x_spec(D), _resident((1, D)), tiles.seq_spec(1, mod3.shape[-1]),
                _resident(w.shape), _resident((2 * nt, D)), _resident((GDN_CONV_W, cdim))]
    args = [x3, gamma.reshape(1, D), mod3, w, jnp.concatenate([wt_hi, wt_lo], axis=0), w_conv]
    if has_state:
        in_specs.append(tiles.seq_spec(keep, cdim))
        args.append(conv_state)
    proj, ott, cst = pl.pallas_call(
        functools.partial(_gdn_proj_kernel, tiles=tiles, c_real=c_real, has_state=has_state),
        grid=(tiles.n,),
        in_specs=in_specs,
        out_specs=[tiles.rows_spec(N), pl.BlockSpec((nt, tiles.tm), lambda i: (0, i)),
                   tiles.tail_rows_spec(keep, cdim)],
        out_shape=[jax.ShapeDtypeStruct((B * Tp, N), out_dtype),
                   jax.ShapeDtypeStruct((nt, B * Tp), F32),
                   jax.ShapeDtypeStruct((tiles.n * tiles.bt, keep, cdim), F32)],
        scratch_shapes=[pltpu.VMEM((tiles.tm, D), BF16),
                        pltpu.VMEM((tiles.bt, SUBLANES, cdim) if has_state
                                   else (cdim // LANES, SUBLANES + tiles.tt, LANES), F32)],
        compiler_params=_cparams("arbitrary"),
        name="gdn_proj",
    )(*args)
    return proj, ott, tiles.last_tile(cst)


def _ffn_kernel(*refs, tiles, c_real, has_state, final):
    it = iter(refs)
    o_ref, wo_ref, x_ref, gam_ref, mod_ref = (next(it) for _ in range(5))
    wup_ref, wdw_ref, bdw_ref, wd_ref = (next(it) for _ in range(4))
    if has_state:
        cs_ref = next(it)
    if final:
        gamf_ref, modf_ref = next(it), next(it)
    y_ref, cst_ref = next(it), next(it)
    xm_ref, h_ref, act_ref, prev_ref = (next(it) for _ in range(4))
    bt, tt, D = x_ref.shape
    fd = wd_ref.shape[0]
    keep = FFN_CONV_W - 1

    xm_ref[...] = x_ref[...] + _mod(mod_ref[...], 0, D) * _dot(o_ref[...], wo_ref[...]).reshape(bt, tt, D)
    _normed_rows(xm_ref, gam_ref, mod_ref, 1, h_ref)
    if has_state:
        prev_ref[...] = jnp.zeros_like(prev_ref)
        prev_ref[:, SUBLANES - keep:, :] = cs_ref[...]
    else:
        @pl.when(_start_of_sequence(tiles))
        def _():
            prev_ref[...] = jnp.zeros_like(prev_ref)

    for f in range(fd // FFN_COLS):
        sl = slice(f * FFN_COLS, (f + 1) * FFN_COLS)
        h = h_ref[...]
        g3 = jnp.dot(h, wup_ref[:, sl], preferred_element_type=F32).reshape(bt, tt, FFN_COLS)
        val = jnp.dot(h, wup_ref[:, fd + sl.start:fd + sl.stop], preferred_element_type=F32)
        s2, s1 = _causal_taps(g3, prev_ref, sl, FFN_CONV_W)
        wh = 0.5 * wdw_ref[:, sl]
        conv = wh[0:1, :] * s2 + wh[1:2, :] * s1 + wh[2:3, :] * g3 + 0.5 * bdw_ref[:, sl]
        cst_ref[:, :, sl] = g3[:, c_real - keep:c_real, :]
        if not has_state:
            prev_ref[:, :, sl] = g3[:, tt - SUBLANES:, :]
        act_ref[:, sl] = (_silu_of_half(conv).reshape(bt * tt, FFN_COLS) * val).astype(BF16)

    acc = jnp.dot(act_ref[...], wd_ref[...], preferred_element_type=F32).reshape(bt, tt, D)
    xn = xm_ref[...] + _mod(mod_ref[...], 3, D) * acc
    if final:
        mf = modf_ref[...]
        xn = _ada_norm(xn, gamf_ref[...], _mod(mf, 0, D), _mod(mf, 1, D))
    y_ref[...] = xn


def _ffn(tiles, o, w_out, x3, gamma, mod3, w_up, w_dw, b_dw, w_down, conv_state, c_real, final=None):
    B, Tp, D = x3.shape
    Fd = w_down.shape[0]
    K = o.shape[1]
    keep = FFN_CONV_W - 1
    has_state = conv_state is not None
    in_specs = [tiles.rows_spec(K), _resident((K, D)),
                tiles.x_spec(D), _resident((1, D)), tiles.seq_spec(1, mod3.shape[-1]),
                _resident((D, 2 * Fd)), _resident((FFN_CONV_W, Fd)), _resident((1, Fd)),
                _resident((Fd, D))]
    args = [o, w_out, x3, gamma.reshape(1, D), mod3, w_up, w_dw, b_dw.reshape(1, Fd), w_down]
    if has_state:
        in_specs.append(tiles.seq_spec(keep, Fd))
        args.append(conv_state)
    if final is not None:
        in_specs += [_resident((1, D)), tiles.seq_spec(1, final[1].shape[-1])]
        args += [final[0].reshape(1, D), final[1]]
    y3, cst = pl.pallas_call(
        functools.partial(_ffn_kernel, tiles=tiles, c_real=c_real, has_state=has_state,
                          final=final is not None),
        grid=(tiles.n,),
        in_specs=in_specs,
        out_specs=[tiles.x_spec(D), tiles.tail_rows_spec(keep, Fd)],
        out_shape=[jax.ShapeDtypeStruct((B, Tp, D), F32),
                   jax.ShapeDtypeStruct((tiles.n * tiles.bt, keep, Fd), F32)],
        scratch_shapes=[pltpu.VMEM((tiles.bt, tiles.tt, D), F32), pltpu.VMEM((tiles.tm, D), BF16),
                        pltpu.VMEM((tiles.tm, Fd), BF16), pltpu.VMEM((tiles.bt, SUBLANES, Fd), F32)],
        compiler_params=_cparams("arbitrary"),
        name="out_proj_ffn",
    )(*args)
    return y3, tiles.last_tile(cst)


def _ret_core_kernel(*refs, c_real, has_state):
    it = iter(refs)
    q_ref, k_ref, v_ref, g_ref = (next(it) for _ in range(4))
    if has_state:
        s0_ref = next(it)
    o_ref, s_ref = next(it), next(it)
    nb, cp, _ = q_ref.shape

    @pl.when(pl.program_id(1) == 0)
    def _():
        if has_state:
            s_ref[...] = s0_ref[...]
        else:
            s_ref[...] = jnp.zeros_like(s_ref)

    ri = lax.broadcasted_iota(jnp.int32, (cp, cp), 0)
    ci = lax.broadcasted_iota(jnp.int32, (cp, cp), 1)
    causal = ri >= ci
    diff = jnp.where(causal, ri - ci, 0).astype(F32)
    rowi = lax.broadcasted_iota(jnp.int32, (cp, 1), 0)
    row = rowi.astype(F32)
    lgs = [math.log(1.0 - 2.0 ** (-5.0 - h)) for h in range(RET_HEADS)]
    decays = [jnp.where(causal, jnp.exp(diff * lg), 0.0) for lg in lgs]
    k_decays = [jnp.where(rowi < c_real, jnp.exp((c_real - 1.0 - row) * lg), 0.0) for lg in lgs]
    q_decays = [jnp.exp((row + 1.0) * lg) for lg in lgs]

    items = [(i, h) for i in range(nb) for h in range(RET_HEADS)]
    qs = [q_ref[i, :, h * RET_DK:(h + 1) * RET_DK] for i, h in items]
    ks = [k_ref[i, :, h * RET_DK:(h + 1) * RET_DK].astype(F32) for i, h in items]
    vs = [v_ref[i, :, h * RET_DV:(h + 1) * RET_DV] for i, h in items]
    ss = [s_ref[i, h] for i, h in items]
    scores = [_dot_nt(q, k) for q, k in zip(qs, ks)]
    cross = [_dot(q, s) for q, s in zip(qs, ss)]
    upds = [_dot_tn(k * k_decays[h], v) for k, v, (i, h) in zip(ks, vs, items)]
    inner = [_dot(sc * decays[h], v) for sc, v, (i, h) in zip(scores, vs, items)]
    for m, (i, h) in enumerate(items):
        s_ref[i, h] = ss[m] * math.exp(c_real * lgs[h]) + upds[m]
        o = inner[m] + cross[m] * q_decays[h]
        o = o * lax.rsqrt(jnp.mean(o * o, axis=-1, keepdims=True) + EPS)
        gate = _silu(g_ref[i, :, h * RET_DV:(h + 1) * RET_DV].astype(F32))
        o_ref[i, :, h * RET_DV:(h + 1) * RET_DV] = (o * gate).astype(o_ref.dtype)


def _ret_core(proj3, s0, c_real, cp):
    B, Tp, _ = proj3.shape
    qk = RET_HEADS * RET_DK
    vd = RET_HEADS * RET_DV
    has_state = s0 is not None
    nb = min(B, RET_SEQS_PER_STEP)
    in_specs = [pl.BlockSpec((nb, cp, qk), lambda b, n: (b, n, 0)),
                pl.BlockSpec((nb, cp, qk), lambda b, n: (b, n, 1)),
                pl.BlockSpec((nb, cp, vd), lambda b, n: (b, n, 1)),
                pl.BlockSpec((nb, cp, vd), lambda b, n: (b, n, 2))]
    args = [proj3, proj3, proj3, proj3]
    s_spec = pl.BlockSpec((nb, RET_HEADS, RET_DK, RET_DV), lambda b, n: (b, 0, 0, 0))
    if has_state:
        in_specs.append(s_spec)
        args.append(s0)
    return pl.pallas_call(
        functools.partial(_ret_core_kernel, c_real=c_real, has_state=has_state),
        grid=(B // nb, Tp // cp),
        in_specs=in_specs,
        out_specs=[pl.BlockSpec((nb, cp, vd), lambda b, n: (b, n, 0)), s_spec],
        out_shape=[jax.ShapeDtypeStruct((B, Tp, vd), proj3.dtype),
                   jax.ShapeDtypeStruct((B, RET_HEADS, RET_DK, RET_DV), F32)],
        compiler_params=_cparams("parallel", "arbitrary"),
        name="ret_core",
    )(*args)


def _block_inverse_many(ls, ri, ci, bs):
    eye = (ri == ci).astype(F32)
    base = (ri // INV_BASE) == (ci // INV_BASE)
    pws = [-jnp.where(base, l, 0.0) for l in ls]
    ps = [eye + m for m in pws]
    span = 2
    while span < INV_BASE:
        pws = [_dot(pw, pw) for pw in pws]
        ps = [p + _dot(p, pw) for p, pw in zip(ps, pws)]
        span *= 2
    size = INV_BASE
    while size < bs:
        off = ((ri // (2 * size)) == (ci // (2 * size))) & ((ri // size) != (ci // size))
        xs = [_dot(jnp.where(off, l, 0.0), p) for l, p in zip(ls, ps)]
        ps = [p - _dot(p, x) for p, x in zip(ps, xs)]
        size *= 2
    return ps


def _gdn_core_kernel(*refs, c_real, has_state):
    it = iter(refs)
    qkv_ref, z_ref, gt_ref, al_ref, dt_ref, nw_ref = (next(it) for _ in range(6))
    if has_state:
        s0_ref = next(it)
    o_ref, s_ref = next(it), next(it)
    bb, cp, _ = qkv_ref.shape
    bs = cp
    hg = GROUP_ROWS // bs
    ng = GDN_V_HEADS // hg
    rep = GDN_V_HEADS // GDN_K_HEADS
    key = GDN_K_HEADS * GDN_D
    n_items = bb * ng
    assert n_items % SUBLANES == 0

    @pl.when(pl.program_id(1) == 0)
    def _():
        if has_state:
            s_ref[...] = s0_ref[...]
        else:
            s_ref[...] = jnp.zeros_like(s_ref)

    ri = lax.broadcasted_iota(jnp.int32, (GROUP_ROWS, GROUP_ROWS), 0)
    ci = lax.broadcasted_iota(jnp.int32, (GROUP_ROWS, GROUP_ROWS), 1)
    same = (ri // bs) == (ci // bs)
    incl = same & (ri >= ci)
    strict = same & (ri > ci)
    eye = (ri == ci).astype(F32)

    live = (lax.broadcasted_iota(jnp.int32, (1, GROUP_ROWS), 1) % bs) < c_real
    beta_rows = jnp.where(live, jax.nn.sigmoid(gt_ref[0]), 0.0)
    g_rows = jnp.where(live, -jnp.exp(al_ref[...]) * _softplus(gt_ref[1] + dt_ref[...]), 0.0)
    gsum_rows = jnp.dot(g_rows, (same & (ri <= ci)).astype(F32), precision=_HIGHEST,
                        preferred_element_type=F32)
    rows = [beta_rows, gsum_rows]
    assert len(rows) * n_items <= GROUP_ROWS
    if len(rows) * n_items < GROUP_ROWS:
        rows.append(jnp.zeros((GROUP_ROWS - len(rows) * n_items, GROUP_ROWS), F32))
    cols = jnp.concatenate(rows, axis=0).T

    def column(q, m):
        c = q * n_items + m
        return jnp.broadcast_to(cols[:, c:c + 1], (GROUP_ROWS, GROUP_ROWS))

    items = [(i, j) for i in range(bb) for j in range(ng)]

    def stack(i, j, col_of_head):
        return jnp.concatenate(
            [qkv_ref[i, :, col_of_head(j * hg + hh):col_of_head(j * hg + hh) + GDN_D].astype(F32)
             for hh in range(hg)], axis=0)

    qxs = [stack(i, j, lambda h: (h // rep) * GDN_D) for i, j in items]
    kxs = [stack(i, j, lambda h: key + (h // rep) * GDN_D) for i, j in items]
    vxs = [stack(i, j, lambda h: 2 * key + h * GDN_D) for i, j in items]
    bcs = [column(0, m) for m in range(n_items)]
    gcs = [column(1, m) for m in range(n_items)]
    kbs = [kx * bc for kx, bc in zip(kxs, bcs)]
    kqs = [_dot_nt(jnp.concatenate([kb, qx], axis=0), kx) for kb, qx, kx in zip(kbs, qxs, kxs)]
    kks = [r[:GROUP_ROWS] for r in kqs]
    qks = [r[GROUP_ROWS:] for r in kqs]
    decays = [jnp.exp(jnp.where(incl, gc - gsum_rows[m:m + 1, :], -jnp.inf))
              for m, gc in enumerate(gcs)]
    ls = [jnp.where(strict, kk * d, 0.0) for kk, d in zip(kks, decays)]
    attns = [qk * d for qk, d in zip(qks, decays)]
    ps = _block_inverse_many(ls, ri, ci, bs)
    egs = [jnp.exp(gc) for gc in gcs]
    rhss = [jnp.concatenate([vx * bc, kb * eg], axis=-1) for vx, bc, kb, eg in zip(vxs, bcs, kbs, egs)]
    sols = [rhs + _dot(p - eye, rhs) for rhs, p in zip(rhss, ps)]
    qes = [qx * eg for qx, eg in zip(qxs, egs)]

    heads = [(m, i, j * hg + hh, slice(hh * bs, (hh + 1) * bs))
             for m, (i, j) in enumerate(items) for hh in range(hg)]
    wqs = [_dot(jnp.concatenate([sols[m][rs, GDN_D:], qes[m][rs, :]], axis=0), s_ref[i, h])
           for m, i, h, rs in heads]
    v_news, qss = [], []
    for m in range(n_items):
        part = wqs[m * hg:(m + 1) * hg]
        v_news.append(sols[m][:, :GDN_D] - jnp.concatenate([r[:bs] for r in part], axis=0))
        qss.append(jnp.concatenate([r[bs:] for r in part], axis=0))
    outs = [qs + _dot(attn, vn) for qs, attn, vn in zip(qss, attns, v_news)]

    def last_rows(gc):
        return [gc[hh * bs + c_real - 1:hh * bs + c_real, :] for hh in range(hg)]
    kds = [kx * jnp.exp(jnp.concatenate([jnp.broadcast_to(r, (bs, GROUP_ROWS)) for r in last_rows(gc)],
                                        axis=0) - gc)
           for kx, gc in zip(kxs, gcs)]
    upds = [_dot_tn(kds[m][rs, :], v_news[m][rs, :]) for m, i, h, rs in heads]
    for (m, i, h, rs), upd in zip(heads, upds):
        g_last = gcs[m][rs.start + c_real - 1:rs.start + c_real, :]
        s_ref[i, h] = s_ref[i, h] * jnp.exp(g_last) + upd

    nw = nw_ref[...]
    for m, i, h, rs in heads:
        o = outs[m][rs, :]
        o = o * lax.rsqrt(jnp.mean(o * o, axis=-1, keepdims=True) + EPS) * nw
        gate = _silu(z_ref[i, :, h * GDN_D:(h + 1) * GDN_D].astype(F32))
        o_ref[i, :, h * GDN_D:(h + 1) * GDN_D] = (o * gate).astype(o_ref.dtype)


def _gdn_core(proj3, gates, a_log, dt_bias, norm_w, s0, c_real, cp, bb):
    B, Tp, _ = proj3.shape
    key = GDN_K_HEADS * GDN_D
    vd = GDN_V_HEADS * GDN_D
    cdim = 2 * key + vd
    nc = Tp // cp
    hg = GROUP_ROWS // cp
    ng = GDN_V_HEADS // hg
    has_state = s0 is not None
    lanes = lambda a: jnp.tile(jnp.repeat(a.reshape(ng, hg), cp, axis=1), (bb, 1))
    full = lambda shape: pl.BlockSpec(shape, lambda b, n: (0,) * len(shape))
    in_specs = [pl.BlockSpec((bb, cp, cdim), lambda b, n: (b, n, 0)),
                pl.BlockSpec((bb, cp, vd), lambda b, n: (b, n, cdim // vd)),
                pl.BlockSpec((None, 2, bb * ng, GROUP_ROWS), lambda b, n: (b * nc + n, 0, 0, 0)),
                full((bb * ng, GROUP_ROWS)), full((bb * ng, GROUP_ROWS)), full((1, GDN_D))]
    args = [proj3, proj3, gates, lanes(a_log), lanes(dt_bias), norm_w.reshape(1, GDN_D)]
    s_spec = pl.BlockSpec((bb, GDN_V_HEADS, GDN_D, GDN_D), lambda b, n: (b, 0, 0, 0))
    if has_state:
        in_specs.append(s_spec)
        args.append(s0)
    return pl.pallas_call(
        functools.partial(_gdn_core_kernel, c_real=c_real, has_state=has_state),
        grid=(B // bb, nc),
        in_specs=in_specs,
        out_specs=[pl.BlockSpec((bb, cp, vd), lambda b, n: (b, n, 0)), s_spec],
        out_shape=[jax.ShapeDtypeStruct((B, Tp, vd), proj3.dtype),
                   jax.ShapeDtypeStruct((B, GDN_V_HEADS, GDN_D, GDN_D), F32)],
        compiler_params=_cparams("parallel", "arbitrary"),
        name="gdn_core",
    )(*args)


def _rope_tables(pos0, t_real, t_pad):
    half = RET_DK // 2
    inv_freq = ROPE_BASE ** (-np.arange(half, dtype=np.float64) / half)
    pos = pos0 + np.arange(t_pad, dtype=np.float64)
    ang = pos[:, None] * inv_freq[None, :]
    live = (np.arange(t_pad) < t_real)[:, None]
    return (jnp.asarray(np.where(live, np.cos(ang), 0.0), F32),
            jnp.asarray(np.where(live, np.sin(ang), 0.0), F32))


def _trunk(x, mods, mod_final, pos0, s_ret, s_gdn, s_gconv, s_fconv, p, wb, tiles, act_dtype):
    B, T, D = x.shape
    Tp = tiles.Tp
    x3 = x if T == Tp else jnp.pad(x, ((0, 0), (0, Tp - T), (0, 0)))
    M = B * Tp
    stateful = s_ret is not None
    new_fconv = []

    mod_mix, mod_ffn = mods[0][:, None, :2 * D], mods[0][:, None, 2 * D:]
    cos, sin = _rope_tables(pos0, T, Tp)
    proj = _ret_proj(tiles, x3, p['norm_mix'][0], mod_mix, wb['ret_in'], cos, sin, act_dtype)
    cp = min(RET_CHUNK, Tp)
    o3, new_ret = _ret_core(proj.reshape(B, Tp, -1), s_ret[0] if stateful else None, min(T, cp), cp)
    x3, cst = _ffn(tiles, o3.reshape(M, -1), wb['ret_out'], x3, p['norm_ffn'][0], mod_ffn,
                   wb['ffn_up'][0], p['w_ffn_dw'][0], p['b_ffn_dw'][0], wb['ffn_down'][0],
                   s_fconv[0] if stateful else None, min(T, tiles.tt))
    new_fconv.append(cst)

    mod_mix, mod_ffn = mods[1][:, None, :2 * D], mods[1][:, None, 2 * D:]
    cp = min(GDN_CHUNK, Tp)
    nc = Tp // cp
    ng = GDN_V_HEADS * cp // GROUP_ROWS
    bb = min(B, max(SUBLANES // ng, GDN_SEQS_PER_STEP))
    proj, bat, new_gconv = _gdn_proj(tiles, x3, p['norm_mix'][1], mod_mix, wb['gdn_in'], wb['gdn_tail_t'],
                                     p['w_gdn_conv'][0], s_gconv[0] if stateful else None,
                                     min(T, tiles.tt), act_dtype)
    gates = bat.reshape(2, ng, GROUP_ROWS // cp, B // bb, bb, nc, cp)
    gates = gates.transpose(3, 5, 0, 4, 1, 2, 6).reshape(B // bb * nc, 2, bb * ng, GROUP_ROWS)
    o3, new_gdn = _gdn_core(proj.reshape(B, Tp, -1), gates, p['gdn_a_log'][0], p['gdn_dt_bias'][0],
                            p['gdn_norm'][0], s_gdn[0] if stateful else None, min(T, cp), cp, bb)
    y3, cst = _ffn(tiles, o3.reshape(M, -1), wb['gdn_out'], x3, p['norm_ffn'][1], mod_ffn,
                   wb['ffn_up'][1], p['w_ffn_dw'][1], p['b_ffn_dw'][1], wb['ffn_down'][1],
                   s_fconv[1] if stateful else None, min(T, tiles.tt),
                   final=(p['norm_final'], mod_final[:, None, :]))
    new_fconv.append(cst)
    return (y3[:, :T], new_ret[None], new_gdn[None], new_gconv[None], jnp.stack(new_fconv))


def kernel(x_prompt, x_sample, state_ret, state_gdn, state_gdn_conv, state_ffn_conv, c_prompt, c_sample, w_ada, b_ada, w_ada_final, b_ada_final, norm_mix, norm_ffn, norm_final, w_ret_in, w_ret_out, w_gdn_in, w_gdn_conv, gdn_a_log, gdn_dt_bias, gdn_norm, w_gdn_out, w_ffn_up, w_ffn_dw, b_ffn_dw, w_ffn_down):
    p = {'norm_mix': norm_mix, 'norm_ffn': norm_ffn, 'norm_final': norm_final,
         'w_gdn_conv': w_gdn_conv, 'gdn_a_log': gdn_a_log, 'gdn_dt_bias': gdn_dt_bias,
         'gdn_norm': gdn_norm, 'w_ffn_dw': w_ffn_dw, 'b_ffn_dw': b_ffn_dw}
    cdim = (2 * GDN_K_HEADS + GDN_V_HEADS) * GDN_D
    vd = GDN_V_HEADS * GDN_D
    w_gdn_in_t = jnp.swapaxes(w_gdn_in, 1, 2)
    wb = {'ret_in': _layer_bf16(w_ret_in, 0), 'ret_out': _layer_bf16(w_ret_out, 0),
          'gdn_in': _layer_bf16(w_gdn_in_t, 0, transposed_cols=cdim + vd),
          'gdn_out': _layer_bf16(w_gdn_out, 0), 'gdn_tail_t': w_gdn_in_t[0, cdim + vd:, :],
          'ffn_up': [_layer_bf16(w_ffn_up, l) for l in range(2)],
          'ffn_down': [_layer_bf16(w_ffn_down, l) for l in range(2)]}

    bp, tp = x_prompt.shape[:2]
    bs_ = x_sample.shape[0]
    c_all = jnp.concatenate([c_prompt, c_sample], axis=0)
    mods = _ada_mod(c_all, w_ada, b_ada)
    mod_final = _ada_mod(c_all, w_ada_final[None], b_ada_final[None])[0]

    out_p = _trunk(x_prompt, mods[:, :bp], mod_final[:bp], 0, None, None, None, None, p, wb,
                   _Tiles(bp, tp, 1, min(PROMPT_TILE_ROWS, tp)), BF16)
    out_s = _trunk(x_sample, mods[:, bp:], mod_final[bp:], PAST_LEN, state_ret, state_gdn,
                   state_gdn_conv, state_ffn_conv, p, wb,
                   _Tiles(bs_, SUBLANES, min(SHORT_TILE_SEQS, bs_), SUBLANES), BF16)
    y_p, ret_p, gdn_p, gconv_p, fconv_p = out_p
    y_s, ret_s, gdn_s, gconv_s, fconv_s = out_s
    return (y_p, y_s, ret_p, gdn_p, gconv_p, fconv_p, ret_s, gdn_s, gconv_s, fconv_s)
```

```python
import functools
import math

import numpy as np
import jax
import jax.numpy as jnp
from jax import lax
from jax.experimental import pallas as pl
from jax.experimental.pallas import tpu as pltpu

F32 = jnp.float32
BF16 = jnp.bfloat16
EPS = 1e-6
ROPE_BASE = 10000.0
PAST_LEN = 16384

RET_HEADS = 4
RET_DK = 256
RET_DV = 512
RET_CHUNK = 256
GDN_K_HEADS = 8
GDN_V_HEADS = 16
GDN_D = 128
GDN_CHUNK = 64
GDN_CONV_W = 4
FFN_CONV_W = 3
SUBLANES = 8
LANES = 128
INV_BASE = 8
GROUP_ROWS = 128
GDN_SEQS_PER_STEP = 4
RET_SEQS_PER_STEP = 2
PROJ_COLS = 512
FFN_COLS = 256
PROMPT_TILE_ROWS = 512
SHORT_TILE_SEQS = 32
VMEM_LIMIT = 48 * 1024 * 1024
CAST_BLOCK_BYTES = 3 * 1024 * 1024

_HIGHEST = lax.Precision.HIGHEST


def _cparams(*sem):
    return pltpu.CompilerParams(dimension_semantics=sem, vmem_limit_bytes=VMEM_LIMIT)


def _silu_of_half(hx):
    return hx + hx * jnp.tanh(hx)


def _silu(x):
    return _silu_of_half(0.5 * x)


def _softplus(x):
    return jnp.maximum(x, 0.0) + jnp.log1p(jnp.exp(-jnp.abs(x)))


def _ada_norm(x, gamma, shift, scale):
    ms = jnp.mean(x * x, axis=-1, keepdims=True)
    xn = x * lax.rsqrt(ms + EPS) * gamma
    return xn * (1.0 + scale) + shift


def _dot(a, b):
    return jnp.dot(a.astype(BF16), b.astype(BF16), preferred_element_type=F32)


def _dot_nt(a, b):
    return lax.dot_general(a.astype(BF16), b.astype(BF16), (((1,), (1,)), ((), ())),
                           preferred_element_type=F32)


def _dot_tn(a, b):
    return lax.dot_general(a.astype(BF16), b.astype(BF16), (((0,), (0,)), ((), ())),
                           preferred_element_type=F32)


def _cast_kernel(w_ref, o_ref, *, transpose):
    w = w_ref[...]
    o_ref[...] = (w.T if transpose else w).astype(o_ref.dtype)


def _layer_bf16(w, layer, transposed_cols=None):
    tiled = w.shape[1] if transposed_cols is None else transposed_cols
    rows = max(r for r in range(16, tiled + 1, 16)
               if tiled % r == 0 and r * w.shape[2] * 4 <= CAST_BLOCK_BYTES)
    if transposed_cols is None:
        _, K, N = w.shape
        grid, in_block, out_block = K // rows, (None, rows, N), (rows, N)
        in_map, out_map = (lambda i: (layer, i, 0)), (lambda i: (i, 0))
    else:
        K, N = w.shape[2], transposed_cols
        grid, in_block, out_block = N // rows, (None, rows, K), (K, rows)
        in_map, out_map = (lambda i: (layer, i, 0)), (lambda i: (0, i))
    return pl.pallas_call(
        functools.partial(_cast_kernel, transpose=transposed_cols is not None),
        grid=(grid,),
        in_specs=[pl.BlockSpec(in_block, in_map)],
        out_specs=pl.BlockSpec(out_block, out_map),
        out_shape=jax.ShapeDtypeStruct((K, N), BF16),
        compiler_params=_cparams("parallel"),
        name="cast_bf16",
    )(w)


def _ada_kernel(c_ref, w_ref, b_ref, o_ref):
    cs = _silu(c_ref[...])
    o_ref[...] = _dot(cs, w_ref[...]) + b_ref[...]


def _ada_mod(c, w, b, tn=2048):
    L, D, N = w.shape
    Mc = c.shape[0]
    return pl.pallas_call(
        _ada_kernel,
        grid=(L, N // tn),
        in_specs=[pl.BlockSpec((Mc, D), lambda l, j: (0, 0)),
                  pl.BlockSpec((None, D, tn), lambda l, j: (l, 0, j)),
                  pl.BlockSpec((None, 1, tn), lambda l, j: (l, 0, j))],
        out_specs=pl.BlockSpec((None, Mc, tn), lambda l, j: (l, 0, j)),
        out_shape=jax.ShapeDtypeStruct((L, Mc, N), F32),
        compiler_params=_cparams("parallel", "parallel"),
        name="ada_mod",
    )(c, w, b.reshape(L, 1, N))


class _Tiles:
    def __init__(self, B, Tp, bt, tt):
        assert bt == 1 or tt == Tp
        self.B, self.Tp, self.bt, self.tt = B, Tp, bt, tt
        self.tps = Tp // tt
        self.n = (B // bt) * self.tps
        self.tm = bt * tt

    def x_spec(self, D):
        tps = self.tps
        return pl.BlockSpec((self.bt, self.tt, D), lambda i: (i // tps, i % tps, 0))

    def seq_spec(self, r, D):
        tps = self.tps
        return pl.BlockSpec((self.bt, r, D), lambda i: (i // tps, 0, 0))

    def rows_spec(self, N):
        return pl.BlockSpec((self.tm, N), lambda i: (i, 0))

    def pos_spec(self, N):
        tps = self.tps
        return pl.BlockSpec((self.tt, N), lambda i: (i % tps, 0))

    def tail_rows_spec(self, r, N):
        return pl.BlockSpec((self.bt, r, N), lambda i: (i, 0, 0))

    def last_tile(self, a):
        return a[self.tps - 1::self.tps]


def _resident(shape):
    return pl.BlockSpec(shape, lambda i: (0,) * len(shape), pipeline_mode=pl.Buffered(1))


def _mod(m3, k, D):
    return m3[:, :, k * D:(k + 1) * D]


def _normed_rows(x_ref, gam_ref, mod_ref, k_shift, h_ref):
    D = x_ref.shape[-1]
    m3 = mod_ref[...]
    h3 = _ada_norm(x_ref[...], gam_ref[...], _mod(m3, k_shift, D), _mod(m3, k_shift + 1, D))
    h_ref[...] = h3.reshape(h_ref.shape).astype(BF16)
    return h3


def _causal_taps(g3, prev_ref, sl, width):
    ext = jnp.concatenate([prev_ref[:, :, sl], g3], axis=1)
    return [pltpu.roll(ext, k, axis=1)[:, SUBLANES:, :] for k in range(width - 1, 0, -1)]


def _start_of_sequence(tiles):
    return (pl.program_id(0) % tiles.tps) == 0


def _ret_proj_kernel(x_ref, gam_ref, mod_ref, w_ref, cos_ref, sin_ref, o_ref, h_ref):
    bt, tt, _ = x_ref.shape
    _normed_rows(x_ref, gam_ref, mod_ref, 0, h_ref)
    cos, sin = cos_ref[...], sin_ref[...]
    qk = RET_HEADS * RET_DK
    half = RET_DK // 2
    for j in range(w_ref.shape[1] // PROJ_COLS):
        sl = slice(j * PROJ_COLS, (j + 1) * PROJ_COLS)
        y = jnp.dot(h_ref[...], w_ref[:, sl], preferred_element_type=F32)
        if sl.start < 2 * qk:
            y3 = y.reshape(bt, tt, PROJ_COLS)
            parts = []
            for c in range(PROJ_COLS // RET_DK):
                x1 = y3[:, :, c * RET_DK:c * RET_DK + half]
                x2 = y3[:, :, c * RET_DK + half:(c + 1) * RET_DK]
                parts += [x1 * cos - x2 * sin, x1 * sin + x2 * cos]
            y = jnp.concatenate(parts, axis=-1).reshape(bt * tt, PROJ_COLS)
            if sl.start >= qk:
                y = y * (RET_DK ** -0.5)
        o_ref[:, sl] = y.astype(o_ref.dtype)


def _ret_proj(tiles, x3, gamma, mod3, w, cos, sin, out_dtype):
    B, Tp, D = x3.shape
    N = w.shape[1]
    return pl.pallas_call(
        _ret_proj_kernel,
        grid=(tiles.n,),
        in_specs=[tiles.x_spec(D), _resident((1, D)), tiles.seq_spec(1, mod3.shape[-1]),
                  _resident((D, N)), tiles.pos_spec(cos.shape[1]), tiles.pos_spec(sin.shape[1])],
        out_specs=tiles.rows_spec(N),
        out_shape=jax.ShapeDtypeStruct((B * Tp, N), out_dtype),
        scratch_shapes=[pltpu.VMEM((tiles.tm, D), BF16)],
        compiler_params=_cparams("parallel"),
        name="ret_proj",
    )(x3, gamma.reshape(1, D), mod3, w, cos, sin)


def _gdn_proj_kernel(*refs, tiles, c_real, has_state):
    it = iter(refs)
    x_ref, gam_ref, mod_ref, w_ref, wtt_ref, wc_ref = (next(it) for _ in range(6))
    if has_state:
        cs_ref = next(it)
    o_ref, ott_ref, cst_ref = next(it), next(it), next(it)
    h_ref, prev_ref = next(it), next(it)
    bt, tt, _ = x_ref.shape
    key = GDN_K_HEADS * GDN_D
    cdim = wc_ref.shape[1]
    keep = GDN_CONV_W - 1

    h3 = _normed_rows(x_ref, gam_ref, mod_ref, 0, h_ref)
    nt = ott_ref.shape[0]
    h_hi = h_ref[...]
    h_lo = (h3.reshape(bt * tt, -1) - h_hi.astype(F32)).astype(BF16)
    w_hl = wtt_ref[...]
    a = _dot_nt(w_hl, h_hi)
    ott_ref[...] = a[:nt] + a[nt:] + _dot_nt(w_hl[:nt], h_lo)
    if has_state:
        prev_ref[...] = jnp.zeros_like(prev_ref)
        prev_ref[:, SUBLANES - keep:, :] = cs_ref[...]
    else:
        @pl.when(_start_of_sequence(tiles))
        def _():
            prev_ref[:, 0:SUBLANES, :] = jnp.zeros((prev_ref.shape[0], SUBLANES, LANES), F32)

    def head_norm(a, col):
        if col >= 2 * key:
            return a
        inv = lax.rsqrt(jnp.sum(a * a, axis=-1, keepdims=True) + EPS)
        return a * (inv * (GDN_D ** -0.5) if col < key else inv)

    def conv_by_phase(col):
        slab = col // LANES
        n8 = tt // SUBLANES
        cur =[prev_ref[slab, pl.ds(SUBLANES + b, n8, stride=SUBLANES), :] for b in range(SUBLANES)]
        old = {b: prev_ref[slab, pl.ds(b, n8, stride=SUBLANES), :]
               for b in range(SUBLANES - keep, SUBLANES)}
        wch = 0.5 * wc_ref[:, col:col + LANES]
        outs = []
        for b in range(SUBLANES):
            acc = wch[keep:keep + 1, :] * cur[b]
            for k in range(1, keep + 1):
                src = cur[b - k] if b >= k else old[b - k + SUBLANES]
                acc = acc + wch[keep - k:keep - k + 1, :] * src
            outs.append(head_norm(_silu_of_half(acc), col))
        cst_ref[0, :, col:col + LANES] = prev_ref[slab, SUBLANES + c_real - keep:SUBLANES + c_real, :]
        prev_ref[slab, 0:SUBLANES, :] = prev_ref[slab, tt:tt + SUBLANES, :]
        for b in range(SUBLANES):
            prev_ref[slab, pl.ds(SUBLANES + b, n8, stride=SUBLANES), :] = outs[b]
        o_ref[:, col:col + LANES] = prev_ref[slab, SUBLANES:, :].astype(o_ref.dtype)

    for j in range(o_ref.shape[1] // PROJ_COLS):
        sl = slice(j * PROJ_COLS, (j + 1) * PROJ_COLS)
        y = jnp.dot(h_ref[...], w_ref[:, sl], preferred_element_type=F32)
        if sl.start < cdim and not has_state:
            for c in range(0, PROJ_COLS, LANES):
                prev_ref[(sl.start + c) // LANES, SUBLANES:, :] = y[:, c:c + LANES]
            for c in range(0, PROJ_COLS, LANES):
                conv_by_phase(sl.start + c)
            continue
        if sl.start < cdim:
            y3 = y.reshape(bt, tt, PROJ_COLS)
            taps = _causal_taps(y3, prev_ref, sl, GDN_CONV_W)
            wch = 0.5 * wc_ref[:, sl]
            acc = wch[keep:keep + 1, :] * y3
            for i, tap in enumerate(taps):
                acc = acc + wch[i:i + 1, :] * tap
            acc = _silu_of_half(acc)
            acc = jnp.concatenate([head_norm(acc[:, :, c:c + GDN_D], sl.start + c)
                                   for c in range(0, PROJ_COLS, GDN_D)], axis=-1)
            cst_ref[:, :, sl] = y3[:, c_real - keep:c_real, :]
            y = acc.reshape(bt * tt, PROJ_COLS)
        o_ref[:, sl] = y.astype(o_ref.dtype)


def _gdn_proj(tiles, x3, gamma, mod3, w, w_tail_t, w_conv, conv_state, c_real, out_dtype):
    B, Tp, D = x3.shape
    nt = w_tail_t.shape[0]
    cdim = w_conv.shape[1]
    N = w.shape[1]
    keep = GDN_CONV_W - 1
    has_state = conv_state is not None
    wt_hi = w_tail_t.astype(BF16)
    wt_lo = (w_tail_t - wt_hi.astype(F32)).astype(BF16)
    in_specs = [tiles.x_spec(D), _resident((1, D)), tiles.seq_spec(1, mod3.shape[-1]),
                _resident(w.shape), _resident((2 * nt, D)), _resident((GDN_CONV_W, cdim))]
    args = [x3, gamma.reshape(1, D), mod3, w, jnp.concatenate([wt_hi, wt_lo], axis=0), w_conv]
    if has_state:
        in_specs.append(tiles.seq_spec(keep, cdim))
        args.append(conv_state)
    proj, ott, cst = pl.pallas_call(
        functools.partial(_gdn_proj_kernel, tiles=tiles, c_real=c_real, has_state=has_state),
        grid=(tiles.n,),
        in_specs=in_specs,
        out_specs=[tiles.rows_spec(N), pl.BlockSpec((nt, tiles.tm), lambda i: (0, i)),
                   tiles.tail_rows_spec(keep, cdim)],
        out_shape=[jax.ShapeDtypeStruct((B * Tp, N), out_dtype),
                   jax.ShapeDtypeStruct((nt, B * Tp), F32),
                   jax.ShapeDtypeStruct((tiles.n * tiles.bt, keep, cdim), F32)],
        scratch_shapes=[pltpu.VMEM((tiles.tm, D), BF16),
                        pltpu.VMEM((tiles.bt, SUBLANES, cdim) if has_state
                                   else (cdim // LANES, SUBLANES + tiles.tt, LANES), F32)],
        compiler_params=_cparams("arbitrary"),
        name="gdn_proj",
    )(*args)
    return proj, ott, tiles.last_tile(cst)


def _ffn_kernel(*refs, tiles, c_real, has_state, final):
    it = iter(refs)
    o_ref, wo_ref, x_ref, gam_ref, mod_ref = (next(it) for _ in range(5))
    wup_ref, wdw_ref, bdw_ref, wd_ref = (next(it) for _ in range(4))
    if has_state:
        cs_ref = next(it)
    if final:
        gamf_ref, modf_ref = next(it), next(it)
    y_ref, cst_ref = next(it), next(it)
    xm_ref, h_ref, act_ref, prev_ref = (next(it) for _ in range(4))
    bt, tt, D = x_ref.shape
    fd = wd_ref.shape[0]
    keep = FFN_CONV_W - 1

    xm_ref[...] = x_ref[...] + _mod(mod_ref[...], 0, D) * _dot(o_ref[...], wo_ref[...]).reshape(bt, tt, D)
    _normed_rows(xm_ref, gam_ref, mod_ref, 1, h_ref)
    if has_state:
        prev_ref[...] = jnp.zeros_like(prev_ref)
        prev_ref[:, SUBLANES - keep:, :] = cs_ref[...]
    else:
        @pl.when(_start_of_sequence(tiles))
        def _():
            prev_ref[...] = jnp.zeros_like(prev_ref)

    for f in range(fd // FFN_COLS):
        sl = slice(f * FFN_COLS, (f + 1) * FFN_COLS)
        h = h_ref[...]
        g3 = jnp.dot(h, wup_ref[:, sl], preferred_element_type=F32).reshape(bt, tt, FFN_COLS)
        val = jnp.dot(h, wup_ref[:, fd + sl.start:fd + sl.stop], preferred_element_type=F32)
        s2, s1 = _causal_taps(g3, prev_ref, sl, FFN_CONV_W)
        wh = 0.5 * wdw_ref[:, sl]
        conv = wh[0:1, :] * s2 + wh[1:2, :] * s1 + wh[2:3, :] * g3 + 0.5 * bdw_ref[:, sl]
        cst_ref[:, :, sl] = g3[:, c_real - keep:c_real, :]
        if not has_state:
            prev_ref[:, :, sl] = g3[:, tt - SUBLANES:, :]
        act_ref[:, sl] = (_silu_of_half(conv).reshape(bt * tt, FFN_COLS) * val).astype(BF16)

    acc = jnp.dot(act_ref[...], wd_ref[...], preferred_element_type=F32).reshape(bt, tt, D)
    xn = xm_ref[...] + _mod(mod_ref[...], 3, D) * acc
    if final:
        mf = modf_ref[...]
        xn = _ada_norm(xn, gamf_ref[...], _mod(mf, 0, D), _mod(mf, 1, D))
    y_ref[...] = xn


def _ffn(tiles, o, w_out, x3, gamma, mod3, w_up, w_dw, b_dw, w_down, conv_state, c_real, final=None):
    B, Tp, D = x3.shape
    Fd = w_down.shape[0]
    K = o.shape[1]
    keep = FFN_CONV_W - 1
    has_state = conv_state is not None
    in_specs = [tiles.rows_spec(K), _resident((K, D)),
                tiles.x_spec(D), _resident((1, D)), tiles.seq_spec(1, mod3.shape[-1]),
                _resident((D, 2 * Fd)), _resident((FFN_CONV_W, Fd)), _resident((1, Fd)),
                _resident((Fd, D))]
    args = [o, w_out, x3, gamma.reshape(1, D), mod3, w_up, w_dw, b_dw.reshape(1, Fd), w_down]
    if has_state:
        in_specs.append(tiles.seq_spec(keep, Fd))
        args.append(conv_state)
    if final is not None:
        in_specs += [_resident((1, D)), tiles.seq_spec(1, final[1].shape[-1])]
        args += [final[0].reshape(1, D), final[1]]
    y3, cst = pl.pallas_call(
        functools.partial(_ffn_kernel, tiles=tiles, c_real=c_real, has_state=has_state,
                          final=final is not None),
        grid=(tiles.n,),
        in_specs=in_specs,
        out_specs=[tiles.x_spec(D), tiles.tail_rows_spec(keep, Fd)],
        out_shape=[jax.ShapeDtypeStruct((B, Tp, D), F32),
                   jax.ShapeDtypeStruct((tiles.n * tiles.bt, keep, Fd), F32)],
        scratch_shapes=[pltpu.VMEM((tiles.bt, tiles.tt, D), F32), pltpu.VMEM((tiles.tm, D), BF16),
                        pltpu.VMEM((tiles.tm, Fd), BF16), pltpu.VMEM((tiles.bt, SUBLANES, Fd), F32)],
        compiler_params=_cparams("arbitrary"),
        name="out_proj_ffn",
    )(*args)
    return y3, tiles.last_tile(cst)


def _ret_core_kernel(*refs, c_real, has_state):
    it = iter(refs)
    q_ref, k_ref, v_ref, g_ref = (next(it) for _ in range(4))
    if has_state:
        s0_ref = next(it)
    o_ref, s_ref = next(it), next(it)
    nb, cp, _ = q_ref.shape

    @pl.when(pl.program_id(1) == 0)
    def _():
        if has_state:
            s_ref[...] = s0_ref[...]
        else:
            s_ref[...] = jnp.zeros_like(s_ref)

    ri = lax.broadcasted_iota(jnp.int32, (cp, cp), 0)
    ci = lax.broadcasted_iota(jnp.int32, (cp, cp), 1)
    causal = ri >= ci
    diff = jnp.where(causal, ri - ci, 0).astype(F32)
    rowi = lax.broadcasted_iota(jnp.int32, (cp, 1), 0)
    row = rowi.astype(F32)
    lgs = [math.log(1.0 - 2.0 ** (-5.0 - h)) for h in range(RET_HEADS)]
    decays = [jnp.where(causal, jnp.exp(diff * lg), 0.0) for lg in lgs]
    k_decays = [jnp.where(rowi < c_real, jnp.exp((c_real - 1.0 - row) * lg), 0.0) for lg in lgs]
    q_decays = [jnp.exp((row + 1.0) * lg) for lg in lgs]

    items = [(i, h) for i in range(nb) for h in range(RET_HEADS)]
    qs = [q_ref[i, :, h * RET_DK:(h + 1) * RET_DK] for i, h in items]
    ks = [k_ref[i, :, h * RET_DK:(h + 1) * RET_DK].astype(F32) for i, h in items]
    vs = [v_ref[i, :, h * RET_DV:(h + 1) * RET_DV] for i, h in items]
    ss = [s_ref[i, h] for i, h in items]
    scores = [_dot_nt(q, k) for q, k in zip(qs, ks)]
    cross = [_dot(q, s) for q, s in zip(qs, ss)]
    upds = [_dot_tn(k * k_decays[h], v) for k, v, (i, h) in zip(ks, vs, items)]
    inner = [_dot(sc * decays[h], v) for sc, v, (i, h) in zip(scores, vs, items)]
    for m, (i, h) in enumerate(items):
        s_ref[i, h] = ss[m] * math.exp(c_real * lgs[h]) + upds[m]
        o = inner[m] + cross[m] * q_decays[h]
        o = o * lax.rsqrt(jnp.mean(o * o, axis=-1, keepdims=True) + EPS)
        gate = _silu(g_ref[i, :, h * RET_DV:(h + 1) * RET_DV].astype(F32))
        o_ref[i, :, h * RET_DV:(h + 1) * RET_DV] = (o * gate).astype(o_ref.dtype)


def _ret_core(proj3, s0, c_real, cp):
    B, Tp, _ = proj3.shape
    qk = RET_HEADS * RET_DK
    vd = RET_HEADS * RET_DV
    has_state = s0 is not None
    nb = min(B, RET_SEQS_PER_STEP * (2 if Tp == cp and cp <= SUBLANES else 1))
    in_specs = [pl.BlockSpec((nb, cp, qk), lambda b, n: (b, n, 0)),
                pl.BlockSpec((nb, cp, qk), lambda b, n: (b, n, 1)),
                pl.BlockSpec((nb, cp, vd), lambda b, n: (b, n, 1)),
                pl.BlockSpec((nb, cp, vd), lambda b, n: (b, n, 2))]
    args = [proj3, proj3, proj3, proj3]
    s_spec = pl.BlockSpec((nb, RET_HEADS, RET_DK, RET_DV), lambda b, n: (b, 0, 0, 0))
    if has_state:
        in_specs.append(s_spec)
        args.append(s0)
    return pl.pallas_call(
        functools.partial(_ret_core_kernel, c_real=c_real, has_state=has_state),
        grid=(B // nb, Tp // cp),
        in_specs=in_specs,
        out_specs=[pl.BlockSpec((nb, cp, vd), lambda b, n: (b, n, 0)), s_spec],
        out_shape=[jax.ShapeDtypeStruct((B, Tp, vd), proj3.dtype),
                   jax.ShapeDtypeStruct((B, RET_HEADS, RET_DK, RET_DV), F32)],
        compiler_params=_cparams("parallel", "arbitrary"),
        name="ret_core",
    )(*args)


def _block_inverse_many(ls, ri, ci, bs):
    eye = (ri == ci).astype(F32)
    base = (ri // INV_BASE) == (ci // INV_BASE)
    pws = [-jnp.where(base, l, 0.0) for l in ls]
    ps = [eye + m for m in pws]
    span = 2
    while span < INV_BASE:
        pws = [_dot(pw, pw) for pw in pws]
        ps = [p + _dot(p, pw) for p, pw in zip(ps, pws)]
        span *= 2
    size = INV_BASE
    while size < bs:
        off = ((ri // (2 * size)) == (ci // (2 * size))) & ((ri // size) != (ci // size))
        xs = [_dot(jnp.where(off, l, 0.0), p) for l, p in zip(ls, ps)]
        ps = [p - _dot(p, x) for p, x in zip(ps, xs)]
        size *= 2
    return ps


def _gdn_core_kernel(*refs, c_real, has_state):
    it = iter(refs)
    qkv_ref, z_ref, gt_ref, al_ref, dt_ref, nw_ref = (next(it) for _ in range(6))
    if has_state:
        s0_ref = next(it)
    o_ref, s_ref = next(it), next(it)
    bb, cp, _ = qkv_ref.shape
    bs = cp
    hg = GROUP_ROWS // bs
    ng = GDN_V_HEADS // hg
    rep = GDN_V_HEADS // GDN_K_HEADS
    key = GDN_K_HEADS * GDN_D
    n_items = bb * ng
    assert n_items % SUBLANES == 0

    @pl.when(pl.program_id(1) == 0)
    def _():
        if has_state:
            s_ref[...] = s0_ref[...]
        else:
            s_ref[...] = jnp.zeros_like(s_ref)

    ri = lax.broadcasted_iota(jnp.int32, (GROUP_ROWS, GROUP_ROWS), 0)
    ci = lax.broadcasted_iota(jnp.int32, (GROUP_ROWS, GROUP_ROWS), 1)
    same = (ri // bs) == (ci // bs)
    incl = same & (ri >= ci)
    strict = same & (ri > ci)
    eye = (ri == ci).astype(F32)

    live = (lax.broadcasted_iota(jnp.int32, (1, GROUP_ROWS), 1) % bs) < c_real
    beta_rows = jnp.where(live, jax.nn.sigmoid(gt_ref[0]), 0.0)
    g_rows = jnp.where(live, -jnp.exp(al_ref[...]) * _softplus(gt_ref[1] + dt_ref[...]), 0.0)
    gsum_rows = jnp.dot(g_rows, (same & (ri <= ci)).astype(F32), precision=_HIGHEST,
                        preferred_element_type=F32)
    rows = [beta_rows, gsum_rows]
    assert len(rows) * n_items <= GROUP_ROWS
    if len(rows) * n_items < GROUP_ROWS:
        rows.append(jnp.zeros((GROUP_ROWS - len(rows) * n_items, GROUP_ROWS), F32))
    cols = jnp.concatenate(rows, axis=0).T

    def column(q, m):
        c = q * n_items + m
        return jnp.broadcast_to(cols[:, c:c + 1], (GROUP_ROWS, GROUP_ROWS))

    items = [(i, j) for i in range(bb) for j in range(ng)]

    def stack(i, j, col_of_head):
        return jnp.concatenate(
            [qkv_ref[i, :, col_of_head(j * hg + hh):col_of_head(j * hg + hh) + GDN_D].astype(F32)
             for hh in range(hg)], axis=0)

    qxs = [stack(i, j, lambda h: (h // rep) * GDN_D) for i, j in items]
    kxs = [stack(i, j, lambda h: key + (h // rep) * GDN_D) for i, j in items]
    vxs = [stack(i, j, lambda h: 2 * key + h * GDN_D) for i, j in items]
    bcs = [column(0, m) for m in range(n_items)]
    gcs = [column(1, m) for m in range(n_items)]
    kbs = [kx * bc for kx, bc in zip(kxs, bcs)]
    kqs = [_dot_nt(jnp.concatenate([kb, qx], axis=0), kx) for kb, qx, kx in zip(kbs, qxs, kxs)]
    kks = [r[:GROUP_ROWS] for r in kqs]
    qks = [r[GROUP_ROWS:] for r in kqs]
    decays = [jnp.exp(jnp.where(incl, gc - gsum_rows[m:m + 1, :], -jnp.inf))
              for m, gc in enumerate(gcs)]
    ls = [jnp.where(strict, kk * d, 0.0) for kk, d in zip(kks, decays)]
    attns = [qk * d for qk, d in zip(qks, decays)]
    ps = _block_inverse_many(ls, ri, ci, bs)
    egs = [jnp.exp(gc) for gc in gcs]
    rhss = [jnp.concatenate([vx * bc, kb * eg], axis=-1) for vx, bc, kb, eg in zip(vxs, bcs, kbs, egs)]
    sols = [rhs + _dot(p - eye, rhs) for rhs, p in zip(rhss, ps)]
    qes = [qx * eg for qx, eg in zip(qxs, egs)]

    heads = [(m, i, j * hg + hh, slice(hh * bs, (hh + 1) * bs))
             for m, (i, j) in enumerate(items) for hh in range(hg)]
    wqs = [_dot(jnp.concatenate([sols[m][rs, GDN_D:], qes[m][rs, :]], axis=0), s_ref[i, h])
           for m, i, h, rs in heads]
    v_news, qss = [], []
    for m in range(n_items):
        part = wqs[m * hg:(m + 1) * hg]
        v_news.append(sols[m][:, :GDN_D] - jnp.concatenate([r[:bs] for r in part], axis=0))
        qss.append(jnp.concatenate([r[bs:] for r in part], axis=0))
    outs = [qs + _dot(attn, vn) for qs, attn, vn in zip(qss, attns, v_news)]

    def last_rows(gc):
        return [gc[hh * bs + c_real - 1:hh * bs + c_real, :] for hh in range(hg)]
    kds = [kx * jnp.exp(jnp.concatenate([jnp.broadcast_to(r, (bs, GROUP_ROWS)) for r in last_rows(gc)],
                                        axis=0) - gc)
           for kx, gc in zip(kxs, gcs)]
    upds = [_dot_tn(kds[m][rs, :], v_news[m][rs, :]) for m, i, h, rs in heads]
    for (m, i, h, rs), upd in zip(heads, upds):
        g_last = gcs[m][rs.start + c_real - 1:rs.start + c_real, :]
        s_ref[i, h] = s_ref[i, h] * jnp.exp(g_last) + upd

    nw = nw_ref[...]
    for m, i, h, rs in heads:
        o = outs[m][rs, :]
        o = o * lax.rsqrt(jnp.mean(o * o, axis=-1, keepdims=True) + EPS) * nw
        gate = _silu(z_ref[i, :, h * GDN_D:(h + 1) * GDN_D].astype(F32))
        o_ref[i, :, h * GDN_D:(h + 1) * GDN_D] = (o * gate).astype(o_ref.dtype)


def _gdn_core(proj3, gates, a_log, dt_bias, norm_w, s0, c_real, cp, bb):
    B, Tp, _ = proj3.shape
    key = GDN_K_HEADS * GDN_D
    vd = GDN_V_HEADS * GDN_D
    cdim = 2 * key + vd
    nc = Tp // cp
    hg = GROUP_ROWS // cp
    ng = GDN_V_HEADS // hg
    has_state = s0 is not None
    lanes = lambda a: jnp.tile(jnp.repeat(a.reshape(ng, hg), cp, axis=1), (bb, 1))
    full = lambda shape: pl.BlockSpec(shape, lambda b, n: (0,) * len(shape))
    in_specs = [pl.BlockSpec((bb, cp, cdim), lambda b, n: (b, n, 0)),
                pl.BlockSpec((bb, cp, vd), lambda b, n: (b, n, cdim // vd)),
                pl.BlockSpec((None, 2, bb * ng, GROUP_ROWS), lambda b, n: (b * nc + n, 0, 0, 0)),
                full((bb * ng, GROUP_ROWS)), full((bb * ng, GROUP_ROWS)), full((1, GDN_D))]
    args = [proj3, proj3, gates, lanes(a_log), lanes(dt_bias), norm_w.reshape(1, GDN_D)]
    s_spec = pl.BlockSpec((bb, GDN_V_HEADS, GDN_D, GDN_D), lambda b, n: (b, 0, 0, 0))
    if has_state:
        in_specs.append(s_spec)
        args.append(s0)
    return pl.pallas_call(
        functools.partial(_gdn_core_kernel, c_real=c_real, has_state=has_state),
        grid=(B // bb, nc),
        in_specs=in_specs,
        out_specs=[pl.BlockSpec((bb, cp, vd), lambda b, n: (b, n, 0)), s_spec],
        out_shape=[jax.ShapeDtypeStruct((B, Tp, vd), proj3.dtype),
                   jax.ShapeDtypeStruct((B, GDN_V_HEADS, GDN_D, GDN_D), F32)],
        compiler_params=_cparams("parallel", "arbitrary"),
        name="gdn_core",
    )(*args)


def _rope_tables(pos0, t_real, t_pad):
    half = RET_DK // 2
    inv_freq = ROPE_BASE ** (-np.arange(half, dtype=np.float64) / half)
    pos = pos0 + np.arange(t_pad, dtype=np.float64)
    ang = pos[:, None] * inv_freq[None, :]
    live = (np.arange(t_pad) < t_real)[:, None]
    return (jnp.asarray(np.where(live, np.cos(ang), 0.0), F32),
            jnp.asarray(np.where(live, np.sin(ang), 0.0), F32))


def _trunk(x, mods, mod_final, pos0, s_ret, s_gdn, s_gconv, s_fconv, p, wb, tiles, act_dtype):
    B, T, D = x.shape
    Tp = tiles.Tp
    x3 = x if T == Tp else jnp.pad(x, ((0, 0), (0, Tp - T), (0, 0)))
    M = B * Tp
    stateful = s_ret is not None
    new_fconv = []

    mod_mix, mod_ffn = mods[0][:, None, :2 * D], mods[0][:, None, 2 * D:]
    cos, sin = _rope_tables(pos0, T, Tp)
    proj = _ret_proj(tiles, x3, p['norm_mix'][0], mod_mix, wb['ret_in'], cos, sin, act_dtype)
    cp = min(RET_CHUNK, Tp)
    o3, new_ret = _ret_core(proj.reshape(B, Tp, -1), s_ret[0] if stateful else None, min(T, cp), cp)
    x3, cst = _ffn(tiles, o3.reshape(M, -1), wb['ret_out'], x3, p['norm_ffn'][0], mod_ffn,
                   wb['ffn_up'][0], p['w_ffn_dw'][0], p['b_ffn_dw'][0], wb['ffn_down'][0],
                   s_fconv[0] if stateful else None, min(T, tiles.tt))
    new_fconv.append(cst)

    mod_mix, mod_ffn = mods[1][:, None, :2 * D], mods[1][:, None, 2 * D:]
    cp = min(GDN_CHUNK, Tp)
    nc = Tp // cp
    ng = GDN_V_HEADS * cp // GROUP_ROWS
    bb = min(B, max(SUBLANES // ng, GDN_SEQS_PER_STEP))
    proj, bat, new_gconv = _gdn_proj(tiles, x3, p['norm_mix'][1], mod_mix, wb['gdn_in'], wb['gdn_tail_t'],
                                     p['w_gdn_conv'][0], s_gconv[0] if stateful else None,
                                     min(T, tiles.tt), act_dtype)
    gates = bat.reshape(2, ng, GROUP_ROWS // cp, B // bb, bb, nc, cp)
    gates = gates.transpose(3, 5, 0, 4, 1, 2, 6).reshape(B // bb * nc, 2, bb * ng, GROUP_ROWS)
    o3, new_gdn = _gdn_core(proj.reshape(B, Tp, -1), gates, p['gdn_a_log'][0], p['gdn_dt_bias'][0],
                            p['gdn_norm'][0], s_gdn[0] if stateful else None, min(T, cp), cp, bb)
    y3, cst = _ffn(tiles, o3.reshape(M, -1), wb['gdn_out'], x3, p['norm_ffn'][1], mod_ffn,
                   wb['ffn_up'][1], p['w_ffn_dw'][1], p['b_ffn_dw'][1], wb['ffn_down'][1],
                   s_fconv[1] if stateful else None, min(T, tiles.tt),
                   final=(p['norm_final'], mod_final[:, None, :]))
    new_fconv.append(cst)
    return (y3[:, :T], new_ret[None], new_gdn[None], new_gconv[None], jnp.stack(new_fconv))


def kernel(x_prompt, x_sample, state_ret, state_gdn, state_gdn_conv, state_ffn_conv, c_prompt, c_sample, w_ada, b_ada, w_ada_final, b_ada_final, norm_mix, norm_ffn, norm_final, w_ret_in, w_ret_out, w_gdn_in, w_gdn_conv, gdn_a_log, gdn_dt_bias, gdn_norm, w_gdn_out, w_ffn_up, w_ffn_dw, b_ffn_dw, w_ffn_down):
    p = {'norm_mix': norm_mix, 'norm_ffn': norm_ffn, 'norm_final': norm_final,
         'w_gdn_conv': w_gdn_conv, 'gdn_a_log': gdn_a_log, 'gdn_dt_bias': gdn_dt_bias,
         'gdn_norm': gdn_norm, 'w_ffn_dw': w_ffn_dw, 'b_ffn_dw': b_ffn_dw}
    cdim = (2 * GDN_K_HEADS + GDN_V_HEADS) * GDN_D
    vd = GDN_V_HEADS * GDN_D
    w_gdn_in_t = jnp.swapaxes(w_gdn_in, 1, 2)
    wb = {'ret_in': _layer_bf16(w_ret_in, 0), 'ret_out': _layer_bf16(w_ret_out, 0),
          'gdn_in': _layer_bf16(w_gdn_in_t, 0, transposed_cols=cdim + vd),
          'gdn_out': _layer_bf16(w_gdn_out, 0), 'gdn_tail_t': w_gdn_in_t[0, cdim + vd:, :],
          'ffn_up': [_layer_bf16(w_ffn_up, l) for l in range(2)],
          'ffn_down': [_layer_bf16(w_ffn_down, l) for l in range(2)]}

    bp, tp = x_prompt.shape[:2]
    bs_ = x_sample.shape[0]
    c_all = jnp.concatenate([c_prompt, c_sample], axis=0)
    mods = _ada_mod(c_all, w_ada, b_ada)
    mod_final = _ada_mod(c_all, w_ada_final[None], b_ada_final[None])[0]

    out_p = _trunk(x_prompt, mods[:, :bp], mod_final[:bp], 0, None, None, None, None, p, wb,
                   _Tiles(bp, tp, 1, min(PROMPT_TILE_ROWS, tp)), BF16)
    out_s = _trunk(x_sample, mods[:, bp:], mod_final[bp:], PAST_LEN, state_ret, state_gdn,
                   state_gdn_conv, state_ffn_conv, p, wb,
                   _Tiles(bs_, SUBLANES, min(SHORT_TILE_SEQS, bs_), SUBLANES), BF16)
    y_p, ret_p, gdn_p, gconv_p, fconv_p = out_p
    y_s, ret_s, gdn_s, gconv_s, fconv_s = out_s
    return (y_p, y_s, ret_p, gdn_p, gconv_p, fconv_p, ret_s, gdn_s, gconv_s, fconv_s)
```
